```python
import math
import jax
import jax.numpy as jnp
from jax import lax
import numpy as np

D_MODEL = 1024
BATCH = 16
SEQ = 4096
DEPTH = 4

F32 = jnp.float32
D_HY = D_MODEL // 2
D_RW = D_MODEL // 2
RW_HEAD = 64
RW_HEADS = D_RW // RW_HEAD
HY_ORDER = 2
HY_EMB = 33
HY_BANDS = (HY_EMB - 1) // 2
HY_FH = 64
HY_INNER = 2
HY_TARGET = 1e-2
HY_SHORT_DECAY_PCT = 0.3
HY_LONG_DECAY_PCT = 1.5
R_DECAY = max(32, int(round(1.8 * D_MODEL ** 0.5 / 32)) * 32)
R_AAA = max(32, int(round(1.8 * D_MODEL ** 0.5 / 32)) * 32)
R_MV = max(32, int(round(1.3 * D_MODEL ** 0.5 / 32)) * 32)
R_GATE = max(32, int(round(0.6 * D_MODEL ** 0.8 / 32)) * 32)
N_EXPERTS = 16
EXPERT_FF = 2 * D_MODEL
EC_CAPACITY = 2
ALPHA = (2 * DEPTH) ** 0.25
BETA = (8 * DEPTH) ** -0.25
LN_EPS = 1e-5
GN_EPS = 64e-5
P_HY = 3 * D_HY
P_RW = 3 * D_RW + 2 * R_DECAY + 2 * R_AAA + R_GATE
P_IN = P_HY + P_RW + 2 * D_MODEL

kernel_name = 'hyena_rwkv7_ec_moe_deepnorm_encoder'


def layer_norm(x, g, b, eps=LN_EPS):
    xf = x.astype(F32)
    mu = jnp.mean(xf, -1, keepdims=True)
    var = jnp.mean(jnp.square(xf - mu), -1, keepdims=True)
    return ((xf - mu) * lax.rsqrt(var + eps) * g + b).astype(x.dtype)


def neighbours(z):
    zp = jnp.pad(z, ((0, 0), (1, 0), (0, 0)))[:, :-1]
    zn = jnp.pad(z, ((0, 0), (0, 1), (0, 0)))[:, 1:]
    return zp, zn


def hyena_filter_spectra(seq_len, w1, b1, w2, b2, w3, freq):
    t = jnp.linspace(0.0, 1.0, seq_len, dtype=F32)[:, None]
    w = 2.0 * math.pi * jnp.arange(seq_len, dtype=F32)[:, None] / seq_len
    f = jnp.linspace(1e-4, HY_BANDS - 1, HY_BANDS, dtype=F32)[None, :]
    z = jnp.concatenate([t, jnp.cos(f * w), -jnp.sin(f * w)], axis=-1)
    freq = freq.astype(F32)
    a = jnp.sin(freq * (z @ w1 + b1))
    for i in range(HY_INNER):
        a = jnp.sin(freq * (a @ w2[i] + b2[i]))
    filt = (a @ w3).astype(F32).reshape(seq_len, HY_ORDER, 2, D_HY)
    max_decay = math.log(HY_TARGET) / HY_SHORT_DECAY_PCT
    min_decay = math.log(HY_TARGET) / HY_LONG_DECAY_PCT
    deltas = jnp.linspace(min_decay, max_decay, D_HY, dtype=F32)
    filt = filt * jnp.exp(-t * jnp.abs(deltas))[:, None, None, :]
    filt = filt * lax.rsqrt(jnp.sum(filt * filt, axis=(0, 2), keepdims=True) + 1e-12)
    k_full = jnp.concatenate([filt[:, :, 0], filt[::-1, :, 1]], axis=0)
    return jnp.fft.rfft(k_full, axis=0)


def long_conv(u, kf, bias):
    seq_len = u.shape[1]
    uf = jnp.fft.rfft(u.astype(F32), n=2 * seq_len, axis=1)
    y = jnp.fft.irfft(uf * kf[None], n=2 * seq_len, axis=1)[:, :seq_len]
    return (y + u.astype(F32) * bias).astype(u.dtype)


def hyena_branch(p, conv_w, conv_b, kf, bias):
    zp, zn = neighbours(p)
    p = conv_w[0] * zp + conv_w[1] * p + conv_w[2] * zn + conv_b
    v, x1, x2 = jnp.split(p, 3, axis=-1)
    z = x1 * long_conv(v, kf[:, 0], bias[0])
    return x2 * long_conv(z, kf[:, 1], bias[1])


def wkv7_scan(r, w, k, v, kk, a, reverse):
    bsz, _, nh, hd = r.shape

    def step(state, inp):
        r_t, w_t, k_t, v_t, kk_t, a_t = inp
        sa = jnp.einsum('bhvk,bhk->bhv', state, kk_t)
        state = (state * w_t[:, :, None, :] - sa[..., None] * (kk_t * a_t)[:, :, None, :]
                 + v_t[..., None] * k_t[:, :, None, :])
        return state, jnp.einsum('bhvk,bhk->bhv', state, r_t)

    xs = tuple(jnp.moveaxis(t.astype(F32), 1, 0) for t in (r, w, k, v, kk, a))
    s0 = jnp.zeros((bsz, nh, hd, hd), F32)
    _, y = lax.scan(step, s0, xs, reverse=reverse)
    return jnp.moveaxis(y, 0, 1)


def rwkv7_branch(p, v_first, v_res, mu, w0, w2, a0, a2, g2, k_k, k_a, r_k, lnx_g, lnx_b):
    bsz, seq_len, _ = p.shape
    zp, zn = neighbours(p)
    p = p + mu[0] * (zp - p) + mu[1] * (zn - p)
    c1 = D_RW
    c2 = 2 * D_RW
    c3 = 3 * D_RW
    c4 = c3 + 2 * R_DECAY
    c5 = c4 + 2 * R_AAA
    r, k, v, wd, ad, gd = jnp.split(p, [c1, c2, c3, c4, c5], axis=-1)
    if v_res is None:
        v_first = v
    else:
        v0, v1, v2 = v_res
        v = v + (v_first - v) * jax.nn.sigmoid(v0 + (v @ v1) @ v2)
    wd = wd.reshape(bsz, seq_len, 2, R_DECAY)
    ad = ad.reshape(bsz, seq_len, 2, R_AAA)
    wpre = (w0 + jnp.einsum('bsdr,drc->bsdc', jnp.tanh(wd), w2)).astype(F32)
    decay = jnp.exp(-jnp.exp(-jax.nn.softplus(-wpre) - 0.5))
    a = jax.nn.sigmoid((a0 + jnp.einsum('bsdr,drc->bsdc', ad, a2)).astype(F32))
    g = jax.nn.sigmoid(gd) @ g2

    def heads(t):
        return t.reshape(bsz, seq_len, RW_HEADS, RW_HEAD)

    kk = heads((k * k_k).astype(F32))
    kk = kk / jnp.maximum(jnp.sqrt(jnp.sum(kk * kk, -1, keepdims=True)), 1e-12)
    k_dir = k[:, :, None, :].astype(F32) * (1.0 + (a - 1.0) * k_a)
    rh = heads(r)
    vh = heads(v)
    y = jnp.zeros((bsz, seq_len, RW_HEADS, RW_HEAD), F32)
    for d in range(2):
        y = y + wkv7_scan(rh, heads(decay[:, :, d]), heads(k_dir[:, :, d]), vh, kk,
                          heads(a[:, :, d]), reverse=(d == 1))
    mu_y = jnp.mean(y, -1, keepdims=True)
    var_y = jnp.mean(jnp.square(y - mu_y), -1, keepdims=True)
    yn = ((y - mu_y) * lax.rsqrt(var_y + GN_EPS)).reshape(bsz, seq_len, D_RW) * lnx_g + lnx_b
    bonus = jnp.sum(rh.astype(F32) * heads(jnp.sum(k_dir, 2)) * r_k, -1, keepdims=True) * vh.astype(F32)
    out = (yn + bonus.reshape(bsz, seq_len, D_RW)) * g
    return out.astype(p.dtype), v_first


def expert_choice_ffn(x, w_router, b_router, w_gate, w_up, w_down):
    bsz, seq_len, _ = x.shape
    cap = EC_CAPACITY * seq_len // N_EXPERTS
    aff = jax.nn.softmax((x @ w_router + b_router).astype(F32), axis=-1)
    gate, idx = lax.top_k(jnp.swapaxes(aff, 1, 2), cap)
    idx_e = jnp.swapaxes(idx, 0, 1)
    b_idx = jnp.arange(bsz)[:, None]

    def expert(args):
        wg, wu, wdn, ix = args
        xe = x[b_idx, ix]
        return (jax.nn.silu(xe @ wg) * (xe @ wu)) @ wdn

    ye = lax.map(expert, (w_gate, w_up, w_down, idx_e))
    ye = ye * jnp.swapaxes(gate, 0, 1)[..., None].astype(ye.dtype)
    return jnp.zeros_like(x).at[jnp.arange(bsz)[None, :, None], idx_e].add(ye.astype(x.dtype))


def setup_inputs(seed: int = 0) -> dict:
    key = jax.random.key(seed)
    keys = iter(jax.random.split(key, 64))

    def nrm(shape, scale):
        return jax.random.normal(next(keys), shape, F32) * scale

    def gain(shape):
        return 1.0 + nrm(shape, 0.02)

    n = jnp.arange(D_RW, dtype=F32) / (D_RW - 1)
    ratio = jnp.arange(DEPTH, dtype=F32) / max(DEPTH - 1, 1)
    decay_speed = -7.0 + 5.0 * n[None, :] ** (0.85 + jnp.sqrt(ratio)[:, None])
    nv = max(DEPTH - 1, 0)
    return {
        'x': nrm((BATCH, SEQ, D_MODEL), 1.0),
        'ln0_g': gain((D_MODEL,)),
        'ln0_b': nrm((D_MODEL,), 0.02),
        'w_in': nrm((DEPTH, D_MODEL, P_IN), D_MODEL ** -0.5),
        'hy_conv_w': nrm((DEPTH, 3, P_HY), 3 ** -0.5),
        'hy_conv_b': nrm((DEPTH, P_HY), 0.02),
        'hy_ffn_w1': nrm((DEPTH, HY_EMB, HY_FH), HY_EMB ** -0.5),
        'hy_ffn_b1': nrm((DEPTH, HY_FH), 0.1),
        'hy_ffn_w2': nrm((DEPTH, HY_INNER, HY_FH, HY_FH), HY_FH ** -0.5),
        'hy_ffn_b2': nrm((DEPTH, HY_INNER, HY_FH), 0.1),
        'hy_ffn_w3': nrm((DEPTH, HY_FH, HY_ORDER * 2 * D_HY), HY_FH ** -0.5),
        'hy_freq': gain((DEPTH, HY_FH)),
        'hy_bias': nrm((DEPTH, HY_ORDER, D_HY), 0.5),
        'rw_mu': jax.random.uniform(next(keys), (DEPTH, 2, P_RW), F32, 0.0, 0.5),
        'rw_w0': (decay_speed + 0.5)[:, None, :] + nrm((DEPTH, 2, D_RW), 0.05),
        'rw_w2': nrm((DEPTH, 2, R_DECAY, D_RW), 0.1 * R_DECAY ** -0.5),
        'rw_a0': nrm((DEPTH, 2, D_RW), 0.1),
        'rw_a2': nrm((DEPTH, 2, R_AAA, D_RW), 0.5 * R_AAA ** -0.5),
        'rw_v0': gain((nv, D_RW)),
        'rw_v1': nrm((nv, D_RW, R_MV), D_RW ** -0.5),
        'rw_v2': nrm((nv, R_MV, D_RW), 0.5 * R_MV ** -0.5),
        'rw_g2': nrm((DEPTH, R_GATE, D_RW), R_GATE ** -0.5),
        'rw_k_k': 0.85 + nrm((DEPTH, D_RW), 0.02),
        'rw_k_a': gain((DEPTH, D_RW)),
        'rw_r_k': -0.04 + nrm((DEPTH, RW_HEADS, RW_HEAD), 0.1),
        'rw_lnx_g': gain((DEPTH, D_RW)),
        'rw_lnx_b': nrm((DEPTH, D_RW), 0.02),
        'w_o_hy': nrm((DEPTH, D_HY, D_MODEL), D_HY ** -0.5),
        'w_o_rw': nrm((DEPTH, D_RW, D_MODEL), D_RW ** -0.5),
        'w_out': nrm((DEPTH, D_MODEL, D_MODEL), BETA * D_MODEL ** -0.5),
        'ln1_g': gain((DEPTH, D_MODEL)),
        'ln1_b': nrm((DEPTH, D_MODEL), 0.02),
        'w_router': nrm((DEPTH, D_MODEL, N_EXPERTS), D_MODEL ** -0.5),
        'b_router': nrm((DEPTH, N_EXPERTS), 0.01),
        'w_gate': nrm((DEPTH, N_EXPERTS, D_MODEL, EXPERT_FF), D_MODEL ** -0.5),
        'w_up': nrm((DEPTH, N_EXPERTS, D_MODEL, EXPERT_FF), D_MODEL ** -0.5),
        'w_down': nrm((DEPTH, N_EXPERTS, EXPERT_FF, D_MODEL), BETA * EXPERT_FF ** -0.5),
        'ln2_g': gain((DEPTH, D_MODEL)),
        'ln2_b': nrm((DEPTH, D_MODEL), 0.02),
    }


def reference(x, ln0_g, ln0_b, w_in, hy_conv_w, hy_conv_b, hy_ffn_w1, hy_ffn_b1, hy_ffn_w2, hy_ffn_b2,
              hy_ffn_w3, hy_freq, hy_bias, rw_mu, rw_w0, rw_w2, rw_a0, rw_a2, rw_v0, rw_v1, rw_v2, rw_g2,
              rw_k_k, rw_k_a, rw_r_k, rw_lnx_g, rw_lnx_b, w_o_hy, w_o_rw, w_out, ln1_g, ln1_b,
              w_router, b_router, w_gate, w_up, w_down, ln2_g, ln2_b):
    seq_len = x.shape[1]
    h = layer_norm(x, ln0_g, ln0_b)
    v_first = None
    for l in range(DEPTH):
        p = h @ w_in[l]
        p_hy, p_rw, p_gate = jnp.split(p, [P_HY, P_HY + P_RW], axis=-1)
        kf = hyena_filter_spectra(seq_len, hy_ffn_w1[l], hy_ffn_b1[l], hy_ffn_w2[l], hy_ffn_b2[l],
                                  hy_ffn_w3[l], hy_freq[l])
        y_hy = hyena_branch(p_hy, hy_conv_w[l], hy_conv_b[l], kf, hy_bias[l])
        v_res = None if l == 0 else (rw_v0[l - 1], rw_v1[l - 1], rw_v2[l - 1])
        y_rw, v_first = rwkv7_branch(p_rw, v_first, v_res, rw_mu[l], rw_w0[l], rw_w2[l], rw_a0[l], rw_a2[l],
                                     rw_g2[l], rw_k_k[l], rw_k_a[l], rw_r_k[l], rw_lnx_g[l], rw_lnx_b[l])
        g_hy, g_rw = jnp.split(jax.nn.sigmoid(p_gate), 2, axis=-1)
        merged = g_hy * (y_hy @ w_o_hy[l]) + g_rw * (y_rw @ w_o_rw[l])
        h = layer_norm(ALPHA * h + merged @ w_out[l], ln1_g[l], ln1_b[l])
        moe = expert_choice_ffn(h, w_router[l], b_router[l], w_gate[l], w_up[l], w_down[l])
        h = layer_norm(ALPHA * h + moe, ln2_g[l], ln2_b[l])
    return h
```

```python
import functools
import math

import jax
import jax.numpy as jnp
import numpy as np
from jax import lax
from jax.experimental import pallas as pl
from jax.experimental.pallas import tpu as pltpu

F32 = jnp.float32
BF16 = jnp.bfloat16

RW_HEAD = 64
LANES = 128
WKV_CHUNK = 64
VMEM_LIMIT = 56 * 1024 * 1024


def _split(x):
    hi = x.astype(BF16)
    lo = (x - hi.astype(F32)).astype(BF16)
    return hi, lo


def _dg(a, b, dims):
    return lax.dot_general(a, b, (dims, ((), ())), preferred_element_type=F32)


def _mm(a, b, dims=((1,), (0,)), passes=3):
    if passes == 1:
        return _dg(a.astype(BF16), b.astype(BF16), dims)
    ah, al = _split(a)
    bh, bl = _split(b)
    return _dg(ah, bh, dims) + (_dg(ah, bl, dims) + _dg(al, bh, dims))


_NN = ((1,), (0,))
_NT = ((1,), (1,))
_TN = ((0,), (0,))


def _wkv_kernel(r_ref, k_ref, v_ref, kk_ref, lw_ref, a_ref, ka_ref, y_ref, h_ref):
    C = WKV_CHUNK
    d = pl.program_id(1)
    c = pl.program_id(2)

    @pl.when(c == 0)
    def _():
        h_ref[...] = jnp.zeros_like(h_ref)

    r = r_ref[0]
    k = k_ref[0]
    v = v_ref[0]
    kk = kk_ref[0]
    lw = lw_ref[0, 0]
    a = a_ref[0, 0]
    ka = ka_ref[...]
    n_pairs = r.shape[1] // LANES

    row = lax.broadcasted_iota(jnp.int32, (C, C), 0)
    col = lax.broadcasted_iota(jnp.int32, (C, C), 1)
    sgn = 1 - 2 * d
    incl = (col - row) * sgn <= 0
    row2 = lax.broadcasted_iota(jnp.int32, (C, 2 * C), 0)
    col2 = lax.broadcasted_iota(jnp.int32, (C, 2 * C), 1) % C
    earlier2 = (col2 - row2) * sgn < 0
    incl2 = (col2 - row2) * sgn <= 0

    cin = _mm(incl.astype(F32), lw)
    tot = jnp.sum(lw, axis=0, keepdims=True)
    half = 0.5 * tot
    e0 = jnp.exp(half)
    e1 = jnp.exp(cin - half)
    e2 = jnp.exp(half - cin)
    ew = jnp.exp(-lw)
    kdir = k * (1.0 + (a - 1.0) * ka)
    rt = r * e1
    bt = kk * (e1 * ew)
    kt = kdir * e2
    at = -(kk * a) * e2
    r0 = rt * e0
    b0 = bt * e0
    kh = kt * e0
    ah = at * e0
    e0sq = e0 * e0

    lane = lax.broadcasted_iota(jnp.int32, (1, LANES), 1)
    first = lane < RW_HEAD
    ri = lax.broadcasted_iota(jnp.int32, (LANES, LANES), 0)
    ci = lax.broadcasted_iota(jnp.int32, (LANES, LANES), 1)
    eye = ri == ci
    same_head = (ri // RW_HEAD) == (ci // RW_HEAD)

    def stack2(x):
        return jnp.concatenate([jnp.where(first, x, 0.0), jnp.where(first, 0.0, x)], axis=0)

    for p in range(n_pairs):
        sl = slice(p * LANES, (p + 1) * LANES)
        vp = v[:, sl]
        g = _mm(jnp.concatenate([bt[:, sl], rt[:, sl]], axis=0),
                jnp.concatenate([stack2(kt[:, sl]), stack2(at[:, sl])], axis=0), _NT)
        a_bk = jnp.where(earlier2, g[:C, :2 * C], 0.0)
        a_ba = jnp.where(earlier2, g[:C, 2 * C:], 0.0)
        a_rk = jnp.where(incl2, g[C:, :2 * C], 0.0)
        a_ra = jnp.where(incl2, g[C:, 2 * C:], 0.0)
        hp = h_ref[p]
        bh = _mm(jnp.concatenate([b0[:, sl], r0[:, sl]], axis=0), hp)
        v2 = stack2(vp)
        x = bh[:C] + _mm(a_bk, v2)
        pw = a_ba
        n_steps = C.bit_length() - 1
        for i in range(n_steps):
            x = x + _mm(pw, stack2(x))
            if i + 1 < n_steps:
                pw = _mm(pw, stack2(pw))
        y = bh[C:] + _mm(jnp.concatenate([a_rk, a_ra], axis=1),
                         jnp.concatenate([v2, stack2(x)], axis=0))
        y_ref[0, 0, :, sl] = y
        dg = jnp.where(eye, jnp.broadcast_to(e0sq[:, sl], (LANES, LANES)), 0.0)
        hn = _mm(jnp.concatenate([kh[:, sl], ah[:, sl], dg], axis=0),
                 jnp.concatenate([vp, x, hp], axis=0), _TN)
        h_ref[p] = jnp.where(same_head, hn, 0.0)


def wkv7(r, k, v, kk, lw, a, k_a):
    B, S, D = r.shape
    C = WKV_CHUNK
    nc = S // C
    assert S % C == 0 and D % LANES == 0

    def tok(b, d, c):
        return (b, c + d * (nc - 1 - 2 * c), 0)

    def tok_d(b, d, c):
        return (b, d, c + d * (nc - 1 - 2 * c), 0)

    def out_map(b, d, c):
        return (d, b, c + d * (nc - 1 - 2 * c), 0)

    shared = pl.BlockSpec((1, C, D), tok)
    per_dir = pl.BlockSpec((1, 1, C, D), tok_d)
    return pl.pallas_call(
        _wkv_kernel,
        grid=(B, 2, nc),
        in_specs=[shared, shared, shared, shared, per_dir, per_dir,
                  pl.BlockSpec((1, D), lambda b, d, c: (0, 0))],
        out_specs=pl.BlockSpec((1, 1, C, D), out_map),
        out_shape=jax.ShapeDtypeStruct((2, B, S, D), F32),
        scratch_shapes=[pltpu.VMEM((D // LANES, LANES, LANES), F32)],
        compiler_params=pltpu.CompilerParams(
            dimension_semantics=("parallel", "arbitrary", "arbitrary")),
        name="wkv7_chunked",
    )(r, k, v, kk, lw, a, k_a.reshape(1, D))


def _proj_kernel(x_ref, w_ref, o_ref):
    o_ref[...] = jnp.dot(x_ref[...].astype(BF16), w_ref[...], preferred_element_type=F32)


def _proj(x, w, tm=512):
    M, K = x.shape
    N = w.shape[1]
    assert M % tm == 0 and N % LANES == 0
    return pl.pallas_call(
        _proj_kernel,
        grid=(M // tm,),
        in_specs=[pl.BlockSpec((tm, K), lambda i: (i, 0)),
                  pl.BlockSpec((K, N), lambda i: (0, 0))],
        out_specs=pl.BlockSpec((tm, N), lambda i: (i, 0)),
        out_shape=jax.ShapeDtypeStruct((M, N), F32),
        compiler_params=pltpu.CompilerParams(dimension_semantics=("parallel",),
                                             vmem_limit_bytes=VMEM_LIMIT),
        name="proj",
    )(x, w.astype(BF16))


def _pad_cols(w, mult=LANES):
    n = w.shape[-1]
    pad = (-n) % mult
    return jnp.pad(w, ((0, 0), (0, pad))) if pad else w


LN_EPS = 1e-5
GN_EPS = 64e-5
HY_ORDER = 2
HY_TARGET = 1e-2
HY_SHORT_DECAY_PCT = 0.3
HY_LONG_DECAY_PCT = 1.5
N_EXPERTS = 16
EC_CAPACITY = 2


def _layer_norm(x, g, b):
    mu = jnp.mean(x, -1, keepdims=True)
    var = jnp.mean(jnp.square(x - mu), -1, keepdims=True)
    return (x - mu) * lax.rsqrt(var + LN_EPS) * g + b


def _neighbours(z):
    zp = jnp.pad(z, ((0, 0), (1, 0), (0, 0)))[:, :-1]
    zn = jnp.pad(z, ((0, 0), (0, 1), (0, 0)))[:, 1:]
    return zp, zn


def _hyena_filters(seq_len, w1, b1, w2, b2, w3, freq, d_hy):
    emb = w1.shape[0]
    bands = (emb - 1) // 2
    t = jnp.linspace(0.0, 1.0, seq_len, dtype=F32)[:, None]
    w = 2.0 * math.pi * jnp.arange(seq_len, dtype=F32)[:, None] / seq_len
    f = jnp.linspace(1e-4, bands - 1, bands, dtype=F32)[None, :]
    z = jnp.concatenate([t, jnp.cos(f * w), -jnp.sin(f * w)], axis=-1)
    a = jnp.sin(freq * (z @ w1 + b1))
    for i in range(w2.shape[0]):
        a = jnp.sin(freq * (a @ w2[i] + b2[i]))
    filt = (a @ w3).reshape(seq_len, HY_ORDER, 2, d_hy)
    max_decay = math.log(HY_TARGET) / HY_SHORT_DECAY_PCT
    min_decay = math.log(HY_TARGET) / HY_LONG_DECAY_PCT
    deltas = jnp.linspace(min_decay, max_decay, d_hy, dtype=F32)
    filt = filt * jnp.exp(-t * jnp.abs(deltas))[:, None, None, :]
    filt = filt * lax.rsqrt(jnp.sum(filt * filt, axis=(0, 2), keepdims=True) + 1e-12)
    return filt


def _long_conv_fft(u, filt_o, bias):
    seq_len = u.shape[1]
    k_full = jnp.concatenate([filt_o[:, 0], filt_o[::-1, 1]], axis=0)
    kf = jnp.fft.rfft(k_full, axis=0)
    uf = jnp.fft.rfft(u, n=2 * seq_len, axis=1)
    y = jnp.fft.irfft(uf * kf[None], n=2 * seq_len, axis=1)[:, :seq_len]
    return y + u * bias


def _hyena_branch(p, conv_w, conv_b, filt, bias):
    zp, zn = _neighbours(p)
    p = conv_w[0] * zp + conv_w[1] * p + conv_w[2] * zn + conv_b
    v, x1, x2 = jnp.split(p, 3, axis=-1)
    z = x1 * _long_conv_fft(v, filt[:, 0], bias[0])
    return x2 * _long_conv_fft(z, filt[:, 1], bias[1])


def _rwkv_branch(p, v_first, v_res, mu, w0, w2, a0, a2, g2, k_k, k_a, r_k, lnx_g, lnx_b):
    bsz, seq_len, _ = p.shape
    d_rw = w0.shape[-1]
    r_decay = w2.shape[1]
    r_aaa = a2.shape[1]
    nh = d_rw // RW_HEAD
    zp, zn = _neighbours(p)
    p = p + mu[0] * (zp - p) + mu[1] * (zn - p)
    c3 = 3 * d_rw
    c4 = c3 + 2 * r_decay
    c5 = c4 + 2 * r_aaa
    r, k, v, wd, ad, gd = jnp.split(p, [d_rw, 2 * d_rw, c3, c4, c5], axis=-1)
    if v_res is None:
        v_first = v
    else:
        v0, v1, v2 = v_res
        v = v + (v_first - v) * jax.nn.sigmoid(v0 + (v @ v1) @ v2)
    wd = wd.reshape(bsz, seq_len, 2, r_decay)
    ad = ad.reshape(bsz, seq_len, 2, r_aaa)
    wpre = w0 + jnp.einsum('bsdr,drc->bsdc', jnp.tanh(wd), w2)
    lw = -jnp.exp(-jax.nn.softplus(-wpre) - 0.5)
    a = jax.nn.sigmoid(a0 + jnp.einsum('bsdr,drc->bsdc', ad, a2))
    g = jax.nn.sigmoid(gd) @ g2

    def heads(t):
        return t.reshape(bsz, seq_len, nh, RW_HEAD)

    kk = heads(k * k_k)
    kk = (kk / jnp.maximum(jnp.sqrt(jnp.sum(kk * kk, -1, keepdims=True)), 1e-12)).reshape(bsz, seq_len, d_rw)
    y2 = wkv7(r, k, v, kk, jnp.swapaxes(lw, 1, 2), jnp.swapaxes(a, 1, 2), k_a)
    y = heads(y2[0] + y2[1])
    mu_y = jnp.mean(y, -1, keepdims=True)
    var_y = jnp.mean(jnp.square(y - mu_y), -1, keepdims=True)
    yn = ((y - mu_y) * lax.rsqrt(var_y + GN_EPS)).reshape(bsz, seq_len, d_rw) * lnx_g + lnx_b
    k_sum = k[:, :, None, :] * (1.0 + (a - 1.0) * k_a)
    k_sum = k_sum[:, :, 0] + k_sum[:, :, 1]
    bonus = jnp.sum(heads(r) * heads(k_sum) * r_k, -1, keepdims=True) * heads(v)
    out = (yn + bonus.reshape(bsz, seq_len, d_rw)) * g
    return out, v_first


def _expert_choice_ffn(x, w_router, b_router, w_gate, w_up, w_down):
    bsz, seq_len, _ = x.shape
    cap = EC_CAPACITY * seq_len // N_EXPERTS
    aff = jax.nn.softmax(x @ w_router + b_router, axis=-1)
    gate, idx = lax.top_k(jnp.swapaxes(aff, 1, 2), cap)
    idx_e = jnp.swapaxes(idx, 0, 1)
    b_idx = jnp.arange(bsz)[:, None]

    def expert(args):
        wg, wu, wdn, ix = args
        xe = x[b_idx, ix]
        return (jax.nn.silu(xe @ wg) * (xe @ wu)) @ wdn

    ye = lax.map(expert, (w_gate, w_up, w_down, idx_e))
    ye = ye * jnp.swapaxes(gate, 0, 1)[..., None]
    return jnp.zeros_like(x).at[jnp.arange(bsz)[None, :, None], idx_e].add(ye)


def kernel(x, ln0_g, ln0_b, w_in, hy_conv_w, hy_conv_b, hy_ffn_w1, hy_ffn_b1, hy_ffn_w2, hy_ffn_b2,
           hy_ffn_w3, hy_freq, hy_bias, rw_mu, rw_w0, rw_w2, rw_a0, rw_a2, rw_v0, rw_v1, rw_v2, rw_g2,
           rw_k_k, rw_k_a, rw_r_k, rw_lnx_g, rw_lnx_b, w_o_hy, w_o_rw, w_out, ln1_g, ln1_b,
           w_router, b_router, w_gate, w_up, w_down, ln2_g, ln2_b):
    bsz, seq_len, d_model = x.shape
    depth = w_in.shape[0]
    d_hy = hy_bias.shape[-1]
    p_hy = 3 * d_hy
    p_rw = rw_mu.shape[-1]
    alpha = (2 * depth) ** 0.25
    tokens = bsz * seq_len
    h = _layer_norm(x, ln0_g, ln0_b)
    v_first = None
    for l in range(depth):
        p = _proj(h.reshape(tokens, d_model), _pad_cols(w_in[l])).reshape(bsz, seq_len, -1)
        pa = p[..., :p_hy]
        pb = p[..., p_hy:p_hy + p_rw]
        pg = p[..., p_hy + p_rw:p_hy + p_rw + 2 * d_model]
        filt = _hyena_filters(seq_len, hy_ffn_w1[l], hy_ffn_b1[l], hy_ffn_w2[l], hy_ffn_b2[l],
                              hy_ffn_w3[l], hy_freq[l], d_hy)
        y_hy = _hyena_branch(pa, hy_conv_w[l], hy_conv_b[l], filt, hy_bias[l])
        v_res = None if l == 0 else (rw_v0[l - 1], rw_v1[l - 1], rw_v2[l - 1])
        y_rw, v_first = _rwkv_branch(pb, v_first, v_res, rw_mu[l], rw_w0[l], rw_w2[l], rw_a0[l], rw_a2[l],
                                     rw_g2[l], rw_k_k[l], rw_k_a[l], rw_r_k[l], rw_lnx_g[l], rw_lnx_b[l])
        gates = jax.nn.sigmoid(pg)
        g_hy = gates[..., :d_model]
        g_rw = gates[..., d_model:]
        o_hy = _proj(y_hy.reshape(tokens, -1), w_o_hy[l]).reshape(bsz, seq_len, d_model)
        o_rw = _proj(y_rw.reshape(tokens, -1), w_o_rw[l]).reshape(bsz, seq_len, d_model)
        merged = g_hy * o_hy + g_rw * o_rw
        mo = _proj(merged.reshape(tokens, d_model), w_out[l]).reshape(bsz, seq_len, d_model)
        h = _layer_norm(alpha * h + mo, ln1_g[l], ln1_b[l])
        moe = _expert_choice_ffn(h, w_router[l], b_router[l], w_gate[l], w_up[l], w_down[l])
        h = _layer_norm(alpha * h + moe, ln2_g[l], ln2_b[l])
    return h
```

```python
import functools
import math

import jax
import jax.numpy as jnp
import numpy as np
from jax import lax
from jax.experimental import pallas as pl
from jax.experimental.pallas import tpu as pltpu

F32 = jnp.float32
BF16 = jnp.bfloat16

RW_HEAD = 64
LANES = 128
WKV_CHUNK = 64
VMEM_LIMIT = 56 * 1024 * 1024


def _split(x):
    hi = x.astype(BF16)
    lo = (x - hi.astype(F32)).astype(BF16)
    return hi, lo


def _dg(a, b, dims):
    return lax.dot_general(a, b, (dims, ((), ())), preferred_element_type=F32)


def _mm(a, b, dims=((1,), (0,)), passes=1):
    if passes == 1:
        return _dg(a.astype(BF16), b.astype(BF16), dims)
    ah, al = _split(a)
    bh, bl = _split(b)
    return _dg(ah, bh, dims) + (_dg(ah, bl, dims) + _dg(al, bh, dims))


_NN = ((1,), (0,))
_NT = ((1,), (1,))
_TN = ((0,), (0,))


def _wkv_kernel(r_ref, k_ref, v_ref, kk_ref, lw_ref, a_ref, ka_ref, y_ref, h_ref):
    C = WKV_CHUNK
    d = pl.program_id(1)
    c = pl.program_id(2)

    @pl.when(c == 0)
    def _():
        h_ref[...] = jnp.zeros_like(h_ref)

    r = r_ref[0]
    k = k_ref[0]
    v = v_ref[0]
    kk = kk_ref[0]
    lw = lw_ref[0, 0]
    a = a_ref[0, 0]
    ka = ka_ref[...]
    n_pairs = r.shape[1] // LANES

    row = lax.broadcasted_iota(jnp.int32, (C, C), 0)
    col = lax.broadcasted_iota(jnp.int32, (C, C), 1)
    sgn = 1 - 2 * d
    incl = (col - row) * sgn <= 0
    row2 = lax.broadcasted_iota(jnp.int32, (C, 2 * C), 0)
    col2 = lax.broadcasted_iota(jnp.int32, (C, 2 * C), 1) % C
    earlier2 = (col2 - row2) * sgn < 0
    incl2 = (col2 - row2) * sgn <= 0

    lw_hi, lw_lo = _split(lw)
    tri = incl.astype(BF16)
    cin = _dg(tri, lw_hi, _NN) + _dg(tri, lw_lo, _NN)
    tot = jnp.sum(lw, axis=0, keepdims=True)
    half = 0.5 * tot
    e0 = jnp.exp(half)
    e1 = jnp.exp(cin - half)
    e2 = jnp.exp(half - cin)
    ew = jnp.exp(-lw)
    kdir = k * (1.0 + (a - 1.0) * ka)
    rt = r * e1
    bt = kk * (e1 * ew)
    kt = kdir * e2
    at = -(kk * a) * e2
    r0 = rt * e0
    b0 = bt * e0
    kh = kt * e0
    ah = at * e0
    e0sq = e0 * e0

    lane = lax.broadcasted_iota(jnp.int32, (1, LANES), 1)
    first = lane < RW_HEAD
    ri = lax.broadcasted_iota(jnp.int32, (LANES, LANES), 0)
    ci = lax.broadcasted_iota(jnp.int32, (LANES, LANES), 1)
    eye = ri == ci
    same_head = (ri // RW_HEAD) == (ci // RW_HEAD)

    def stack2(x):
        return jnp.concatenate([jnp.where(first, x, 0.0), jnp.where(first, 0.0, x)], axis=0)

    pairs = range(n_pairs)
    sls = [slice(p * LANES, (p + 1) * LANES) for p in pairs]
    g = [_mm(jnp.concatenate([bt[:, sl], rt[:, sl]], axis=0),
             jnp.concatenate([stack2(kt[:, sl]), stack2(at[:, sl])], axis=0), _NT) for sl in sls]
    hp = [h_ref[p] for p in pairs]
    bh = [_mm(jnp.concatenate([b0[:, sl], r0[:, sl]], axis=0), hp[p]) for p, sl in zip(pairs, sls)]
    v2 = [stack2(v[:, sl]) for sl in sls]
    x = [bh[p][:C] + _mm(jnp.where(earlier2, g[p][:C, :2 * C], 0.0), v2[p]) for p in pairs]
    pw = [jnp.where(earlier2, g[p][:C, 2 * C:], 0.0) for p in pairs]
    n_steps = C.bit_length() - 1
    for i in range(n_steps):
        x = [x[p] + _mm(pw[p], stack2(x[p])) for p in pairs]
        if i + 1 < n_steps:
            pw = [_mm(pw[p], stack2(pw[p])) for p in pairs]
    for p, sl in zip(pairs, sls):
        a_r = jnp.concatenate([jnp.where(incl2, g[p][C:, :2 * C], 0.0),
                               jnp.where(incl2, g[p][C:, 2 * C:], 0.0)], axis=1)
        y_ref[0, 0, :, sl] = bh[p][C:] + _mm(a_r, jnp.concatenate([v2[p], stack2(x[p])], axis=0))
    for p, sl in zip(pairs, sls):
        dg = jnp.where(eye, jnp.broadcast_to(e0sq[:, sl], (LANES, LANES)), 0.0)
        hn = _mm(jnp.concatenate([kh[:, sl], ah[:, sl], dg], axis=0),
                 jnp.concatenate([v[:, sl], x[p], hp[p]], axis=0), _TN)
        h_ref[p] = jnp.where(same_head, hn, 0.0)


def wkv7(r, k, v, kk, lw, a, k_a):
    B, S, D = r.shape
    C = WKV_CHUNK
    nc = S // C
    assert S % C == 0 and D % LANES == 0

    def tok(b, d, c):
        return (b, c + d * (nc - 1 - 2 * c), 0)

    def tok_d(b, d, c):
        return (b, d, c + d * (nc - 1 - 2 * c), 0)

    def out_map(b, d, c):
        return (d, b, c + d * (nc - 1 - 2 * c), 0)

    shared = pl.BlockSpec((1, C, D), tok)
    per_dir = pl.BlockSpec((1, 1, C, D), tok_d)
    return pl.pallas_call(
        _wkv_kernel,
        grid=(B, 2, nc),
        in_specs=[shared, shared, shared, shared, per_dir, per_dir,
                  pl.BlockSpec((1, D), lambda b, d, c: (0, 0))],
        out_specs=pl.BlockSpec((1, 1, C, D), out_map),
        out_shape=jax.ShapeDtypeStruct((2, B, S, D), F32),
        scratch_shapes=[pltpu.VMEM((D // LANES, LANES, LANES), F32)],
        compiler_params=pltpu.CompilerParams(
            dimension_semantics=("parallel", "arbitrary", "arbitrary")),
        name="wkv7_chunked",
    )(r, k, v, kk, lw, a, k_a.reshape(1, D))


FFT_N2 = 64
FFT_K1_BLOCK = 8


def _round_up(n, m):
    return (n + m - 1) // m * m


def _cat3(m, axis):
    m32 = jnp.asarray(m, F32)
    hi, lo = _split(m32)
    return jnp.concatenate([hi, hi, lo], axis=axis)


def _data3(x):
    hi, lo = _split(x)
    return jnp.concatenate([hi, lo, hi], axis=0)


@functools.lru_cache(maxsize=None)
def _fft_tables(seq_len):
    n = 2 * seq_len
    n2 = FFT_N2
    n1 = n // n2
    h1 = n1 // 2
    k1n = h1 + 1
    k1p = _round_up(k1n, FFT_K1_BLOCK)
    k1 = np.arange(k1n)[:, None]
    th1 = 2 * np.pi * k1 * np.arange(h1)[None, :] / n1
    f1 = np.zeros((2 * k1p, h1))
    f1[:k1n] = np.cos(th1)
    f1[k1p:k1p + k1n] = -np.sin(th1)
    m = np.arange(n2)
    th2 = 2 * np.pi * (m[None, None, :] * m[None, :, None] / n2 + m[None, None, :] * np.arange(k1n)[:, None, None] / n)
    gr, gi = np.cos(th2), -np.sin(th2)
    g = np.zeros((k1p, 2 * n2, 2 * n2))
    g[:k1n] = np.block([[gr, -gi], [gi, gr]])
    gt = np.transpose(g, (0, 2, 1))
    coef = np.full((k1n,), 2.0)
    coef[0] = 1.0
    coef[-1] = 1.0
    th3 = th1.T
    m3 = np.zeros((h1, 2 * k1p))
    m3[:, :k1n] = coef * np.cos(th3) / n
    m3[:, k1p:k1p + k1n] = -coef * np.sin(th3) / n
    kidx = (np.arange(k1n)[:, None] + n1 * np.arange(n2)[None, :])
    return dict(n1=n1, h1=h1, k1n=k1n, k1p=k1p, f1=f1, g=g, gt=gt, m3=m3, kidx=kidx)


def _fft_s1_kernel(f_ref, x_ref, o_ref):
    o_ref[0] = jnp.dot(f_ref[...], _data3(x_ref[0]), preferred_element_type=F32)


def _fft_mid_kernel(g_ref, gt_ref, kr_ref, ki_ref, a_ref, o_ref):
    n2 = FFT_N2
    for j in range(FFT_K1_BLOCK):
        xin = jnp.concatenate([a_ref[0, 0, j], a_ref[0, 1, j]], axis=0)
        z = jnp.dot(g_ref[j], _data3(xin), preferred_element_type=F32)
        zr, zi = z[:n2], z[n2:]
        kr, ki = kr_ref[j], ki_ref[j]
        y = jnp.concatenate([zr * kr - zi * ki, zr * ki + zi * kr], axis=0)
        b = jnp.dot(gt_ref[j], _data3(y), preferred_element_type=F32)
        o_ref[0, 0, j] = b[:n2]
        o_ref[0, 1, j] = b[n2:]


def _fft_s3_kernel(m_ref, b_ref, u_ref, x_ref, bias_ref, o_ref):
    y = jnp.dot(m_ref[...], _data3(b_ref[0]), preferred_element_type=F32)
    u = u_ref[0]
    o_ref[0] = x_ref[0] * (y + u * bias_ref[...])


def _long_conv_gated(u, gate, kf_re, kf_im, bias, lane_tile=2048):
    bsz, seq_len, ch = u.shape
    t = _fft_tables(seq_len)
    n2, h1, k1p = FFT_N2, t["h1"], t["k1p"]
    lanes = n2 * ch
    lt = min(lane_tile, lanes)
    assert lanes % lt == 0 and seq_len == h1 * n2
    params = pltpu.CompilerParams(dimension_semantics=("parallel", "parallel"),
                                  vmem_limit_bytes=VMEM_LIMIT)
    f1 = _cat3(t["f1"], 1)
    a = pl.pallas_call(
        _fft_s1_kernel,
        grid=(bsz, lanes // lt),
        in_specs=[pl.BlockSpec(f1.shape, lambda b, j: (0, 0)),
                  pl.BlockSpec((1, h1, lt), lambda b, j: (b, 0, j))],
        out_specs=pl.BlockSpec((1, 2 * k1p, lt), lambda b, j: (b, 0, j)),
        out_shape=jax.ShapeDtypeStruct((bsz, 2 * k1p, lanes), F32),
        compiler_params=params, name="hyena_dft1",
    )(f1, u.reshape(bsz, h1, lanes))
    kb = FFT_K1_BLOCK
    g = _cat3(t["g"], 2)
    gt = _cat3(t["gt"], 2)
    mat_spec = pl.BlockSpec((kb,) + g.shape[1:], lambda b, j: (j, 0, 0))
    kf_spec = pl.BlockSpec((kb, n2, ch), lambda b, j: (j, 0, 0))
    blk = pl.BlockSpec((1, 2, kb, n2, ch), lambda b, j: (b, 0, j, 0, 0))
    bm = pl.pallas_call(
        _fft_mid_kernel,
        grid=(bsz, k1p // kb),
        in_specs=[mat_spec, mat_spec, kf_spec, kf_spec, blk],
        out_specs=blk,
        out_shape=jax.ShapeDtypeStruct((bsz, 2, k1p, n2, ch), F32),
        compiler_params=params, name="hyena_dft2",
    )(g, gt, kf_re, kf_im, a.reshape(bsz, 2, k1p, n2, ch))
    m3 = _cat3(t["m3"], 1)
    row = pl.BlockSpec((1, h1, lt), lambda b, j: (b, 0, j))
    out = pl.pallas_call(
        _fft_s3_kernel,
        grid=(bsz, lanes // lt),
        in_specs=[pl.BlockSpec(m3.shape, lambda b, j: (0, 0)),
                  pl.BlockSpec((1, 2 * k1p, lt), lambda b, j: (b, 0, j)),
                  row, row, pl.BlockSpec((1, lt), lambda b, j: (0, j))],
        out_specs=row,
        out_shape=jax.ShapeDtypeStruct((bsz, h1, lanes), F32),
        compiler_params=params, name="hyena_dft3",
    )(m3, bm.reshape(bsz, 2 * k1p, lanes), u.reshape(bsz, h1, lanes), gate.reshape(bsz, h1, lanes),
      jnp.tile(bias, n2).reshape(1, lanes))
    return out.reshape(bsz, seq_len, ch)


def _filter_spectrum(filt_o, seq_len):
    t = _fft_tables(seq_len)
    k_full = jnp.concatenate([filt_o[:, 0], filt_o[::-1, 1]], axis=0)
    kf = jnp.fft.fft(k_full, axis=0)[t["kidx"].reshape(-1)]
    kf = kf.reshape(t["k1n"], FFT_N2, -1)
    pad = ((0, t["k1p"] - t["k1n"]), (0, 0), (0, 0))
    return jnp.pad(jnp.real(kf), pad), jnp.pad(jnp.imag(kf), pad)


def _proj_kernel(x_ref, w_ref, o_ref):
    o_ref[...] = jnp.dot(x_ref[...].astype(BF16), w_ref[...], preferred_element_type=F32)


def _proj(x, w, tm=512):
    M, K = x.shape
    N = w.shape[1]
    assert M % tm == 0 and N % LANES == 0
    return pl.pallas_call(
        _proj_kernel,
        grid=(M // tm,),
        in_specs=[pl.BlockSpec((tm, K), lambda i: (i, 0)),
                  pl.BlockSpec((K, N), lambda i: (0, 0))],
        out_specs=pl.BlockSpec((tm, N), lambda i: (i, 0)),
        out_shape=jax.ShapeDtypeStruct((M, N), F32),
        compiler_params=pltpu.CompilerParams(dimension_semantics=("parallel",),
                                             vmem_limit_bytes=VMEM_LIMIT),
        name="proj",
    )(x, w.astype(BF16))


def _pad_cols(w, mult=LANES):
    n = w.shape[-1]
    pad = (-n) % mult
    return jnp.pad(w, ((0, 0), (0, pad))) if pad else w


LN_EPS = 1e-5
GN_EPS = 64e-5
HY_ORDER = 2
HY_TARGET = 1e-2
HY_SHORT_DECAY_PCT = 0.3
HY_LONG_DECAY_PCT = 1.5
N_EXPERTS = 16
EC_CAPACITY = 2


def _layer_norm(x, g, b):
    mu = jnp.mean(x, -1, keepdims=True)
    var = jnp.mean(jnp.square(x - mu), -1, keepdims=True)
    return (x - mu) * lax.rsqrt(var + LN_EPS) * g + b


def _neighbours(z):
    zp = jnp.pad(z, ((0, 0), (1, 0), (0, 0)))[:, :-1]
    zn = jnp.pad(z, ((0, 0), (0, 1), (0, 0)))[:, 1:]
    return zp, zn


def _hyena_filters(seq_len, w1, b1, w2, b2, w3, freq, d_hy):
    emb = w1.shape[0]
    bands = (emb - 1) // 2
    t = jnp.linspace(0.0, 1.0, seq_len, dtype=F32)[:, None]
    w = 2.0 * math.pi * jnp.arange(seq_len, dtype=F32)[:, None] / seq_len
    f = jnp.linspace(1e-4, bands - 1, bands, dtype=F32)[None, :]
    z = jnp.concatenate([t, jnp.cos(f * w), -jnp.sin(f * w)], axis=-1)
    a = jnp.sin(freq * (z @ w1 + b1))
    for i in range(w2.shape[0]):
        a = jnp.sin(freq * (a @ w2[i] + b2[i]))
    filt = (a @ w3).reshape(seq_len, HY_ORDER, 2, d_hy)
    max_decay = math.log(HY_TARGET) / HY_SHORT_DECAY_PCT
    min_decay = math.log(HY_TARGET) / HY_LONG_DECAY_PCT
    deltas = jnp.linspace(min_decay, max_decay, d_hy, dtype=F32)
    filt = filt * jnp.exp(-t * jnp.abs(deltas))[:, None, None, :]
    filt = filt * lax.rsqrt(jnp.sum(filt * filt, axis=(0, 2), keepdims=True) + 1e-12)
    return filt


def _hyena_branch(p, conv_w, conv_b, filt, bias):
    seq_len = p.shape[1]
    zp, zn = _neighbours(p)
    p = conv_w[0] * zp + conv_w[1] * p + conv_w[2] * zn + conv_b
    v, x1, x2 = jnp.split(p, 3, axis=-1)
    z = _long_conv_gated(v, x1, *_filter_spectrum(filt[:, 0], seq_len), bias[0])
    return _long_conv_gated(z, x2, *_filter_spectrum(filt[:, 1], seq_len), bias[1])


def _rwkv_branch(p, v_first, v_res, mu, w0, w2, a0, a2, g2, k_k, k_a, r_k, lnx_g, lnx_b):
    bsz, seq_len, _ = p.shape
    d_rw = w0.shape[-1]
    r_decay = w2.shape[1]
    r_aaa = a2.shape[1]
    nh = d_rw // RW_HEAD
    zp, zn = _neighbours(p)
    p = p + mu[0] * (zp - p) + mu[1] * (zn - p)
    c3 = 3 * d_rw
    c4 = c3 + 2 * r_decay
    c5 = c4 + 2 * r_aaa
    r, k, v, wd, ad, gd = jnp.split(p, [d_rw, 2 * d_rw, c3, c4, c5], axis=-1)
    if v_res is None:
        v_first = v
    else:
        v0, v1, v2 = v_res
        v = v + (v_first - v) * jax.nn.sigmoid(v0 + (v @ v1) @ v2)
    wd = wd.reshape(bsz, seq_len, 2, r_decay)
    ad = ad.reshape(bsz, seq_len, 2, r_aaa)
    wpre = w0 + jnp.einsum('bsdr,drc->bsdc', jnp.tanh(wd), w2)
    lw = -jnp.exp(-jax.nn.softplus(-wpre) - 0.5)
    a = jax.nn.sigmoid(a0 + jnp.einsum('bsdr,drc->bsdc', ad, a2))
    g = jax.nn.sigmoid(gd) @ g2

    def heads(t):
        return t.reshape(bsz, seq_len, nh, RW_HEAD)

    kk = heads(k * k_k)
    kk = (kk / jnp.maximum(jnp.sqrt(jnp.sum(kk * kk, -1, keepdims=True)), 1e-12)).reshape(bsz, seq_len, d_rw)
    y2 = wkv7(r, k, v, kk, jnp.swapaxes(lw, 1, 2), jnp.swapaxes(a, 1, 2), k_a)
    y = heads(y2[0] + y2[1])
    mu_y = jnp.mean(y, -1, keepdims=True)
    var_y = jnp.mean(jnp.square(y - mu_y), -1, keepdims=True)
    yn = ((y - mu_y) * lax.rsqrt(var_y + GN_EPS)).reshape(bsz, seq_len, d_rw) * lnx_g + lnx_b
    k_sum = k[:, :, None, :] * (1.0 + (a - 1.0) * k_a)
    k_sum = k_sum[:, :, 0] + k_sum[:, :, 1]
    bonus = jnp.sum(heads(r) * heads(k_sum) * r_k, -1, keepdims=True) * heads(v)
    out = (yn + bonus.reshape(bsz, seq_len, d_rw)) * g
    return out, v_first


def _expert_choice_ffn(x, w_router, b_router, w_gate, w_up, w_down):
    bsz, seq_len, _ = x.shape
    cap = EC_CAPACITY * seq_len // N_EXPERTS
    aff = jax.nn.softmax(x @ w_router + b_router, axis=-1)
    gate, idx = lax.top_k(jnp.swapaxes(aff, 1, 2), cap)
    idx_e = jnp.swapaxes(idx, 0, 1)
    b_idx = jnp.arange(bsz)[:, None]

    def expert(args):
        wg, wu, wdn, ix = args
        xe = x[b_idx, ix]
        return (jax.nn.silu(xe @ wg) * (xe @ wu)) @ wdn

    ye = lax.map(expert, (w_gate, w_up, w_down, idx_e))
    ye = ye * jnp.swapaxes(gate, 0, 1)[..., None]
    return jnp.zeros_like(x).at[jnp.arange(bsz)[None, :, None], idx_e].add(ye)


def kernel(x, ln0_g, ln0_b, w_in, hy_conv_w, hy_conv_b, hy_ffn_w1, hy_ffn_b1, hy_ffn_w2, hy_ffn_b2,
           hy_ffn_w3, hy_freq, hy_bias, rw_mu, rw_w0, rw_w2, rw_a0, rw_a2, rw_v0, rw_v1, rw_v2, rw_g2,
           rw_k_k, rw_k_a, rw_r_k, rw_lnx_g, rw_lnx_b, w_o_hy, w_o_rw, w_out, ln1_g, ln1_b,
           w_router, b_router, w_gate, w_up, w_down, ln2_g, ln2_b):
    bsz, seq_len, d_model = x.shape
    depth = w_in.shape[0]
    d_hy = hy_bias.shape[-1]
    p_hy = 3 * d_hy
    p_rw = rw_mu.shape[-1]
    alpha = (2 * depth) ** 0.25
    tokens = bsz * seq_len
    h = _layer_norm(x, ln0_g, ln0_b)
    v_first = None
    for l in range(depth):
        p = _proj(h.reshape(tokens, d_model), _pad_cols(w_in[l])).reshape(bsz, seq_len, -1)
        pa = p[..., :p_hy]
        pb = p[..., p_hy:p_hy + p_rw]
        pg = p[..., p_hy + p_rw:p_hy + p_rw + 2 * d_model]
        filt = _hyena_filters(seq_len, hy_ffn_w1[l], hy_ffn_b1[l], hy_ffn_w2[l], hy_ffn_b2[l],
                              hy_ffn_w3[l], hy_freq[l], d_hy)
        y_hy = _hyena_branch(pa, hy_conv_w[l], hy_conv_b[l], filt, hy_bias[l])
        v_res = None if l == 0 else (rw_v0[l - 1], rw_v1[l - 1], rw_v2[l - 1])
        y_rw, v_first = _rwkv_branch(pb, v_first, v_res, rw_mu[l], rw_w0[l], rw_w2[l], rw_a0[l], rw_a2[l],
                                     rw_g2[l], rw_k_k[l], rw_k_a[l], rw_r_k[l], rw_lnx_g[l], rw_lnx_b[l])
        gates = jax.nn.sigmoid(pg)
        g_hy = gates[..., :d_model]
        g_rw = gates[..., d_model:]
        o_hy = _proj(y_hy.reshape(tokens, -1), w_o_hy[l]).reshape(bsz, seq_len, d_model)
        o_rw = _proj(y_rw.reshape(tokens, -1), w_o_rw[l]).reshape(bsz, seq_len, d_model)
        merged = g_hy * o_hy + g_rw * o_rw
        mo = _proj(merged.reshape(tokens, d_model), w_out[l]).reshape(bsz, seq_len, d_model)
        h = _layer_norm(alpha * h + mo, ln1_g[l], ln1_b[l])
        moe = _expert_choice_ffn(h, w_router[l], b_router[l], w_gate[l], w_up[l], w_down[l])
        h = _layer_norm(alpha * h + moe, ln2_g[l], ln2_b[l])
    return h
```

```python
import functools
import math

import jax
import jax.numpy as jnp
import numpy as np
from jax import lax
from jax.experimental import pallas as pl
from jax.experimental.pallas import tpu as pltpu

F32 = jnp.float32
BF16 = jnp.bfloat16

RW_HEAD = 64
LANES = 128
WKV_CHUNK = 64
VMEM_LIMIT = 56 * 1024 * 1024


def _split(x):
    hi = x.astype(BF16)
    lo = (x - hi.astype(F32)).astype(BF16)
    return hi, lo


def _dg(a, b, dims):
    return lax.dot_general(a, b, (dims, ((), ())), preferred_element_type=F32)


def _mm(a, b, dims=((1,), (0,)), passes=1):
    if passes == 1:
        return _dg(a.astype(BF16), b.astype(BF16), dims)
    ah, al = _split(a)
    bh, bl = _split(b)
    return _dg(ah, bh, dims) + (_dg(ah, bl, dims) + _dg(al, bh, dims))


_NN = ((1,), (0,))
_NT = ((1,), (1,))
_TN = ((0,), (0,))


def _wkv_kernel(r_ref, k_ref, v_ref, kk_ref, lw_ref, a_ref, ka_ref, y_ref, h_ref):
    C = WKV_CHUNK
    d = pl.program_id(1)
    c = pl.program_id(2)

    @pl.when(c == 0)
    def _():
        h_ref[...] = jnp.zeros_like(h_ref)

    r = r_ref[0]
    k = k_ref[0]
    v = v_ref[0]
    kk = kk_ref[0]
    lw = lw_ref[0, 0]
    a = a_ref[0, 0]
    ka = ka_ref[...]
    n_pairs = r.shape[1] // LANES

    row = lax.broadcasted_iota(jnp.int32, (C, C), 0)
    col = lax.broadcasted_iota(jnp.int32, (C, C), 1)
    sgn = 1 - 2 * d
    incl = (col - row) * sgn <= 0
    row2 = lax.broadcasted_iota(jnp.int32, (C, 2 * C), 0)
    col2 = lax.broadcasted_iota(jnp.int32, (C, 2 * C), 1) % C
    earlier2 = (col2 - row2) * sgn < 0
    incl2 = (col2 - row2) * sgn <= 0

    lw_hi, lw_lo = _split(lw)
    tri = incl.astype(BF16)
    cin = _dg(tri, lw_hi, _NN) + _dg(tri, lw_lo, _NN)
    tot = jnp.sum(lw, axis=0, keepdims=True)
    half = 0.5 * tot
    e0 = jnp.exp(half)
    e1 = jnp.exp(cin - half)
    e2 = jnp.exp(half - cin)
    ew = jnp.exp(-lw)
    kdir = k * (1.0 + (a - 1.0) * ka)
    rt = r * e1
    bt = kk * (e1 * ew)
    kt = kdir * e2
    at = -(kk * a) * e2
    r0 = rt * e0
    b0 = bt * e0
    kh = kt * e0
    ah = at * e0
    e0sq = e0 * e0

    lane = lax.broadcasted_iota(jnp.int32, (1, LANES), 1)
    first = lane < RW_HEAD
    ri = lax.broadcasted_iota(jnp.int32, (LANES, LANES), 0)
    ci = lax.broadcasted_iota(jnp.int32, (LANES, LANES), 1)
    eye = ri == ci
    same_head = (ri // RW_HEAD) == (ci // RW_HEAD)

    def stack2(x):
        return jnp.concatenate([jnp.where(first, x, 0.0), jnp.where(first, 0.0, x)], axis=0)

    pairs = range(n_pairs)
    sls = [slice(p * LANES, (p + 1) * LANES) for p in pairs]
    g = [_mm(jnp.concatenate([bt[:, sl], rt[:, sl]], axis=0),
             jnp.concatenate([stack2(kt[:, sl]), stack2(at[:, sl])], axis=0), _NT) for sl in sls]
    hp = [h_ref[p] for p in pairs]
    bh = [_mm(jnp.concatenate([b0[:, sl], r0[:, sl]], axis=0), hp[p]) for p, sl in zip(pairs, sls)]
    v2 = [stack2(v[:, sl]) for sl in sls]
    x = [bh[p][:C] + _mm(jnp.where(earlier2, g[p][:C, :2 * C], 0.0), v2[p]) for p in pairs]
    pw = [jnp.where(earlier2, g[p][:C, 2 * C:], 0.0) for p in pairs]
    n_steps = C.bit_length() - 1
    for i in range(n_steps):
        x = [x[p] + _mm(pw[p], stack2(x[p])) for p in pairs]
        if i + 1 < n_steps:
            pw = [_mm(pw[p], stack2(pw[p])) for p in pairs]
    for p, sl in zip(pairs, sls):
        a_r = jnp.concatenate([jnp.where(incl2, g[p][C:, :2 * C], 0.0),
                               jnp.where(incl2, g[p][C:, 2 * C:], 0.0)], axis=1)
        y_ref[0, 0, :, sl] = bh[p][C:] + _mm(a_r, jnp.concatenate([v2[p], stack2(x[p])], axis=0))
    for p, sl in zip(pairs, sls):
        dg = jnp.where(eye, jnp.broadcast_to(e0sq[:, sl], (LANES, LANES)), 0.0)
        hn = _mm(jnp.concatenate([kh[:, sl], ah[:, sl], dg], axis=0),
                 jnp.concatenate([v[:, sl], x[p], hp[p]], axis=0), _TN)
        h_ref[p] = jnp.where(same_head, hn, 0.0)


def wkv7(r, k, v, kk, lw, a, k_a):
    B, S, D = r.shape
    C = WKV_CHUNK
    nc = S // C
    assert S % C == 0 and D % LANES == 0

    def tok(b, d, c):
        return (b, c + d * (nc - 1 - 2 * c), 0)

    def tok_d(b, d, c):
        return (b, d, c + d * (nc - 1 - 2 * c), 0)

    def out_map(b, d, c):
        return (d, b, c + d * (nc - 1 - 2 * c), 0)

    shared = pl.BlockSpec((1, C, D), tok)
    per_dir = pl.BlockSpec((1, 1, C, D), tok_d)
    return pl.pallas_call(
        _wkv_kernel,
        grid=(B, 2, nc),
        in_specs=[shared, shared, shared, shared, per_dir, per_dir,
                  pl.BlockSpec((1, D), lambda b, d, c: (0, 0))],
        out_specs=pl.BlockSpec((1, 1, C, D), out_map),
        out_shape=jax.ShapeDtypeStruct((2, B, S, D), F32),
        scratch_shapes=[pltpu.VMEM((D // LANES, LANES, LANES), F32)],
        compiler_params=pltpu.CompilerParams(
            dimension_semantics=("parallel", "arbitrary", "arbitrary")),
        name="wkv7_chunked",
    )(r, k, v, kk, lw, a, k_a.reshape(1, D))


FFT_N2 = 64
FFT_K1_BLOCK = 8


def _round_up(n, m):
    return (n + m - 1) // m * m


def _cat3(m, axis):
    m32 = jnp.asarray(m, F32)
    hi, lo = _split(m32)
    return jnp.concatenate([hi, hi, lo], axis=axis)


def _data3(x):
    hi, lo = _split(x)
    return jnp.concatenate([hi, lo, hi], axis=0)


@functools.lru_cache(maxsize=None)
def _fft_tables(seq_len):
    n = 2 * seq_len
    n2 = FFT_N2
    n1 = n // n2
    h1 = n1 // 2
    k1n = h1 + 1
    k1p = _round_up(k1n, FFT_K1_BLOCK)
    k1 = np.arange(k1n)[:, None]
    th1 = 2 * np.pi * k1 * np.arange(h1)[None, :] / n1
    f1 = np.zeros((2 * k1p, h1))
    f1[:k1n] = np.cos(th1)
    f1[k1p:k1p + k1n] = -np.sin(th1)
    m = np.arange(n2)
    th2 = 2 * np.pi * (m[None, None, :] * m[None, :, None] / n2 + m[None, None, :] * np.arange(k1n)[:, None, None] / n)
    gr, gi = np.cos(th2), -np.sin(th2)
    g = np.zeros((k1p, 2 * n2, 2 * n2))
    g[:k1n] = np.block([[gr, -gi], [gi, gr]])
    gt = np.transpose(g, (0, 2, 1))
    coef = np.full((k1n,), 2.0)
    coef[0] = 1.0
    coef[-1] = 1.0
    th3 = th1.T
    m3 = np.zeros((h1, 2 * k1p))
    m3[:, :k1n] = coef * np.cos(th3) / n
    m3[:, k1p:k1p + k1n] = -coef * np.sin(th3) / n
    kidx = (np.arange(k1n)[:, None] + n1 * np.arange(n2)[None, :])
    return dict(n1=n1, h1=h1, k1n=k1n, k1p=k1p, f1=f1, g=g, gt=gt, m3=m3, kidx=kidx)


def _fft_s1_kernel(f_ref, x_ref, o_ref):
    o_ref[0] = jnp.dot(f_ref[...], _data3(x_ref[0]), preferred_element_type=F32)


def _fft_mid_kernel(g_ref, gt_ref, kr_ref, ki_ref, a_ref, o_ref):
    n2 = FFT_N2
    for j in range(FFT_K1_BLOCK):
        xin = jnp.concatenate([a_ref[0, 0, j], a_ref[0, 1, j]], axis=0)
        z = jnp.dot(g_ref[j], _data3(xin), preferred_element_type=F32)
        zr, zi = z[:n2], z[n2:]
        kr, ki = kr_ref[j], ki_ref[j]
        y = jnp.concatenate([zr * kr - zi * ki, zr * ki + zi * kr], axis=0)
        b = jnp.dot(gt_ref[j], _data3(y), preferred_element_type=F32)
        o_ref[0, 0, j] = b[:n2]
        o_ref[0, 1, j] = b[n2:]


def _fft_s3_kernel(m_ref, b_ref, u_ref, x_ref, bias_ref, o_ref):
    y = jnp.dot(m_ref[...], _data3(b_ref[0]), preferred_element_type=F32)
    u = u_ref[0]
    o_ref[0] = x_ref[0] * (y + u * bias_ref[...])


def _long_conv_gated(u, gate, kf_re, kf_im, bias, lane_tile=2048):
    bsz, seq_len, ch = u.shape
    t = _fft_tables(seq_len)
    n2, h1, k1p = FFT_N2, t["h1"], t["k1p"]
    lanes = n2 * ch
    lt = min(lane_tile, lanes)
    assert lanes % lt == 0 and seq_len == h1 * n2
    params = pltpu.CompilerParams(dimension_semantics=("parallel", "parallel"),
                                  vmem_limit_bytes=VMEM_LIMIT)
    f1 = _cat3(t["f1"], 1)
    a = pl.pallas_call(
        _fft_s1_kernel,
        grid=(bsz, lanes // lt),
        in_specs=[pl.BlockSpec(f1.shape, lambda b, j: (0, 0)),
                  pl.BlockSpec((1, h1, lt), lambda b, j: (b, 0, j))],
        out_specs=pl.BlockSpec((1, 2 * k1p, lt), lambda b, j: (b, 0, j)),
        out_shape=jax.ShapeDtypeStruct((bsz, 2 * k1p, lanes), F32),
        compiler_params=params, name="hyena_dft1",
    )(f1, u.reshape(bsz, h1, lanes))
    kb = FFT_K1_BLOCK
    g = _cat3(t["g"], 2)
    gt = _cat3(t["gt"], 2)
    mat_spec = pl.BlockSpec((kb,) + g.shape[1:], lambda b, j: (j, 0, 0))
    kf_spec = pl.BlockSpec((kb, n2, ch), lambda b, j: (j, 0, 0))
    blk = pl.BlockSpec((1, 2, kb, n2, ch), lambda b, j: (b, 0, j, 0, 0))
    bm = pl.pallas_call(
        _fft_mid_kernel,
        grid=(bsz, k1p // kb),
        in_specs=[mat_spec, mat_spec, kf_spec, kf_spec, blk],
        out_specs=blk,
        out_shape=jax.ShapeDtypeStruct((bsz, 2, k1p, n2, ch), F32),
        compiler_params=params, name="hyena_dft2",
    )(g, gt, kf_re, kf_im, a.reshape(bsz, 2, k1p, n2, ch))
    m3 = _cat3(t["m3"], 1)
    row = pl.BlockSpec((1, h1, lt), lambda b, j: (b, 0, j))
    out = pl.pallas_call(
        _fft_s3_kernel,
        grid=(bsz, lanes // lt),
        in_specs=[pl.BlockSpec(m3.shape, lambda b, j: (0, 0)),
                  pl.BlockSpec((1, 2 * k1p, lt), lambda b, j: (b, 0, j)),
                  row, row, pl.BlockSpec((1, lt), lambda b, j: (0, j))],
        out_specs=row,
        out_shape=jax.ShapeDtypeStruct((bsz, h1, lanes), F32),
        compiler_params=params, name="hyena_dft3",
    )(m3, bm.reshape(bsz, 2 * k1p, lanes), u.reshape(bsz, h1, lanes), gate.reshape(bsz, h1, lanes),
      jnp.tile(bias, n2).reshape(1, lanes))
    return out.reshape(bsz, seq_len, ch)


def _filter_spectrum(filt_o, seq_len):
    t = _fft_tables(seq_len)
    k_full = jnp.concatenate([filt_o[:, 0], filt_o[::-1, 1]], axis=0)
    kf = jnp.fft.fft(k_full, axis=0)[t["kidx"].reshape(-1)]
    kf = kf.reshape(t["k1n"], FFT_N2, -1)
    pad = ((0, t["k1p"] - t["k1n"]), (0, 0), (0, 0))
    return jnp.pad(jnp.real(kf), pad), jnp.pad(jnp.imag(kf), pad)


ROUTE_COLS = 16
ROUTE_TOKEN_CHUNK = 512


def _route_kernel(aff_ref, o_ref, key_ref, val_ref, *, cap):
    a = aff_ref[0]
    n_exp, seq_len = a.shape
    bits = pltpu.bitcast(a, jnp.int32)

    def bisect(i, t):
        cand = t | jnp.left_shift(1, 30 - i)
        cnt = jnp.sum((bits >= cand).astype(jnp.int32), axis=1, keepdims=True)
        return jnp.where(cnt >= cap, cand, t)

    thr = lax.fori_loop(0, 31, bisect, jnp.zeros((n_exp, 1), jnp.int32))
    gt = bits > thr
    eq = bits == thr

    ri = lax.broadcasted_iota(jnp.int32, (LANES, LANES), 0)
    ci = lax.broadcasted_iota(jnp.int32, (LANES, LANES), 1)
    upper = (ri < ci).astype(BF16)

    def prefix(mask):
        m = mask.astype(F32)
        run = jnp.zeros((n_exp, 1), F32)
        parts = []
        for j in range(seq_len // LANES):
            tile = m[:, j * LANES:(j + 1) * LANES]
            parts.append(run + jnp.dot(tile.astype(BF16), upper, preferred_element_type=F32))
            run = run + jnp.sum(tile, axis=1, keepdims=True)
        return jnp.concatenate(parts, axis=1), run

    eq_before, _ = prefix(eq)
    n_gt = jnp.sum(gt.astype(F32), axis=1, keepdims=True)
    sel = jnp.logical_or(gt, jnp.logical_and(eq, eq_before < cap - n_gt))
    pos, _ = prefix(sel)
    key_ref[...] = jnp.where(sel, pos, -1.0)
    val_ref[...] = a

    tok = lax.broadcasted_iota(jnp.int32, (1, seq_len), 1)
    t_hi = (tok // 64).astype(F32)
    t_lo = (tok % 64).astype(F32)
    slot = lax.broadcasted_iota(jnp.int32, (cap, ROUTE_TOKEN_CHUNK), 0).astype(F32)
    zeros = jnp.zeros((ROUTE_COLS - 5, seq_len), F32)

    def compact(e, carry):
        key = key_ref[pl.ds(e, 1), :]
        g = val_ref[pl.ds(e, 1), :]
        g_hi = g.astype(BF16).astype(F32)
        g_mid = (g - g_hi).astype(BF16).astype(F32)
        g_lo = g - g_hi - g_mid
        rows = jnp.concatenate([t_hi, t_lo, g_hi, g_mid, g_lo, zeros], axis=0).astype(BF16)
        acc = jnp.zeros((cap, ROUTE_COLS), F32)
        for j in range(seq_len // ROUTE_TOKEN_CHUNK):
            sl = slice(j * ROUTE_TOKEN_CHUNK, (j + 1) * ROUTE_TOKEN_CHUNK)
            onehot = jnp.where(key[:, sl] == slot, 1.0, 0.0).astype(BF16)
            acc = acc + _dg(onehot, rows[:, sl], _NT)
        o_ref[0, pl.ds(e, 1)] = acc[None]
        return carry

    lax.fori_loop(0, n_exp, compact, 0)


def _route(aff_t, cap):
    bsz, n_exp, seq_len = aff_t.shape
    assert seq_len % ROUTE_TOKEN_CHUNK == 0 and seq_len <= 64 * 256
    out = pl.pallas_call(
        functools.partial(_route_kernel, cap=cap),
        grid=(bsz,),
        in_specs=[pl.BlockSpec((1, n_exp, seq_len), lambda b: (b, 0, 0))],
        out_specs=pl.BlockSpec((1, n_exp, cap, ROUTE_COLS), lambda b: (b, 0, 0, 0)),
        out_shape=jax.ShapeDtypeStruct((bsz, n_exp, cap, ROUTE_COLS), F32),
        scratch_shapes=[pltpu.VMEM((n_exp, seq_len), F32), pltpu.VMEM((n_exp, seq_len), F32)],
        compiler_params=pltpu.CompilerParams(dimension_semantics=("parallel",),
                                             vmem_limit_bytes=VMEM_LIMIT),
        name="ec_route",
    )(aff_t)
    idx = (out[..., 0] * 64.0 + out[..., 1]).astype(jnp.int32)
    gate = out[..., 2] + out[..., 3] + out[..., 4]
    return idx, gate


MOE_FF_TILE = 512
LN_ROWS = 256


def _moe_kernel(idx_ref, gate_ref, h_ref, wg_ref, wu_ref, wd_ref, lg_ref, lb_ref, o_ref,
                xe_ref, xb_ref, y_ref, *, alpha):
    e = pl.program_id(1)
    f = pl.program_id(2)
    last_e = pl.num_programs(1) - 1
    last_f = pl.num_programs(2) - 1
    cap = xe_ref.shape[0]
    seq_len = h_ref.shape[1]

    @pl.when(jnp.logical_and(e == 0, f == 0))
    def _():
        o_ref[...] = jnp.zeros_like(o_ref)

    @pl.when(f == 0)
    def _():
        def gather(c, carry):
            t = idx_ref[0, 0, c]
            xe_ref[pl.ds(c, 1), :] = h_ref[0, pl.ds(t, 1), :]
            return carry

        lax.fori_loop(0, cap, gather, 0, unroll=8)
        xb_ref[...] = xe_ref[...].astype(BF16)

    xb = xb_ref[...]
    a = jnp.dot(xb, wg_ref[0], preferred_element_type=F32)
    u = jnp.dot(xb, wu_ref[0], preferred_element_type=F32)
    act = (a * jax.nn.sigmoid(a) * u).astype(BF16)
    part = jnp.dot(act, wd_ref[0], preferred_element_type=F32)

    @pl.when(f == 0)
    def _():
        y_ref[...] = part

    @pl.when(f > 0)
    def _():
        y_ref[...] += part

    @pl.when(f == last_f)
    def _():
        def scatter(c, carry):
            t = idx_ref[0, 0, c]
            row = pl.ds(t, 1)
            o_ref[0, row, :] = o_ref[0, row, :] + gate_ref[0, 0, c] * y_ref[pl.ds(c, 1), :]
            return carry

        lax.fori_loop(0, cap, scatter, 0, unroll=8)

    @pl.when(jnp.logical_and(e == last_e, f == last_f))
    def _():
        def norm(i, carry):
            rows = pl.ds(pl.multiple_of(i * LN_ROWS, LN_ROWS), LN_ROWS)
            z = alpha * h_ref[0, rows, :] + o_ref[0, rows, :]
            o_ref[0, rows, :] = _layer_norm(z, lg_ref[...], lb_ref[...])
            return carry

        lax.fori_loop(0, seq_len // LN_ROWS, norm, 0)


def _moe_block(h, idx, gate, wg, wu, wd, ln_g, ln_b, alpha):
    bsz, seq_len, d = h.shape
    n_exp, _, ff = wg.shape
    cap = idx.shape[-1]
    ft = min(MOE_FF_TILE, ff)
    assert ff % ft == 0 and seq_len % LN_ROWS == 0
    smem = pl.BlockSpec((1, 1, cap), lambda b, e, f: (b * n_exp + e, 0, 0), memory_space=pltpu.SMEM)
    whole = pl.BlockSpec((1, seq_len, d), lambda b, e, f: (b, 0, 0), pipeline_mode=pl.Buffered(1))
    vec = pl.BlockSpec((1, d), lambda b, e, f: (0, 0))
    return pl.pallas_call(
        functools.partial(_moe_kernel, alpha=alpha),
        grid=(bsz, n_exp, ff // ft),
        in_specs=[smem, smem, whole,
                  pl.BlockSpec((1, d, ft), lambda b, e, f: (e, 0, f)),
                  pl.BlockSpec((1, d, ft), lambda b, e, f: (e, 0, f)),
                  pl.BlockSpec((1, ft, d), lambda b, e, f: (e, f, 0)),
                  vec, vec],
        out_specs=whole,
        out_shape=jax.ShapeDtypeStruct((bsz, seq_len, d), F32),
        scratch_shapes=[pltpu.VMEM((cap, d), F32), pltpu.VMEM((cap, d), BF16), pltpu.VMEM((cap, d), F32)],
        compiler_params=pltpu.CompilerParams(
            dimension_semantics=("arbitrary", "arbitrary", "arbitrary"), vmem_limit_bytes=VMEM_LIMIT),
        name="ec_moe",
    )(idx.reshape(bsz * n_exp, 1, cap), gate.reshape(bsz * n_exp, 1, cap), h, wg, wu, wd,
      ln_g.reshape(1, d), ln_b.reshape(1, d))


def _proj_kernel(x_ref, w_ref, o_ref):
    o_ref[...] = jnp.dot(x_ref[...].astype(BF16), w_ref[...], preferred_element_type=F32)


def _proj(x, w, tm=512):
    M, K = x.shape
    N = w.shape[1]
    assert M % tm == 0 and N % LANES == 0
    return pl.pallas_call(
        _proj_kernel,
        grid=(M // tm,),
        in_specs=[pl.BlockSpec((tm, K), lambda i: (i, 0)),
                  pl.BlockSpec((K, N), lambda i: (0, 0))],
        out_specs=pl.BlockSpec((tm, N), lambda i: (i, 0)),
        out_shape=jax.ShapeDtypeStruct((M, N), F32),
        compiler_params=pltpu.CompilerParams(dimension_semantics=("parallel",),
                                             vmem_limit_bytes=VMEM_LIMIT),
        name="proj",
    )(x, w.astype(BF16))


def _pad_cols(w, mult=LANES):
    n = w.shape[-1]
    pad = (-n) % mult
    return jnp.pad(w, ((0, 0), (0, pad))) if pad else w


LN_EPS = 1e-5
GN_EPS = 64e-5
HY_ORDER = 2
HY_TARGET = 1e-2
HY_SHORT_DECAY_PCT = 0.3
HY_LONG_DECAY_PCT = 1.5
N_EXPERTS = 16
EC_CAPACITY = 2


def _layer_norm(x, g, b):
    mu = jnp.mean(x, -1, keepdims=True)
    var = jnp.mean(jnp.square(x - mu), -1, keepdims=True)
    return (x - mu) * lax.rsqrt(var + LN_EPS) * g + b


def _neighbours(z):
    zp = jnp.pad(z, ((0, 0), (1, 0), (0, 0)))[:, :-1]
    zn = jnp.pad(z, ((0, 0), (0, 1), (0, 0)))[:, 1:]
    return zp, zn


def _hyena_filters(seq_len, w1, b1, w2, b2, w3, freq, d_hy):
    emb = w1.shape[0]
    bands = (emb - 1) // 2
    t = jnp.linspace(0.0, 1.0, seq_len, dtype=F32)[:, None]
    w = 2.0 * math.pi * jnp.arange(seq_len, dtype=F32)[:, None] / seq_len
    f = jnp.linspace(1e-4, bands - 1, bands, dtype=F32)[None, :]
    z = jnp.concatenate([t, jnp.cos(f * w), -jnp.sin(f * w)], axis=-1)
    a = jnp.sin(freq * (z @ w1 + b1))
    for i in range(w2.shape[0]):
        a = jnp.sin(freq * (a @ w2[i] + b2[i]))
    filt = (a @ w3).reshape(seq_len, HY_ORDER, 2, d_hy)
    max_decay = math.log(HY_TARGET) / HY_SHORT_DECAY_PCT
    min_decay = math.log(HY_TARGET) / HY_LONG_DECAY_PCT
    deltas = jnp.linspace(min_decay, max_decay, d_hy, dtype=F32)
    filt = filt * jnp.exp(-t * jnp.abs(deltas))[:, None, None, :]
    filt = filt * lax.rsqrt(jnp.sum(filt * filt, axis=(0, 2), keepdims=True) + 1e-12)
    return filt


def _hyena_branch(p, conv_w, conv_b, filt, bias):
    seq_len = p.shape[1]
    zp, zn = _neighbours(p)
    p = conv_w[0] * zp + conv_w[1] * p + conv_w[2] * zn + conv_b
    v, x1, x2 = jnp.split(p, 3, axis=-1)
    z = _long_conv_gated(v, x1, *_filter_spectrum(filt[:, 0], seq_len), bias[0])
    return _long_conv_gated(z, x2, *_filter_spectrum(filt[:, 1], seq_len), bias[1])


def _rwkv_branch(p, v_first, v_res, mu, w0, w2, a0, a2, g2, k_k, k_a, r_k, lnx_g, lnx_b):
    bsz, seq_len, _ = p.shape
    d_rw = w0.shape[-1]
    r_decay = w2.shape[1]
    r_aaa = a2.shape[1]
    nh = d_rw // RW_HEAD
    zp, zn = _neighbours(p)
    p = p + mu[0] * (zp - p) + mu[1] * (zn - p)
    c3 = 3 * d_rw
    c4 = c3 + 2 * r_decay
    c5 = c4 + 2 * r_aaa
    r, k, v, wd, ad, gd = jnp.split(p, [d_rw, 2 * d_rw, c3, c4, c5], axis=-1)
    if v_res is None:
        v_first = v
    else:
        v0, v1, v2 = v_res
        v = v + (v_first - v) * jax.nn.sigmoid(v0 + (v @ v1) @ v2)
    wd = wd.reshape(bsz, seq_len, 2, r_decay)
    ad = ad.reshape(bsz, seq_len, 2, r_aaa)
    wpre = w0 + jnp.einsum('bsdr,drc->bsdc', jnp.tanh(wd), w2)
    lw = -jnp.exp(-jax.nn.softplus(-wpre) - 0.5)
    a = jax.nn.sigmoid(a0 + jnp.einsum('bsdr,drc->bsdc', ad, a2))
    g = jax.nn.sigmoid(gd) @ g2

    def heads(t):
        return t.reshape(bsz, seq_len, nh, RW_HEAD)

    kk = heads(k * k_k)
    kk = (kk / jnp.maximum(jnp.sqrt(jnp.sum(kk * kk, -1, keepdims=True)), 1e-12)).reshape(bsz, seq_len, d_rw)
    y2 = wkv7(r, k, v, kk, jnp.swapaxes(lw, 1, 2), jnp.swapaxes(a, 1, 2), k_a)
    y = heads(y2[0] + y2[1])
    mu_y = jnp.mean(y, -1, keepdims=True)
    var_y = jnp.mean(jnp.square(y - mu_y), -1, keepdims=True)
    yn = ((y - mu_y) * lax.rsqrt(var_y + GN_EPS)).reshape(bsz, seq_len, d_rw) * lnx_g + lnx_b
    k_sum = k[:, :, None, :] * (1.0 + (a - 1.0) * k_a)
    k_sum = k_sum[:, :, 0] + k_sum[:, :, 1]
    bonus = jnp.sum(heads(r) * heads(k_sum) * r_k, -1, keepdims=True) * heads(v)
    out = (yn + bonus.reshape(bsz, seq_len, d_rw)) * g
    return out, v_first


def _moe_layer(h, w_router, b_router, wg, wu, wd, ln_g, ln_b, alpha):
    seq_len = h.shape[1]
    cap = EC_CAPACITY * seq_len // N_EXPERTS
    aff = jax.nn.softmax(h @ w_router + b_router, axis=-1)
    idx, gate = _route(jnp.swapaxes(aff, 1, 2), cap)
    return _moe_block(h, idx, gate, wg, wu, wd, ln_g, ln_b, alpha)


def kernel(x, ln0_g, ln0_b, w_in, hy_conv_w, hy_conv_b, hy_ffn_w1, hy_ffn_b1, hy_ffn_w2, hy_ffn_b2,
           hy_ffn_w3, hy_freq, hy_bias, rw_mu, rw_w0, rw_w2, rw_a0, rw_a2, rw_v0, rw_v1, rw_v2, rw_g2,
           rw_k_k, rw_k_a, rw_r_k, rw_lnx_g, rw_lnx_b, w_o_hy, w_o_rw, w_out, ln1_g, ln1_b,
           w_router, b_router, w_gate, w_up, w_down, ln2_g, ln2_b):
    bsz, seq_len, d_model = x.shape
    depth = w_in.shape[0]
    d_hy = hy_bias.shape[-1]
    p_hy = 3 * d_hy
    p_rw = rw_mu.shape[-1]
    alpha = (2 * depth) ** 0.25
    tokens = bsz * seq_len
    h = _layer_norm(x, ln0_g, ln0_b)
    v_first = None
    for l in range(depth):
        p = _proj(h.reshape(tokens, d_model), _pad_cols(w_in[l])).reshape(bsz, seq_len, -1)
        pa = p[..., :p_hy]
        pb = p[..., p_hy:p_hy + p_rw]
        pg = p[..., p_hy + p_rw:p_hy + p_rw + 2 * d_model]
        filt = _hyena_filters(seq_len, hy_ffn_w1[l], hy_ffn_b1[l], hy_ffn_w2[l], hy_ffn_b2[l],
                              hy_ffn_w3[l], hy_freq[l], d_hy)
        y_hy = _hyena_branch(pa, hy_conv_w[l], hy_conv_b[l], filt, hy_bias[l])
        v_res = None if l == 0 else (rw_v0[l - 1], rw_v1[l - 1], rw_v2[l - 1])
        y_rw, v_first = _rwkv_branch(pb, v_first, v_res, rw_mu[l], rw_w0[l], rw_w2[l], rw_a0[l], rw_a2[l],
                                     rw_g2[l], rw_k_k[l], rw_k_a[l], rw_r_k[l], rw_lnx_g[l], rw_lnx_b[l])
        gates = jax.nn.sigmoid(pg)
        g_hy = gates[..., :d_model]
        g_rw = gates[..., d_model:]
        o_hy = _proj(y_hy.reshape(tokens, -1), w_o_hy[l]).reshape(bsz, seq_len, d_model)
        o_rw = _proj(y_rw.reshape(tokens, -1), w_o_rw[l]).reshape(bsz, seq_len, d_model)
        merged = g_hy * o_hy + g_rw * o_rw
        mo = _proj(merged.reshape(tokens, d_model), w_out[l]).reshape(bsz, seq_len, d_model)
        h = _layer_norm(alpha * h + mo, ln1_g[l], ln1_b[l])
        h = _moe_layer(h, w_router[l], b_router[l], w_gate[l].astype(BF16), w_up[l].astype(BF16),
                       w_down[l].astype(BF16), ln2_g[l], ln2_b[l], alpha)
    return h
```

```python
import functools
import math

import jax
import jax.numpy as jnp
import numpy as np
from jax import lax
from jax.experimental import pallas as pl
from jax.experimental.pallas import tpu as pltpu

F32 = jnp.float32
BF16 = jnp.bfloat16

RW_HEAD = 64
LANES = 128
WKV_CHUNK = 64
VMEM_LIMIT = 56 * 1024 * 1024


def _split(x):
    hi = x.astype(BF16)
    lo = (x - hi.astype(F32)).astype(BF16)
    return hi, lo


def _dg(a, b, dims):
    return lax.dot_general(a, b, (dims, ((), ())), preferred_element_type=F32)


def _mm(a, b, dims=((1,), (0,)), passes=1):
    if passes == 1:
        return _dg(a.astype(BF16), b.astype(BF16), dims)
    ah, al = _split(a)
    bh, bl = _split(b)
    return _dg(ah, bh, dims) + (_dg(ah, bl, dims) + _dg(al, bh, dims))


_NN = ((1,), (0,))
_NT = ((1,), (1,))
_TN = ((0,), (0,))


def _wkv_kernel(r_ref, k_ref, v_ref, kk_ref, lw_ref, a_ref, ka_ref, y_ref, h_ref):
    C = WKV_CHUNK
    d = pl.program_id(1)
    c = pl.program_id(2)

    @pl.when(c == 0)
    def _():
        h_ref[...] = jnp.zeros_like(h_ref)

    r = r_ref[0]
    k = k_ref[0]
    v = v_ref[0]
    kk = kk_ref[0]
    lw = lw_ref[0, 0]
    a = a_ref[0, 0]
    ka = ka_ref[...]
    n_pairs = r.shape[1] // LANES

    row = lax.broadcasted_iota(jnp.int32, (C, C), 0)
    col = lax.broadcasted_iota(jnp.int32, (C, C), 1)
    sgn = 1 - 2 * d
    incl = (col - row) * sgn <= 0
    row2 = lax.broadcasted_iota(jnp.int32, (C, 2 * C), 0)
    col2 = lax.broadcasted_iota(jnp.int32, (C, 2 * C), 1) % C
    earlier2 = (col2 - row2) * sgn < 0
    incl2 = (col2 - row2) * sgn <= 0

    lw_hi, lw_lo = _split(lw)
    tri = incl.astype(BF16)
    cin = _dg(tri, lw_hi, _NN) + _dg(tri, lw_lo, _NN)
    tot = jnp.sum(lw, axis=0, keepdims=True)
    half = 0.5 * tot
    e0 = jnp.exp(half)
    e1 = jnp.exp(cin - half)
    e2 = jnp.exp(half - cin)
    ew = jnp.exp(-lw)
    kdir = k * (1.0 + (a - 1.0) * ka)
    rt = r * e1
    bt = kk * (e1 * ew)
    kt = kdir * e2
    at = -(kk * a) * e2
    r0 = rt * e0
    b0 = bt * e0
    kh = kt * e0
    ah = at * e0
    e0sq = e0 * e0

    lane = lax.broadcasted_iota(jnp.int32, (1, LANES), 1)
    first = lane < RW_HEAD
    ri = lax.broadcasted_iota(jnp.int32, (LANES, LANES), 0)
    ci = lax.broadcasted_iota(jnp.int32, (LANES, LANES), 1)
    eye = ri == ci
    same_head = (ri // RW_HEAD) == (ci // RW_HEAD)

    def stack2(x):
        return jnp.concatenate([jnp.where(first, x, 0.0), jnp.where(first, 0.0, x)], axis=0)

    pairs = range(n_pairs)
    sls = [slice(p * LANES, (p + 1) * LANES) for p in pairs]
    g = [_mm(jnp.concatenate([bt[:, sl], rt[:, sl]], axis=0),
             jnp.concatenate([stack2(kt[:, sl]), stack2(at[:, sl])], axis=0), _NT) for sl in sls]
    hp = [h_ref[p] for p in pairs]
    bh = [_mm(jnp.concatenate([b0[:, sl], r0[:, sl]], axis=0), hp[p]) for p, sl in zip(pairs, sls)]
    v2 = [stack2(v[:, sl]) for sl in sls]
    x = [bh[p][:C] + _mm(jnp.where(earlier2, g[p][:C, :2 * C], 0.0), v2[p]) for p in pairs]
    pw = [jnp.where(earlier2, g[p][:C, 2 * C:], 0.0) for p in pairs]
    n_steps = C.bit_length() - 1
    for i in range(n_steps):
        x = [x[p] + _mm(pw[p], stack2(x[p])) for p in pairs]
        if i + 1 < n_steps:
            pw = [_mm(pw[p], stack2(pw[p])) for p in pairs]
    for p, sl in zip(pairs, sls):
        a_r = jnp.concatenate([jnp.where(incl2, g[p][C:, :2 * C], 0.0),
                               jnp.where(incl2, g[p][C:, 2 * C:], 0.0)], axis=1)
        y_ref[0, 0, :, sl] = bh[p][C:] + _mm(a_r, jnp.concatenate([v2[p], stack2(x[p])], axis=0))
    for p, sl in zip(pairs, sls):
        dg = jnp.where(eye, jnp.broadcast_to(e0sq[:, sl], (LANES, LANES)), 0.0)
        hn = _mm(jnp.concatenate([kh[:, sl], ah[:, sl], dg], axis=0),
                 jnp.concatenate([v[:, sl], x[p], hp[p]], axis=0), _TN)
        h_ref[p] = jnp.where(same_head, hn, 0.0)


def wkv7(r, k, v, kk, lw, a, k_a):
    B, S, D = r.shape
    C = WKV_CHUNK
    nc = S // C
    assert S % C == 0 and D % LANES == 0

    def tok(b, d, c):
        return (b, c + d * (nc - 1 - 2 * c), 0)

    def tok_d(b, d, c):
        return (b, d, c + d * (nc - 1 - 2 * c), 0)

    def out_map(b, d, c):
        return (d, b, c + d * (nc - 1 - 2 * c), 0)

    shared = pl.BlockSpec((1, C, D), tok)
    per_dir = pl.BlockSpec((1, 1, C, D), tok_d)
    return pl.pallas_call(
        _wkv_kernel,
        grid=(B, 2, nc),
        in_specs=[shared, shared, shared, shared, per_dir, per_dir,
                  pl.BlockSpec((1, D), lambda b, d, c: (0, 0))],
        out_specs=pl.BlockSpec((1, 1, C, D), out_map),
        out_shape=jax.ShapeDtypeStruct((2, B, S, D), F32),
        scratch_shapes=[pltpu.VMEM((D // LANES, LANES, LANES), F32)],
        compiler_params=pltpu.CompilerParams(
            dimension_semantics=("parallel", "arbitrary", "arbitrary")),
        name="wkv7_chunked",
    )(r, k, v, kk, lw, a, k_a.reshape(1, D))


FFT_N2 = 64
FFT_K1_BLOCK = 8


def _round_up(n, m):
    return (n + m - 1) // m * m


def _cat3(m, axis):
    m32 = jnp.asarray(m, F32)
    hi, lo = _split(m32)
    return jnp.concatenate([hi, hi, lo], axis=axis)


def _data3(x):
    hi, lo = _split(x)
    return jnp.concatenate([hi, lo, hi], axis=0)


@functools.lru_cache(maxsize=None)
def _fft_tables(seq_len):
    n = 2 * seq_len
    n2 = FFT_N2
    n1 = n // n2
    h1 = n1 // 2
    k1n = h1 + 1
    k1p = _round_up(k1n, FFT_K1_BLOCK)
    k1 = np.arange(k1n)[:, None]
    th1 = 2 * np.pi * k1 * np.arange(h1)[None, :] / n1
    f1 = np.zeros((2 * k1p, h1))
    f1[:k1n] = np.cos(th1)
    f1[k1p:k1p + k1n] = -np.sin(th1)
    m = np.arange(n2)
    th2 = 2 * np.pi * (m[None, None, :] * m[None, :, None] / n2 + m[None, None, :] * np.arange(k1n)[:, None, None] / n)
    gr, gi = np.cos(th2), -np.sin(th2)
    g = np.zeros((k1p, 2 * n2, 2 * n2))
    g[:k1n] = np.block([[gr, -gi], [gi, gr]])
    gt = np.transpose(g, (0, 2, 1))
    coef = np.full((k1n,), 2.0)
    coef[0] = 1.0
    coef[-1] = 1.0
    th3 = th1.T
    m3 = np.zeros((h1, 2 * k1p))
    m3[:, :k1n] = coef * np.cos(th3) / n
    m3[:, k1p:k1p + k1n] = -coef * np.sin(th3) / n
    kidx = (np.arange(k1n)[:, None] + n1 * np.arange(n2)[None, :])
    return dict(n1=n1, h1=h1, k1n=k1n, k1p=k1p, f1=f1, g=g, gt=gt, m3=m3, kidx=kidx)


FFT_N2_BLOCK = 8


def _fft_s1_kernel(f_ref, x_ref, o_ref):
    res = jnp.dot(f_ref[...], _data3(x_ref[0]), preferred_element_type=F32)
    ch = o_ref.shape[3]
    for i in range(FFT_N2_BLOCK):
        o_ref[0, :, i, :] = res[:, i * ch:(i + 1) * ch]


def _fft_mid_kernel(g_ref, gt_ref, kr_ref, ki_ref, a_ref, o_ref):
    n2 = FFT_N2
    for j in range(FFT_K1_BLOCK):
        xin = jnp.concatenate([a_ref[0, 0, j], a_ref[0, 1, j]], axis=0)
        z = jnp.dot(g_ref[j], _data3(xin), preferred_element_type=F32)
        zr, zi = z[:n2], z[n2:]
        kr, ki = kr_ref[j], ki_ref[j]
        y = jnp.concatenate([zr * kr - zi * ki, zr * ki + zi * kr], axis=0)
        b = jnp.dot(gt_ref[j], _data3(y), preferred_element_type=F32)
        o_ref[0, 0, j] = b[:n2]
        o_ref[0, 1, j] = b[n2:]


def _fft_s3_kernel(m_ref, b_ref, u_ref, x_ref, bias_ref, o_ref):
    bm = jnp.concatenate([b_ref[0, :, i, :] for i in range(FFT_N2_BLOCK)], axis=1)
    y = jnp.dot(m_ref[...], _data3(bm), preferred_element_type=F32)
    u = u_ref[0]
    o_ref[0] = x_ref[0] * (y + u * bias_ref[...])


def _long_conv_gated(u, gate, kf_re, kf_im, bias):
    bsz, seq_len, ch = u.shape
    t = _fft_tables(seq_len)
    n2, h1, k1p = FFT_N2, t["h1"], t["k1p"]
    lanes = n2 * ch
    nb = FFT_N2_BLOCK
    lt = nb * ch
    assert seq_len == h1 * n2 and ch % LANES == 0
    params = pltpu.CompilerParams(dimension_semantics=("parallel", "parallel"),
                                  vmem_limit_bytes=VMEM_LIMIT)
    spec4 = pl.BlockSpec((1, 2 * k1p, nb, ch), lambda b, j: (b, 0, j, 0))
    f1 = _cat3(t["f1"], 1)
    a = pl.pallas_call(
        _fft_s1_kernel,
        grid=(bsz, n2 // nb),
        in_specs=[pl.BlockSpec(f1.shape, lambda b, j: (0, 0)),
                  pl.BlockSpec((1, h1, lt), lambda b, j: (b, 0, j))],
        out_specs=spec4,
        out_shape=jax.ShapeDtypeStruct((bsz, 2 * k1p, n2, ch), F32),
        compiler_params=params, name="hyena_dft1",
    )(f1, u.reshape(bsz, h1, lanes))
    kb = FFT_K1_BLOCK
    g = _cat3(t["g"], 2)
    gt = _cat3(t["gt"], 2)
    mat_spec = pl.BlockSpec((kb,) + g.shape[1:], lambda b, j: (j, 0, 0))
    kf_spec = pl.BlockSpec((kb, n2, ch), lambda b, j: (j, 0, 0))
    blk = pl.BlockSpec((1, 2, kb, n2, ch), lambda b, j: (b, 0, j, 0, 0))
    bm = pl.pallas_call(
        _fft_mid_kernel,
        grid=(bsz, k1p // kb),
        in_specs=[mat_spec, mat_spec, kf_spec, kf_spec, blk],
        out_specs=blk,
        out_shape=jax.ShapeDtypeStruct((bsz, 2, k1p, n2, ch), F32),
        compiler_params=params, name="hyena_dft2",
    )(g, gt, kf_re, kf_im, a.reshape(bsz, 2, k1p, n2, ch))
    m3 = _cat3(t["m3"], 1)
    row = pl.BlockSpec((1, h1, lt), lambda b, j: (b, 0, j))
    out = pl.pallas_call(
        _fft_s3_kernel,
        grid=(bsz, n2 // nb),
        in_specs=[pl.BlockSpec(m3.shape, lambda b, j: (0, 0)), spec4,
                  row, row, pl.BlockSpec((1, lt), lambda b, j: (0, j))],
        out_specs=row,
        out_shape=jax.ShapeDtypeStruct((bsz, h1, lanes), F32),
        compiler_params=params, name="hyena_dft3",
    )(m3, bm.reshape(bsz, 2 * k1p, n2, ch), u.reshape(bsz, h1, lanes), gate.reshape(bsz, h1, lanes),
      jnp.tile(bias, n2).reshape(1, lanes))
    return out.reshape(bsz, seq_len, ch)


def _filter_spectrum(filt_o, seq_len):
    t = _fft_tables(seq_len)
    k_full = jnp.concatenate([filt_o[:, 0], filt_o[::-1, 1]], axis=0)
    kf = jnp.fft.fft(k_full, axis=0)[t["kidx"].reshape(-1)]
    kf = kf.reshape(t["k1n"], FFT_N2, -1)
    pad = ((0, t["k1p"] - t["k1n"]), (0, 0), (0, 0))
    return jnp.pad(jnp.real(kf), pad), jnp.pad(jnp.imag(kf), pad)


ROUTE_COLS = 16
ROUTE_TOKEN_CHUNK = 512


def _route_kernel(aff_ref, o_ref, key_ref, val_ref, *, cap):
    a = aff_ref[0]
    n_exp, seq_len = a.shape
    bits = pltpu.bitcast(a, jnp.int32)

    def bisect(i, t):
        cand = t | jnp.left_shift(1, 30 - i)
        cnt = jnp.sum((bits >= cand).astype(jnp.int32), axis=1, keepdims=True)
        return jnp.where(cnt >= cap, cand, t)

    thr = lax.fori_loop(0, 31, bisect, jnp.zeros((n_exp, 1), jnp.int32))
    gt = bits > thr
    eq = bits == thr

    ri = lax.broadcasted_iota(jnp.int32, (LANES, LANES), 0)
    ci = lax.broadcasted_iota(jnp.int32, (LANES, LANES), 1)
    upper = (ri < ci).astype(BF16)

    def prefix(mask):
        m = mask.astype(F32)
        run = jnp.zeros((n_exp, 1), F32)
        parts = []
        for j in range(seq_len // LANES):
            tile = m[:, j * LANES:(j + 1) * LANES]
            parts.append(run + jnp.dot(tile.astype(BF16), upper, preferred_element_type=F32))
            run = run + jnp.sum(tile, axis=1, keepdims=True)
        return jnp.concatenate(parts, axis=1), run

    eq_before, _ = prefix(eq)
    n_gt = jnp.sum(gt.astype(F32), axis=1, keepdims=True)
    sel = jnp.logical_or(gt, jnp.logical_and(eq, eq_before < cap - n_gt))
    pos, _ = prefix(sel)
    key_ref[...] = jnp.where(sel, pos, -1.0)
    val_ref[...] = a

    tok = lax.broadcasted_iota(jnp.int32, (1, seq_len), 1)
    t_hi = (tok // 64).astype(F32)
    t_lo = (tok % 64).astype(F32)
    slot = lax.broadcasted_iota(jnp.int32, (cap, ROUTE_TOKEN_CHUNK), 0).astype(F32)
    zeros = jnp.zeros((ROUTE_COLS - 5, seq_len), F32)

    def compact(e, carry):
        key = key_ref[pl.ds(e, 1), :]
        g = val_ref[pl.ds(e, 1), :]
        g_hi = g.astype(BF16).astype(F32)
        g_mid = (g - g_hi).astype(BF16).astype(F32)
        g_lo = g - g_hi - g_mid
        rows = jnp.concatenate([t_hi, t_lo, g_hi, g_mid, g_lo, zeros], axis=0).astype(BF16)
        acc = jnp.zeros((cap, ROUTE_COLS), F32)
        for j in range(seq_len // ROUTE_TOKEN_CHUNK):
            sl = slice(j * ROUTE_TOKEN_CHUNK, (j + 1) * ROUTE_TOKEN_CHUNK)
            onehot = jnp.where(key[:, sl] == slot, 1.0, 0.0).astype(BF16)
            acc = acc + _dg(onehot, rows[:, sl], _NT)
        o_ref[0, pl.ds(e, 1)] = acc[None]
        return carry

    lax.fori_loop(0, n_exp, compact, 0)


def _route(aff_t, cap):
    bsz, n_exp, seq_len = aff_t.shape
    assert seq_len % ROUTE_TOKEN_CHUNK == 0 and seq_len <= 64 * 256
    out = pl.pallas_call(
        functools.partial(_route_kernel, cap=cap),
        grid=(bsz,),
        in_specs=[pl.BlockSpec((1, n_exp, seq_len), lambda b: (b, 0, 0))],
        out_specs=pl.BlockSpec((1, n_exp, cap, ROUTE_COLS), lambda b: (b, 0, 0, 0)),
        out_shape=jax.ShapeDtypeStruct((bsz, n_exp, cap, ROUTE_COLS), F32),
        scratch_shapes=[pltpu.VMEM((n_exp, seq_len), F32), pltpu.VMEM((n_exp, seq_len), F32)],
        compiler_params=pltpu.CompilerParams(dimension_semantics=("parallel",),
                                             vmem_limit_bytes=VMEM_LIMIT),
        name="ec_route",
    )(aff_t)
    idx = (out[..., 0] * 64.0 + out[..., 1]).astype(jnp.int32)
    gate = out[..., 2] + out[..., 3] + out[..., 4]
    return idx, gate


MOE_FF_TILE = 512
LN_ROWS = 256


def _moe_kernel(idx_ref, gate_ref, h_ref, wg_ref, wu_ref, wd_ref, lg_ref, lb_ref, o_ref,
                xe_ref, xb_ref, y_ref, *, alpha):
    e = pl.program_id(1)
    f = pl.program_id(2)
    last_e = pl.num_programs(1) - 1
    last_f = pl.num_programs(2) - 1
    cap = xe_ref.shape[0]
    seq_len = h_ref.shape[1]

    @pl.when(jnp.logical_and(e == 0, f == 0))
    def _():
        o_ref[...] = jnp.zeros_like(o_ref)

    @pl.when(f == 0)
    def _():
        def gather(c, carry):
            t = idx_ref[0, 0, c]
            xe_ref[pl.ds(c, 1), :] = h_ref[0, pl.ds(t, 1), :]
            return carry

        lax.fori_loop(0, cap, gather, 0, unroll=8)
        xb_ref[...] = xe_ref[...].astype(BF16)

    xb = xb_ref[...]
    a = jnp.dot(xb, wg_ref[0], preferred_element_type=F32)
    u = jnp.dot(xb, wu_ref[0], preferred_element_type=F32)
    act = (a * jax.nn.sigmoid(a) * u).astype(BF16)
    part = jnp.dot(act, wd_ref[0], preferred_element_type=F32)

    @pl.when(f == 0)
    def _():
        y_ref[...] = part

    @pl.when(f > 0)
    def _():
        y_ref[...] += part

    @pl.when(f == last_f)
    def _():
        def scatter(c, carry):
            t = idx_ref[0, 0, c]
            row = pl.ds(t, 1)
            o_ref[0, row, :] = o_ref[0, row, :] + gate_ref[0, 0, c] * y_ref[pl.ds(c, 1), :]
            return carry

        lax.fori_loop(0, cap, scatter, 0, unroll=8)

    @pl.when(jnp.logical_and(e == last_e, f == last_f))
    def _():
        def norm(i, carry):
            rows = pl.ds(pl.multiple_of(i * LN_ROWS, LN_ROWS), LN_ROWS)
            z = alpha * h_ref[0, rows, :] + o_ref[0, rows, :]
            o_ref[0, rows, :] = _layer_norm(z, lg_ref[...], lb_ref[...])
            return carry

        lax.fori_loop(0, seq_len // LN_ROWS, norm, 0)


def _moe_block(h, idx, gate, wg, wu, wd, ln_g, ln_b, alpha):
    bsz, seq_len, d = h.shape
    n_exp, _, ff = wg.shape
    cap = idx.shape[-1]
    ft = min(MOE_FF_TILE, ff)
    assert ff % ft == 0 and seq_len % LN_ROWS == 0
    smem = pl.BlockSpec((1, 1, cap), lambda b, e, f: (b * n_exp + e, 0, 0), memory_space=pltpu.SMEM)
    whole = pl.BlockSpec((1, seq_len, d), lambda b, e, f: (b, 0, 0), pipeline_mode=pl.Buffered(1))
    vec = pl.BlockSpec((1, d), lambda b, e, f: (0, 0))
    return pl.pallas_call(
        functools.partial(_moe_kernel, alpha=alpha),
        grid=(bsz, n_exp, ff // ft),
        in_specs=[smem, smem, whole,
                  pl.BlockSpec((1, d, ft), lambda b, e, f: (e, 0, f)),
                  pl.BlockSpec((1, d, ft), lambda b, e, f: (e, 0, f)),
                  pl.BlockSpec((1, ft, d), lambda b, e, f: (e, f, 0)),
                  vec, vec],
        out_specs=whole,
        out_shape=jax.ShapeDtypeStruct((bsz, seq_len, d), F32),
        scratch_shapes=[pltpu.VMEM((cap, d), F32), pltpu.VMEM((cap, d), BF16), pltpu.VMEM((cap, d), F32)],
        compiler_params=pltpu.CompilerParams(
            dimension_semantics=("arbitrary", "arbitrary", "arbitrary"), vmem_limit_bytes=VMEM_LIMIT),
        name="ec_moe",
    )(idx.reshape(bsz * n_exp, 1, cap), gate.reshape(bsz * n_exp, 1, cap), h, wg, wu, wd,
      ln_g.reshape(1, d), ln_b.reshape(1, d))


ROW_TILE = 512
MIX_TILE = 256


def _proj_kernel(x_ref, w_ref, *o_refs):
    xb = x_ref[...].astype(BF16)
    off = 0
    for o_ref in o_refs:
        n = o_ref.shape[1]
        o_ref[...] = jnp.dot(xb, w_ref[:, off:off + n], preferred_element_type=F32)
        off += n


def _proj(x, w, widths):
    M, K = x.shape
    tm = ROW_TILE
    assert M % tm == 0 and sum(widths) == w.shape[1] and all(n % LANES == 0 for n in widths)
    return pl.pallas_call(
        _proj_kernel,
        grid=(M // tm,),
        in_specs=[pl.BlockSpec((tm, K), lambda i: (i, 0)),
                  pl.BlockSpec(w.shape, lambda i: (0, 0), pipeline_mode=pl.Buffered(1))],
        out_specs=[pl.BlockSpec((tm, n), lambda i: (i, 0)) for n in widths],
        out_shape=[jax.ShapeDtypeStruct((M, n), F32) for n in widths],
        compiler_params=pltpu.CompilerParams(dimension_semantics=("parallel",),
                                             vmem_limit_bytes=VMEM_LIMIT),
        name="proj",
    )(x, w.astype(BF16))


def _pad_to(w, n, axis):
    pad = [(0, 0)] * w.ndim
    pad[axis] = (0, n - w.shape[axis])
    return jnp.pad(w, pad)


def _ln_kernel(x_ref, g_ref, b_ref, o_ref):
    o_ref[...] = _layer_norm(x_ref[...], g_ref[...], b_ref[...])


def _ln_rows(x, g, b):
    M, D = x.shape
    row = pl.BlockSpec((ROW_TILE, D), lambda i: (i, 0))
    vec = pl.BlockSpec((1, D), lambda i: (0, 0))
    return pl.pallas_call(
        _ln_kernel, grid=(M // ROW_TILE,), in_specs=[row, vec, vec], out_specs=row,
        out_shape=jax.ShapeDtypeStruct((M, D), F32),
        compiler_params=pltpu.CompilerParams(dimension_semantics=("parallel",)),
        name="ln_in",
    )(x, g.reshape(1, D), b.reshape(1, D))


def _halo_specs(ts, width, seq_len):
    per = ts // 8
    last = seq_len // 8 - 1
    cur = pl.BlockSpec((1, ts, width), lambda b, i: (b, i, 0))
    prev = pl.BlockSpec((1, 8, width), lambda b, i: (b, jnp.maximum(i * per - 1, 0), 0))
    nxt = pl.BlockSpec((1, 8, width), lambda b, i: (b, jnp.minimum((i + 1) * per, last), 0))
    return [cur, prev, nxt]


def _neighbours(x, prev_blk, next_blk):
    ts = x.shape[0]
    i = pl.program_id(1)
    rid = lax.broadcasted_iota(jnp.int32, x.shape, 0)
    prev_row = jnp.where(i > 0, prev_blk[7:8, :], 0.0)
    next_row = jnp.where(i < pl.num_programs(1) - 1, next_blk[0:1, :], 0.0)
    zp = jnp.where(rid == 0, prev_row, pltpu.roll(x, 1, 0))
    zn = jnp.where(rid == ts - 1, next_row, pltpu.roll(x, ts - 1, 0))
    return zp, zn


def _shortconv_kernel(p_ref, pp_ref, pn_ref, w_ref, b_ref, v_ref, x1_ref, x2_ref):
    p = p_ref[0]
    zp, zn = _neighbours(p, pp_ref[0], pn_ref[0])
    q = w_ref[0:1, :] * zp + w_ref[1:2, :] * p + w_ref[2:3, :] * zn + b_ref[...]
    d = v_ref.shape[2]
    v_ref[0] = q[:, :d]
    x1_ref[0] = q[:, d:2 * d]
    x2_ref[0] = q[:, 2 * d:]


def _shortconv(p, conv_w, conv_b):
    bsz, seq_len, width = p.shape
    d = width // 3
    ts = ROW_TILE
    out = pl.BlockSpec((1, ts, d), lambda b, i: (b, i, 0))
    return pl.pallas_call(
        _shortconv_kernel,
        grid=(bsz, seq_len // ts),
        in_specs=_halo_specs(ts, width, seq_len) + [pl.BlockSpec((3, width), lambda b, i: (0, 0)),
                                                     pl.BlockSpec((1, width), lambda b, i: (0, 0))],
        out_specs=[out, out, out],
        out_shape=[jax.ShapeDtypeStruct((bsz, seq_len, d), F32)] * 3,
        compiler_params=pltpu.CompilerParams(dimension_semantics=("parallel", "parallel")),
        name="hyena_shortconv",
    )(p, p, p, conv_w, conv_b.reshape(1, width))


def _head_sum(x, bd):
    hi, lo = _split(x)
    return _dg(hi, bd, _NN) + _dg(lo, bd, _NN)


def _softplus(z):
    return jnp.maximum(z, 0.0) + jnp.log(1.0 + jnp.exp(-jnp.abs(z)))


def _rwkv_prep_kernel(*refs, has_res):
    (p_ref, pp_ref, pn_ref, q_ref, qp_ref, qn_ref) = refs[:6]
    refs = refs[6:]
    if has_res:
        vf_ref, refs = refs[0], refs[1:]
    (mu_ref, mul_ref, w2_ref, w0_ref, a2_ref, a0_ref, g2_ref, kk_ref, v0_ref, v1_ref, v2_ref, bd_ref,
     r_out, k_out, v_out, kn_out, lw_out, a_out, g_out) = refs
    d = r_out.shape[2]
    p = p_ref[0]
    zp, zn = _neighbours(p, pp_ref[0], pn_ref[0])
    p = p + mu_ref[0:1, :] * (zp - p) + mu_ref[1:2, :] * (zn - p)
    q = q_ref[0]
    zp, zn = _neighbours(q, qp_ref[0], qn_ref[0])
    q = q + mul_ref[0:1, :] * (zp - q) + mul_ref[1:2, :] * (zn - q)
    r, k, v = p[:, :d], p[:, d:2 * d], p[:, 2 * d:]
    if has_res:
        mix = _dg(_dg(v.astype(BF16), v1_ref[...], _NN).astype(BF16), v2_ref[...], _NN)
        v = v + (vf_ref[0] - v) * jax.nn.sigmoid(v0_ref[...] + mix)
    wd, ad, gd = q[:, :LANES], q[:, LANES:2 * LANES], q[:, 2 * LANES:]
    wpre = w0_ref[...] + _dg(jnp.tanh(wd).astype(BF16), w2_ref[...], _NN)
    lw = -jnp.exp(-_softplus(-wpre) - 0.5)
    a = jax.nn.sigmoid(a0_ref[...] + _dg(ad.astype(BF16), a2_ref[...], _NN))
    g = _dg(jax.nn.sigmoid(gd).astype(BF16), g2_ref[...], _NN)
    kq = k * kk_ref[...]
    norm = jnp.sqrt(_head_sum(kq * kq, bd_ref[...]))
    r_out[0] = r
    k_out[0] = k
    v_out[0] = v
    kn_out[0] = kq / jnp.maximum(norm, 1e-12)
    lw_out[0, 0] = lw[:, :d]
    lw_out[0, 1] = lw[:, d:]
    a_out[0, 0] = a[:, :d]
    a_out[0, 1] = a[:, d:]
    g_out[0] = g


def _block_diag2(m):
    z = jnp.zeros_like(m[0])
    return jnp.concatenate([jnp.concatenate([m[0], z], axis=1), jnp.concatenate([z, m[1]], axis=1)], axis=0)


def _head_ones(d):
    i = np.arange(d) // RW_HEAD
    return jnp.asarray(i[:, None] == i[None, :], BF16)


def _rwkv_prep(p_rkv, p_lora, v_first, v_res, mu_rkv, mu_lora, w0, w2, a0, a2, g2, k_k):
    bsz, seq_len, w3 = p_rkv.shape
    d = w3 // 3
    wl = p_lora.shape[2]
    ts = ROW_TILE
    assert 2 * w2.shape[1] == LANES and 2 * a2.shape[1] == LANES
    has_res = v_res is not None
    const = lambda shape: pl.BlockSpec(shape, lambda b, i: (0,) * len(shape))
    tile = pl.BlockSpec((1, ts, d), lambda b, i: (b, i, 0))
    pair = pl.BlockSpec((1, 2, ts, d), lambda b, i: (b, 0, i, 0))
    if has_res:
        v0, v1, v2 = v_res
        rank = _round_up(v1.shape[1], LANES)
        v0, v1, v2 = v0.reshape(1, d), _pad_to(v1, rank, 1).astype(BF16), _pad_to(v2, rank, 0).astype(BF16)
    else:
        v0, v1, v2 = jnp.zeros((1, d), F32), jnp.zeros((d, LANES), BF16), jnp.zeros((LANES, d), BF16)
    params = [mu_rkv, mu_lora, _block_diag2(w2).astype(BF16), w0.reshape(1, 2 * d),
              _block_diag2(a2).astype(BF16), a0.reshape(1, 2 * d),
              _pad_to(g2, wl - 2 * LANES, 0).astype(BF16), k_k.reshape(1, d), v0, v1, v2, _head_ones(d)]
    args = [p_rkv, p_rkv, p_rkv, p_lora, p_lora, p_lora] + ([v_first] if has_res else []) + params
    tok = jax.ShapeDtypeStruct((bsz, seq_len, d), F32)
    two = jax.ShapeDtypeStruct((bsz, 2, seq_len, d), F32)
    return pl.pallas_call(
        functools.partial(_rwkv_prep_kernel, has_res=has_res),
        grid=(bsz, seq_len // ts),
        in_specs=(_halo_specs(ts, w3, seq_len) + _halo_specs(ts, wl, seq_len) + ([tile] if has_res else [])
                  + [const(x.shape) for x in params]),
        out_specs=[tile, tile, tile, tile, pair, pair, tile],
        out_shape=[tok, tok, tok, tok, two, two, tok],
        compiler_params=pltpu.CompilerParams(dimension_semantics=("parallel", "parallel"),
                                             vmem_limit_bytes=VMEM_LIMIT),
        name="rwkv_prep",
    )(*args)


def _mix_kernel(yf_ref, yb_ref, r_ref, k_ref, v_ref, a_ref, g_ref, yh_ref, pg_ref, h_ref,
                ka_ref, rk_ref, lg_ref, lb_ref, bd_ref, why_ref, wrw_ref, wo_ref, n1g_ref, n1b_ref,
                wr_ref, br_ref, h_out, aff_out, *, alpha, n_exp):
    d_model = h_ref.shape[2]
    bd = bd_ref[...]
    inv = 1.0 / RW_HEAD
    y = yf_ref[0, 0] + yb_ref[0, 0]
    mu = _head_sum(y, bd) * inv
    dy = y - mu
    var = _head_sum(dy * dy, bd) * inv
    yn = dy * lax.rsqrt(var + GN_EPS) * lg_ref[...] + lb_ref[...]
    k = k_ref[0]
    ka = ka_ref[...]
    k_sum = k * (1.0 + (a_ref[0, 0] - 1.0) * ka) + k * (1.0 + (a_ref[0, 1] - 1.0) * ka)
    bonus = _head_sum(r_ref[0] * k_sum * rk_ref[...], bd) * v_ref[0]
    y_rw = (yn + bonus) * g_ref[0]
    o_hy = _dg(yh_ref[0].astype(BF16), why_ref[...], _NN)
    o_rw = _dg(y_rw.astype(BF16), wrw_ref[...], _NN)
    gates = jax.nn.sigmoid(pg_ref[0])
    merged = gates[:, :d_model] * o_hy + gates[:, d_model:] * o_rw
    mo = _dg(merged.astype(BF16), wo_ref[...], _NN)
    h1 = _layer_norm(alpha * h_ref[0] + mo, n1g_ref[...], n1b_ref[...])
    h_out[0] = h1
    logits = _dg(h1.astype(BF16), wr_ref[...], _NN) + br_ref[...]
    col = lax.broadcasted_iota(jnp.int32, logits.shape, 1)
    logits = jnp.where(col < n_exp, logits, -1e30)
    ex = jnp.exp(logits - jnp.max(logits, axis=1, keepdims=True))
    aff = ex / jnp.sum(ex, axis=1, keepdims=True)
    aff_out[0] = jnp.transpose(aff)[:n_exp, :]


def _mix(y2, r, k, v, a, g, y_hy, p_gate, h, k_a, r_k, lnx_g, lnx_b, w_o_hy, w_o_rw, w_out, ln_g, ln_b,
         w_router, b_router, alpha):
    bsz, seq_len, d = r.shape
    d_model = h.shape[2]
    n_exp = w_router.shape[1]
    ts = MIX_TILE
    tile = lambda w: pl.BlockSpec((1, ts, w), lambda b, i: (b, i, 0))
    const = lambda shape: pl.BlockSpec(shape, lambda b, i: (0,) * len(shape))
    vec = lambda x: x.reshape(1, -1)
    params = [vec(k_a), vec(r_k), vec(lnx_g), vec(lnx_b), _head_ones(d), w_o_hy.astype(BF16),
              w_o_rw.astype(BF16), w_out.astype(BF16), vec(ln_g), vec(ln_b),
              _pad_to(w_router, LANES, 1).astype(BF16), _pad_to(vec(b_router), LANES, 1)]
    return pl.pallas_call(
        functools.partial(_mix_kernel, alpha=alpha, n_exp=n_exp),
        grid=(bsz, seq_len // ts),
        in_specs=[pl.BlockSpec((1, 1, ts, d), lambda b, i: (0, b, i, 0)),
                  pl.BlockSpec((1, 1, ts, d), lambda b, i: (1, b, i, 0)),
                  tile(d), tile(d), tile(d), pl.BlockSpec((1, 2, ts, d), lambda b, i: (b, 0, i, 0)),
                  tile(d), tile(d), tile(2 * d_model), tile(d_model)] + [const(x.shape) for x in params],
        out_specs=[tile(d_model), pl.BlockSpec((1, n_exp, ts), lambda b, i: (b, 0, i))],
        out_shape=[jax.ShapeDtypeStruct((bsz, seq_len, d_model), F32),
                   jax.ShapeDtypeStruct((bsz, n_exp, seq_len), F32)],
        compiler_params=pltpu.CompilerParams(dimension_semantics=("parallel", "parallel"),
                                             vmem_limit_bytes=VMEM_LIMIT),
        name="mix_out",
    )(y2, y2, r, k, v, a, g, y_hy, p_gate, h, *params)


LN_EPS = 1e-5
GN_EPS = 64e-5
HY_ORDER = 2
HY_TARGET = 1e-2
HY_SHORT_DECAY_PCT = 0.3
HY_LONG_DECAY_PCT = 1.5
N_EXPERTS = 16
EC_CAPACITY = 2


def _layer_norm(x, g, b):
    mu = jnp.mean(x, -1, keepdims=True)
    var = jnp.mean(jnp.square(x - mu), -1, keepdims=True)
    return (x - mu) * lax.rsqrt(var + LN_EPS) * g + b


def _hyena_filters(seq_len, w1, b1, w2, b2, w3, freq, d_hy):
    emb = w1.shape[0]
    bands = (emb - 1) // 2
    t = jnp.linspace(0.0, 1.0, seq_len, dtype=F32)[:, None]
    w = 2.0 * math.pi * jnp.arange(seq_len, dtype=F32)[:, None] / seq_len
    f = jnp.linspace(1e-4, bands - 1, bands, dtype=F32)[None, :]
    z = jnp.concatenate([t, jnp.cos(f * w), -jnp.sin(f * w)], axis=-1)
    a = jnp.sin(freq * (z @ w1 + b1))
    for i in range(w2.shape[0]):
        a = jnp.sin(freq * (a @ w2[i] + b2[i]))
    filt = (a @ w3).reshape(seq_len, HY_ORDER, 2, d_hy)
    max_decay = math.log(HY_TARGET) / HY_SHORT_DECAY_PCT
    min_decay = math.log(HY_TARGET) / HY_LONG_DECAY_PCT
    deltas = jnp.linspace(min_decay, max_decay, d_hy, dtype=F32)
    filt = filt * jnp.exp(-t * jnp.abs(deltas))[:, None, None, :]
    filt = filt * lax.rsqrt(jnp.sum(filt * filt, axis=(0, 2), keepdims=True) + 1e-12)
    return filt


def _hyena_branch(p, conv_w, conv_b, filt, bias):
    seq_len = p.shape[1]
    v, x1, x2 = _shortconv(p, conv_w, conv_b)
    z = _long_conv_gated(v, x1, *_filter_spectrum(filt[:, 0], seq_len), bias[0])
    return _long_conv_gated(z, x2, *_filter_spectrum(filt[:, 1], seq_len), bias[1])


def kernel(x, ln0_g, ln0_b, w_in, hy_conv_w, hy_conv_b, hy_ffn_w1, hy_ffn_b1, hy_ffn_w2, hy_ffn_b2,
           hy_ffn_w3, hy_freq, hy_bias, rw_mu, rw_w0, rw_w2, rw_a0, rw_a2, rw_v0, rw_v1, rw_v2, rw_g2,
           rw_k_k, rw_k_a, rw_r_k, rw_lnx_g, rw_lnx_b, w_o_hy, w_o_rw, w_out, ln1_g, ln1_b,
           w_router, b_router, w_gate, w_up, w_down, ln2_g, ln2_b):
    bsz, seq_len, d_model = x.shape
    depth = w_in.shape[0]
    d_hy = hy_bias.shape[-1]
    d_rw = rw_w0.shape[-1]
    p_hy = 3 * d_hy
    p_rw = rw_mu.shape[-1]
    lora = p_rw - 3 * d_rw
    lora_pad = _round_up(lora, LANES)
    widths = (p_hy, 3 * d_rw, lora_pad, 2 * d_model)
    alpha = (2 * depth) ** 0.25
    cap = EC_CAPACITY * seq_len // N_EXPERTS
    tokens = bsz * seq_len
    h = _ln_rows(x.reshape(tokens, d_model), ln0_g, ln0_b).reshape(bsz, seq_len, d_model)
    v_first = None
    for l in range(depth):
        c0, c1, c2 = p_hy, p_hy + 3 * d_rw, p_hy + p_rw
        w = jnp.concatenate([w_in[l][:, :c1], _pad_to(w_in[l][:, c1:c2], lora_pad, 1), w_in[l][:, c2:]], axis=1)
        pa, pb, pl_, pg = [t.reshape(bsz, seq_len, -1) for t in _proj(h.reshape(tokens, d_model), w, widths)]
        filt = _hyena_filters(seq_len, hy_ffn_w1[l], hy_ffn_b1[l], hy_ffn_w2[l], hy_ffn_b2[l],
                              hy_ffn_w3[l], hy_freq[l], d_hy)
        y_hy = _hyena_branch(pa, hy_conv_w[l], hy_conv_b[l], filt, hy_bias[l])
        v_res = None if l == 0 else (rw_v0[l - 1], rw_v1[l - 1], rw_v2[l - 1])
        mu = rw_mu[l]
        r, k, v, kk, lw, a, g = _rwkv_prep(pb, pl_, v_first, v_res, mu[:, :3 * d_rw],
                                           _pad_to(mu[:, 3 * d_rw:], lora_pad, 1), rw_w0[l], rw_w2[l],
                                           rw_a0[l], rw_a2[l], rw_g2[l], rw_k_k[l])
        if l == 0:
            v_first = v
        y2 = wkv7(r, k, v, kk, lw, a, rw_k_a[l])
        h, aff_t = _mix(y2, r, k, v, a, g, y_hy, pg, h, rw_k_a[l], rw_r_k[l], rw_lnx_g[l], rw_lnx_b[l],
                        w_o_hy[l], w_o_rw[l], w_out[l], ln1_g[l], ln1_b[l], w_router[l], b_router[l], alpha)
        idx, gate = _route(aff_t, cap)
        h = _moe_block(h, idx, gate, w_gate[l].astype(BF16), w_up[l].astype(BF16), w_down[l].astype(BF16),
                       ln2_g[l], ln2_b[l], alpha)
    return h
```

```python
import functools
import math

import jax
import jax.numpy as jnp
import numpy as np
from jax import lax
from jax.experimental import pallas as pl
from jax.experimental.pallas import tpu as pltpu

F32 = jnp.float32
BF16 = jnp.bfloat16

RW_HEAD = 64
LANES = 128
WKV_CHUNK = 64
VMEM_LIMIT = 56 * 1024 * 1024


def _split(x):
    hi = x.astype(BF16)
    lo = (x - hi.astype(F32)).astype(BF16)
    return hi, lo


def _dg(a, b, dims):
    return lax.dot_general(a, b, (dims, ((), ())), preferred_element_type=F32)


def _mm(a, b, dims=((1,), (0,)), passes=1):
    if passes == 1:
        return _dg(a.astype(BF16), b.astype(BF16), dims)
    ah, al = _split(a)
    bh, bl = _split(b)
    return _dg(ah, bh, dims) + (_dg(ah, bl, dims) + _dg(al, bh, dims))


_NN = ((1,), (0,))
_NT = ((1,), (1,))
_TN = ((0,), (0,))


def _wkv_operands(r, k, v, kk, lw, a, ka, reverse):
    C = WKV_CHUNK
    row = lax.broadcasted_iota(jnp.int32, (C, C), 0)
    col = lax.broadcasted_iota(jnp.int32, (C, C), 1)
    incl = (col >= row) if reverse else (col <= row)

    lw_hi, lw_lo = _split(lw)
    tri = incl.astype(BF16)
    cin = _dg(tri, lw_hi, _NN) + _dg(tri, lw_lo, _NN)
    tot = jnp.sum(lw, axis=0, keepdims=True)
    half = 0.5 * tot
    e0 = jnp.exp(half)
    e1 = jnp.exp(cin - half)
    e2 = jnp.exp(half - cin)
    ew = jnp.exp(-lw)
    kdir = k * (1.0 + (a - 1.0) * ka)
    rt = r * e1
    bt = kk * (e1 * ew)
    kt = kdir * e2
    at = -(kk * a) * e2
    r0 = rt * e0
    b0 = bt * e0
    kh = kt * e0
    ah = at * e0
    return dict(bt=bt, rt=rt, kt=kt, at=at, b0=b0, r0=r0, kh=kh, ah=ah, e0sq=e0 * e0, v=v)


def _wkv_kernel(rf_ref, kf_ref, vf_ref, nf_ref, rb_ref, kb_ref, vb_ref, nb_ref, lwf_ref, lwb_ref,
                af_ref, ab_ref, ka_ref, yf_ref, yb_ref, h_ref):
    C = WKV_CHUNK

    @pl.when(pl.program_id(1) == 0)
    def _():
        h_ref[...] = jnp.zeros_like(h_ref)

    ka = ka_ref[...]
    n_pairs = ka.shape[1] // LANES
    ops = [_wkv_operands(rf_ref[0], kf_ref[0], vf_ref[0], nf_ref[0], lwf_ref[0, 0], af_ref[0, 0], ka, False),
           _wkv_operands(rb_ref[0], kb_ref[0], vb_ref[0], nb_ref[0], lwb_ref[0, 0], ab_ref[0, 0], ka, True)]
    outs = [yf_ref, yb_ref]

    row2 = lax.broadcasted_iota(jnp.int32, (C, 2 * C), 0)
    col2 = lax.broadcasted_iota(jnp.int32, (C, 2 * C), 1) % C
    earlier2 = [col2 < row2, col2 > row2]
    incl2 = [col2 <= row2, col2 >= row2]
    lane = lax.broadcasted_iota(jnp.int32, (1, LANES), 1)
    first = lane < RW_HEAD
    ri = lax.broadcasted_iota(jnp.int32, (LANES, LANES), 0)
    ci = lax.broadcasted_iota(jnp.int32, (LANES, LANES), 1)
    eye = ri == ci
    same_head = (ri // RW_HEAD) == (ci // RW_HEAD)

    def stack2(x):
        return jnp.concatenate([jnp.where(first, x, 0.0), jnp.where(first, 0.0, x)], axis=0)

    chains = [(d, p) for d in range(2) for p in range(n_pairs)]
    n = range(len(chains))

    def op(name, q):
        d, p = chains[q]
        return ops[d][name][:, p * LANES:(p + 1) * LANES]

    g = [_mm(jnp.concatenate([op("bt", q), op("rt", q)], axis=0),
             jnp.concatenate([stack2(op("kt", q)), stack2(op("at", q))], axis=0), _NT) for q in n]
    hp = [h_ref[q] for q in n]
    bh = [_mm(jnp.concatenate([op("b0", q), op("r0", q)], axis=0), hp[q]) for q in n]
    v2 = [stack2(op("v", q)) for q in n]
    x = [bh[q][:C] + _mm(jnp.where(earlier2[chains[q][0]], g[q][:C, :2 * C], 0.0), v2[q]) for q in n]
    pw = [jnp.where(earlier2[chains[q][0]], g[q][:C, 2 * C:], 0.0) for q in n]
    n_steps = C.bit_length() - 1
    for i in range(n_steps):
        x = [x[q] + _mm(pw[q], stack2(x[q])) for q in n]
        if i + 1 < n_steps:
            pw = [_mm(pw[q], stack2(pw[q])) for q in n]
    for q in n:
        d, p = chains[q]
        a_r = jnp.concatenate([jnp.where(incl2[d], g[q][C:, :2 * C], 0.0),
                               jnp.where(incl2[d], g[q][C:, 2 * C:], 0.0)], axis=1)
        outs[d][0, :, p * LANES:(p + 1) * LANES] = bh[q][C:] + _mm(
            a_r, jnp.concatenate([v2[q], stack2(x[q])], axis=0))
    for q in n:
        dg = jnp.where(eye, jnp.broadcast_to(op("e0sq", q), (LANES, LANES)), 0.0)
        hn = _mm(jnp.concatenate([op("kh", q), op("ah", q), dg], axis=0),
                 jnp.concatenate([op("v", q), x[q], hp[q]], axis=0), _TN)
        h_ref[q] = jnp.where(same_head, hn, 0.0)


def wkv7(r, k, v, kk, lw, a, k_a):
    B, S, D = r.shape
    C = WKV_CHUNK
    nc = S // C
    assert S % C == 0 and D % LANES == 0
    fwd = pl.BlockSpec((1, C, D), lambda b, c: (b, c, 0))
    bwd = pl.BlockSpec((1, C, D), lambda b, c: (b, nc - 1 - c, 0))
    fwd2 = pl.BlockSpec((1, 1, C, D), lambda b, c: (b, 0, c, 0))
    bwd2 = pl.BlockSpec((1, 1, C, D), lambda b, c: (b, 1, nc - 1 - c, 0))
    out = jax.ShapeDtypeStruct((B, S, D), F32)
    return pl.pallas_call(
        _wkv_kernel,
        grid=(B, nc),
        in_specs=[fwd] * 4 + [bwd] * 4 + [fwd2, bwd2, fwd2, bwd2, pl.BlockSpec((1, D), lambda b, c: (0, 0))],
        out_specs=[fwd, bwd],
        out_shape=[out, out],
        scratch_shapes=[pltpu.VMEM((2 * D // LANES, LANES, LANES), F32)],
        compiler_params=pltpu.CompilerParams(dimension_semantics=("parallel", "arbitrary")),
        name="wkv7_chunked",
    )(r, k, v, kk, r, k, v, kk, lw, lw, a, a, k_a.reshape(1, D))


FFT_N2 = 64
FFT_K1_BLOCK = 8


def _round_up(n, m):
    return (n + m - 1) // m * m


def _cat3(m, axis):
    m32 = jnp.asarray(m, F32)
    hi, lo = _split(m32)
    return jnp.concatenate([hi, hi, lo], axis=axis)


def _data3(x):
    hi, lo = _split(x)
    return jnp.concatenate([hi, lo, hi], axis=0)


@functools.lru_cache(maxsize=None)
def _fft_tables(seq_len):
    n = 2 * seq_len
    n2 = FFT_N2
    n1 = n // n2
    h1 = n1 // 2
    k1n = h1 + 1
    k1p = _round_up(k1n, FFT_K1_BLOCK)
    k1 = np.arange(k1n)[:, None]
    th1 = 2 * np.pi * k1 * np.arange(h1)[None, :] / n1
    f1 = np.zeros((2 * k1p, h1))
    f1[:k1n] = np.cos(th1)
    f1[k1p:k1p + k1n] = -np.sin(th1)
    m = np.arange(n2)
    th2 = 2 * np.pi * (m[None, None, :] * m[None, :, None] / n2 + m[None, None, :] * np.arange(k1n)[:, None, None] / n)
    gr, gi = np.cos(th2), -np.sin(th2)
    g = np.zeros((k1p, 2 * n2, 2 * n2))
    g[:k1n] = np.block([[gr, -gi], [gi, gr]])
    gt = np.transpose(g, (0, 2, 1))
    coef = np.full((k1n,), 2.0)
    coef[0] = 1.0
    coef[-1] = 1.0
    th3 = th1.T
    m3 = np.zeros((h1, 2 * k1p))
    m3[:, :k1n] = coef * np.cos(th3) / n
    m3[:, k1p:k1p + k1n] = -coef * np.sin(th3) / n
    kidx = (np.arange(k1n)[:, None] + n1 * np.arange(n2)[None, :])
    return dict(n1=n1, h1=h1, k1n=k1n, k1p=k1p, f1=f1, g=g, gt=gt, m3=m3, kidx=kidx)


FFT_N2_BLOCK = 8


def _rows_to_lanes(ref):
    return jnp.concatenate([ref[0, :, i, :] for i in range(FFT_N2_BLOCK)], axis=1)


def _lanes_to_rows(ref, val):
    ch = ref.shape[3]
    for i in range(FFT_N2_BLOCK):
        ref[0, :, i, :] = val[:, i * ch:(i + 1) * ch]


def _fft_s1_kernel(f_ref, x_ref, o_ref):
    _lanes_to_rows(o_ref, jnp.dot(f_ref[...], _data3(_rows_to_lanes(x_ref)), preferred_element_type=F32))


def _fft_mid_kernel(g_ref, gt_ref, kr_ref, ki_ref, a_ref, o_ref):
    n2 = FFT_N2
    for j in range(FFT_K1_BLOCK):
        xin = jnp.concatenate([a_ref[0, 0, j], a_ref[0, 1, j]], axis=0)
        z = jnp.dot(g_ref[j], _data3(xin), preferred_element_type=F32)
        zr, zi = z[:n2], z[n2:]
        kr, ki = kr_ref[j], ki_ref[j]
        y = jnp.concatenate([zr * kr - zi * ki, zr * ki + zi * kr], axis=0)
        b = jnp.dot(gt_ref[j], _data3(y), preferred_element_type=F32)
        o_ref[0, 0, j] = b[:n2]
        o_ref[0, 1, j] = b[n2:]


def _fft_s3_kernel(m_ref, b_ref, u_ref, x_ref, bias_ref, o_ref):
    y = jnp.dot(m_ref[...], _data3(_rows_to_lanes(b_ref)), preferred_element_type=F32)
    _lanes_to_rows(o_ref, _rows_to_lanes(x_ref) * (y + _rows_to_lanes(u_ref) * bias_ref[...]))


def _long_conv_gated(u, gate, kf_re, kf_im, bias):
    bsz, seq_len, ch = u.shape
    t = _fft_tables(seq_len)
    n2, h1, k1p = FFT_N2, t["h1"], t["k1p"]
    lanes = n2 * ch
    nb = FFT_N2_BLOCK
    lt = nb * ch
    assert seq_len == h1 * n2 and ch % LANES == 0
    params = pltpu.CompilerParams(dimension_semantics=("parallel", "parallel"),
                                  vmem_limit_bytes=VMEM_LIMIT)
    spec4 = pl.BlockSpec((1, 2 * k1p, nb, ch), lambda b, j: (b, 0, j, 0))
    row = pl.BlockSpec((1, h1, nb, ch), lambda b, j: (b, 0, j, 0))
    f1 = _cat3(t["f1"], 1)
    a = pl.pallas_call(
        _fft_s1_kernel,
        grid=(bsz, n2 // nb),
        in_specs=[pl.BlockSpec(f1.shape, lambda b, j: (0, 0)), row],
        out_specs=spec4,
        out_shape=jax.ShapeDtypeStruct((bsz, 2 * k1p, n2, ch), F32),
        compiler_params=params, name="hyena_dft1",
    )(f1, u.reshape(bsz, h1, n2, ch))
    kb = FFT_K1_BLOCK
    g = _cat3(t["g"], 2)
    gt = _cat3(t["gt"], 2)
    mat_spec = pl.BlockSpec((kb,) + g.shape[1:], lambda b, j: (j, 0, 0))
    kf_spec = pl.BlockSpec((kb, n2, ch), lambda b, j: (j, 0, 0))
    blk = pl.BlockSpec((1, 2, kb, n2, ch), lambda b, j: (b, 0, j, 0, 0))
    bm = pl.pallas_call(
        _fft_mid_kernel,
        grid=(bsz, k1p // kb),
        in_specs=[mat_spec, mat_spec, kf_spec, kf_spec, blk],
        out_specs=blk,
        out_shape=jax.ShapeDtypeStruct((bsz, 2, k1p, n2, ch), F32),
        compiler_params=params, name="hyena_dft2",
    )(g, gt, kf_re, kf_im, a.reshape(bsz, 2, k1p, n2, ch))
    m3 = _cat3(t["m3"], 1)
    out = pl.pallas_call(
        _fft_s3_kernel,
        grid=(bsz, n2 // nb),
        in_specs=[pl.BlockSpec(m3.shape, lambda b, j: (0, 0)), spec4,
                  row, row, pl.BlockSpec((1, lt), lambda b, j: (0, j))],
        out_specs=row,
        out_shape=jax.ShapeDtypeStruct((bsz, h1, n2, ch), F32),
        compiler_params=params, name="hyena_dft3",
    )(m3, bm.reshape(bsz, 2 * k1p, n2, ch), u.reshape(bsz, h1, n2, ch), gate.reshape(bsz, h1, n2, ch),
      jnp.tile(bias, n2).reshape(1, lanes))
    return out.reshape(bsz, seq_len, ch)


def _filter_spectrum(filt_o, seq_len):
    t = _fft_tables(seq_len)
    k_full = jnp.concatenate([filt_o[:, 0], filt_o[::-1, 1]], axis=0)
    kf = jnp.fft.fft(k_full, axis=0)[t["kidx"].reshape(-1)]
    kf = kf.reshape(t["k1n"], FFT_N2, -1)
    pad = ((0, t["k1p"] - t["k1n"]), (0, 0), (0, 0))
    return jnp.pad(jnp.real(kf), pad), jnp.pad(jnp.imag(kf), pad)


ROUTE_COLS = 16
ROUTE_TOKEN_CHUNK = 512


def _route_kernel(aff_ref, o_ref, key_ref, val_ref, *, cap):
    a = aff_ref[0]
    n_exp, seq_len = a.shape
    bits = pltpu.bitcast(a, jnp.int32)

    def bisect(i, t):
        cand = t | jnp.left_shift(1, 30 - i)
        cnt = jnp.sum((bits >= cand).astype(jnp.int32), axis=1, keepdims=True)
        return jnp.where(cnt >= cap, cand, t)

    thr = lax.fori_loop(0, 31, bisect, jnp.zeros((n_exp, 1), jnp.int32))
    gt = bits > thr
    eq = bits == thr

    ri = lax.broadcasted_iota(jnp.int32, (LANES, LANES), 0)
    ci = lax.broadcasted_iota(jnp.int32, (LANES, LANES), 1)
    upper = (ri < ci).astype(BF16)

    def prefix(mask):
        m = mask.astype(F32)
        run = jnp.zeros((n_exp, 1), F32)
        parts = []
        for j in range(seq_len // LANES):
            tile = m[:, j * LANES:(j + 1) * LANES]
            parts.append(run + jnp.dot(tile.astype(BF16), upper, preferred_element_type=F32))
            run = run + jnp.sum(tile, axis=1, keepdims=True)
        return jnp.concatenate(parts, axis=1), run

    eq_before, _ = prefix(eq)
    n_gt = jnp.sum(gt.astype(F32), axis=1, keepdims=True)
    sel = jnp.logical_or(gt, jnp.logical_and(eq, eq_before < cap - n_gt))
    pos, _ = prefix(sel)
    key_ref[...] = jnp.where(sel, pos, -1.0)
    val_ref[...] = a

    tok = lax.broadcasted_iota(jnp.int32, (1, seq_len), 1)
    t_hi = (tok // 64).astype(F32)
    t_lo = (tok % 64).astype(F32)
    slot = lax.broadcasted_iota(jnp.int32, (cap, ROUTE_TOKEN_CHUNK), 0).astype(F32)
    zeros = jnp.zeros((ROUTE_COLS - 5, seq_len), F32)

    def compact(e, carry):
        key = key_ref[pl.ds(e, 1), :]
        g = val_ref[pl.ds(e, 1), :]
        g_hi = g.astype(BF16).astype(F32)
        g_mid = (g - g_hi).astype(BF16).astype(F32)
        g_lo = g - g_hi - g_mid
        rows = jnp.concatenate([t_hi, t_lo, g_hi, g_mid, g_lo, zeros], axis=0).astype(BF16)
        acc = jnp.zeros((cap, ROUTE_COLS), F32)
        for j in range(seq_len // ROUTE_TOKEN_CHUNK):
            sl = slice(j * ROUTE_TOKEN_CHUNK, (j + 1) * ROUTE_TOKEN_CHUNK)
            onehot = jnp.where(key[:, sl] == slot, 1.0, 0.0).astype(BF16)
            acc = acc + _dg(onehot, rows[:, sl], _NT)
        o_ref[0, pl.ds(e, 1)] = acc[None]
        return carry

    lax.fori_loop(0, n_exp, compact, 0)


def _route(aff_t, cap):
    bsz, n_exp, seq_len = aff_t.shape
    assert seq_len % ROUTE_TOKEN_CHUNK == 0 and seq_len <= 64 * 256
    out = pl.pallas_call(
        functools.partial(_route_kernel, cap=cap),
        grid=(bsz,),
        in_specs=[pl.BlockSpec((1, n_exp, seq_len), lambda b: (b, 0, 0))],
        out_specs=pl.BlockSpec((1, n_exp, cap, ROUTE_COLS), lambda b: (b, 0, 0, 0)),
        out_shape=jax.ShapeDtypeStruct((bsz, n_exp, cap, ROUTE_COLS), F32),
        scratch_shapes=[pltpu.VMEM((n_exp, seq_len), F32), pltpu.VMEM((n_exp, seq_len), F32)],
        compiler_params=pltpu.CompilerParams(dimension_semantics=("parallel",),
                                             vmem_limit_bytes=VMEM_LIMIT),
        name="ec_route",
    )(aff_t)
    idx = (out[..., 0] * 64.0 + out[..., 1]).astype(jnp.int32)
    gate = out[..., 2] + out[..., 3] + out[..., 4]
    return idx, gate


MOE_FF_TILE = 1024
MOE_GROUP = 16
LN_ROWS = 256


def _moe_kernel(idx_ref, gate_ref, h_ref, wg_ref, wu_ref, wd_ref, lg_ref, lb_ref, o_ref,
                xb_ref, y_ref, *, alpha):
    e = pl.program_id(1)
    f = pl.program_id(2)
    last_e = pl.num_programs(1) - 1
    last_f = pl.num_programs(2) - 1
    cap = xb_ref.shape[0]
    seq_len = h_ref.shape[1]

    @pl.when(jnp.logical_and(e == 0, f == 0))
    def _():
        o_ref[...] = jnp.zeros_like(o_ref)

    @pl.when(f == 0)
    def _():
        def gather(i, carry):
            base = pl.multiple_of(i * MOE_GROUP, MOE_GROUP)
            rows = [h_ref[0, pl.ds(idx_ref[0, 0, base + j], 1), :] for j in range(MOE_GROUP)]
            xb_ref[pl.ds(base, MOE_GROUP), :] = jnp.concatenate(rows, axis=0).astype(BF16)
            return carry

        lax.fori_loop(0, cap // MOE_GROUP, gather, 0)

    xb = xb_ref[...]
    a = jnp.dot(xb, wg_ref[0], preferred_element_type=F32)
    u = jnp.dot(xb, wu_ref[0], preferred_element_type=F32)
    act = (a * jax.nn.sigmoid(a) * u).astype(BF16)
    part = jnp.dot(act, wd_ref[0], preferred_element_type=F32)

    @pl.when(f == 0)
    def _():
        y_ref[...] = part

    @pl.when(f > 0)
    def _():
        y_ref[...] += part

    @pl.when(f == last_f)
    def _():
        def scatter(i, carry):
            base = pl.multiple_of(i * 8, 8)
            y8 = y_ref[pl.ds(base, 8), :]
            toks = [idx_ref[0, 0, base + j] for j in range(8)]
            rows = [o_ref[0, pl.ds(toks[j], 1), :] + gate_ref[0, 0, base + j] * y8[j:j + 1, :]
                    for j in range(8)]
            for j in range(8):
                o_ref[0, pl.ds(toks[j], 1), :] = rows[j]
            return carry

        lax.fori_loop(0, cap // 8, scatter, 0)

    @pl.when(jnp.logical_and(e == last_e, f == last_f))
    def _():
        def norm(i, carry):
            rows = pl.ds(pl.multiple_of(i * LN_ROWS, LN_ROWS), LN_ROWS)
            z = alpha * h_ref[0, rows, :] + o_ref[0, rows, :]
            o_ref[0, rows, :] = _layer_norm(z, lg_ref[...], lb_ref[...])
            return carry

        lax.fori_loop(0, seq_len // LN_ROWS, norm, 0)


def _moe_block(h, idx, gate, wg, wu, wd, ln_g, ln_b, alpha):
    bsz, seq_len, d = h.shape
    n_exp, _, ff = wg.shape
    cap = idx.shape[-1]
    ft = min(MOE_FF_TILE, ff)
    assert ff % ft == 0 and seq_len % LN_ROWS == 0
    smem = pl.BlockSpec((1, 1, cap), lambda b, e, f: (b * n_exp + e, 0, 0), memory_space=pltpu.SMEM)
    whole = pl.BlockSpec((1, seq_len, d), lambda b, e, f: (b, 0, 0), pipeline_mode=pl.Buffered(1))
    vec = pl.BlockSpec((1, d), lambda b, e, f: (0, 0))
    return pl.pallas_call(
        functools.partial(_moe_kernel, alpha=alpha),
        grid=(bsz, n_exp, ff // ft),
        in_specs=[smem, smem, whole,
                  pl.BlockSpec((1, d, ft), lambda b, e, f: (e, 0, f)),
                  pl.BlockSpec((1, d, ft), lambda b, e, f: (e, 0, f)),
                  pl.BlockSpec((1, ft, d), lambda b, e, f: (e, f, 0)),
                  vec, vec],
        out_specs=whole,
        out_shape=jax.ShapeDtypeStruct((bsz, seq_len, d), F32),
        scratch_shapes=[pltpu.VMEM((cap, d), BF16), pltpu.VMEM((cap, d), F32)],
        compiler_params=pltpu.CompilerParams(
            dimension_semantics=("arbitrary", "arbitrary", "arbitrary"), vmem_limit_bytes=VMEM_LIMIT),
        name="ec_moe",
    )(idx.reshape(bsz * n_exp, 1, cap), gate.reshape(bsz * n_exp, 1, cap), h, wg, wu, wd,
      ln_g.reshape(1, d), ln_b.reshape(1, d))


ROW_TILE = 512
MIX_TILE = 256


def _proj_kernel(x_ref, w_ref, *o_refs):
    xb = x_ref[...].astype(BF16)
    off = 0
    for o_ref in o_refs:
        n = o_ref.shape[1]
        o_ref[...] = jnp.dot(xb, w_ref[:, off:off + n], preferred_element_type=F32)
        off += n


def _proj(x, w, widths):
    M, K = x.shape
    tm = ROW_TILE
    assert M % tm == 0 and sum(widths) == w.shape[1] and all(n % LANES == 0 for n in widths)
    return pl.pallas_call(
        _proj_kernel,
        grid=(M // tm,),
        in_specs=[pl.BlockSpec((tm, K), lambda i: (i, 0)),
                  pl.BlockSpec(w.shape, lambda i: (0, 0), pipeline_mode=pl.Buffered(1))],
        out_specs=[pl.BlockSpec((tm, n), lambda i: (i, 0)) for n in widths],
        out_shape=[jax.ShapeDtypeStruct((M, n), F32) for n in widths],
        compiler_params=pltpu.CompilerParams(dimension_semantics=("parallel",),
                                             vmem_limit_bytes=VMEM_LIMIT),
        name="proj",
    )(x, w.astype(BF16))


def _pad_to(w, n, axis):
    pad = [(0, 0)] * w.ndim
    pad[axis] = (0, n - w.shape[axis])
    return jnp.pad(w, pad)


def _ln_kernel(x_ref, g_ref, b_ref, o_ref):
    o_ref[...] = _layer_norm(x_ref[...], g_ref[...], b_ref[...])


def _ln_rows(x, g, b):
    M, D = x.shape
    row = pl.BlockSpec((ROW_TILE, D), lambda i: (i, 0))
    vec = pl.BlockSpec((1, D), lambda i: (0, 0))
    return pl.pallas_call(
        _ln_kernel, grid=(M // ROW_TILE,), in_specs=[row, vec, vec], out_specs=row,
        out_shape=jax.ShapeDtypeStruct((M, D), F32),
        compiler_params=pltpu.CompilerParams(dimension_semantics=("parallel",)),
        name="ln_in",
    )(x, g.reshape(1, D), b.reshape(1, D))


def _halo_specs(ts, width, seq_len):
    per = ts // 8
    last = seq_len // 8 - 1
    cur = pl.BlockSpec((1, ts, width), lambda b, i: (b, i, 0))
    prev = pl.BlockSpec((1, 8, width), lambda b, i: (b, jnp.maximum(i * per - 1, 0), 0))
    nxt = pl.BlockSpec((1, 8, width), lambda b, i: (b, jnp.minimum((i + 1) * per, last), 0))
    return [cur, prev, nxt]


def _neighbours(x, prev_blk, next_blk):
    ts = x.shape[0]
    i = pl.program_id(1)
    rid = lax.broadcasted_iota(jnp.int32, x.shape, 0)
    prev_row = jnp.where(i > 0, prev_blk[7:8, :], 0.0)
    next_row = jnp.where(i < pl.num_programs(1) - 1, next_blk[0:1, :], 0.0)
    zp = jnp.where(rid == 0, prev_row, pltpu.roll(x, 1, 0))
    zn = jnp.where(rid == ts - 1, next_row, pltpu.roll(x, ts - 1, 0))
    return zp, zn


def _shortconv_kernel(p_ref, pp_ref, pn_ref, w_ref, b_ref, v_ref, x1_ref, x2_ref):
    p = p_ref[0]
    zp, zn = _neighbours(p, pp_ref[0], pn_ref[0])
    q = w_ref[0:1, :] * zp + w_ref[1:2, :] * p + w_ref[2:3, :] * zn + b_ref[...]
    d = v_ref.shape[2]
    v_ref[0] = q[:, :d]
    x1_ref[0] = q[:, d:2 * d]
    x2_ref[0] = q[:, 2 * d:]


def _shortconv(p, conv_w, conv_b):
    bsz, seq_len, width = p.shape
    d = width // 3
    ts = ROW_TILE
    out = pl.BlockSpec((1, ts, d), lambda b, i: (b, i, 0))
    return pl.pallas_call(
        _shortconv_kernel,
        grid=(bsz, seq_len // ts),
        in_specs=_halo_specs(ts, width, seq_len) + [pl.BlockSpec((3, width), lambda b, i: (0, 0)),
                                                     pl.BlockSpec((1, width), lambda b, i: (0, 0))],
        out_specs=[out, out, out],
        out_shape=[jax.ShapeDtypeStruct((bsz, seq_len, d), F32)] * 3,
        compiler_params=pltpu.CompilerParams(dimension_semantics=("parallel", "parallel")),
        name="hyena_shortconv",
    )(p, p, p, conv_w, conv_b.reshape(1, width))


def _head_sum(x, bd):
    hi, lo = _split(x)
    return _dg(hi, bd, _NN) + _dg(lo, bd, _NN)


def _softplus(z):
    return jnp.maximum(z, 0.0) + jnp.log(1.0 + jnp.exp(-jnp.abs(z)))


def _rwkv_prep_kernel(*refs, has_res):
    (p_ref, pp_ref, pn_ref, q_ref, qp_ref, qn_ref) = refs[:6]
    refs = refs[6:]
    if has_res:
        vf_ref, refs = refs[0], refs[1:]
    (mu_ref, mul_ref, w2_ref, w0_ref, a2_ref, a0_ref, g2_ref, kk_ref, v0_ref, v1_ref, v2_ref, bd_ref,
     r_out, k_out, v_out, kn_out, lw_out, a_out, g_out) = refs
    d = r_out.shape[2]
    p = p_ref[0]
    zp, zn = _neighbours(p, pp_ref[0], pn_ref[0])
    p = p + mu_ref[0:1, :] * (zp - p) + mu_ref[1:2, :] * (zn - p)
    q = q_ref[0]
    zp, zn = _neighbours(q, qp_ref[0], qn_ref[0])
    q = q + mul_ref[0:1, :] * (zp - q) + mul_ref[1:2, :] * (zn - q)
    r, k, v = p[:, :d], p[:, d:2 * d], p[:, 2 * d:]
    if has_res:
        mix = _dg(_dg(v.astype(BF16), v1_ref[...], _NN).astype(BF16), v2_ref[...], _NN)
        v = v + (vf_ref[0] - v) * jax.nn.sigmoid(v0_ref[...] + mix)
    wd, ad, gd = q[:, :LANES], q[:, LANES:2 * LANES], q[:, 2 * LANES:]
    wpre = w0_ref[...] + _dg(jnp.tanh(wd).astype(BF16), w2_ref[...], _NN)
    lw = -jnp.exp(-_softplus(-wpre) - 0.5)
    a = jax.nn.sigmoid(a0_ref[...] + _dg(ad.astype(BF16), a2_ref[...], _NN))
    g = _dg(jax.nn.sigmoid(gd).astype(BF16), g2_ref[...], _NN)
    kq = k * kk_ref[...]
    norm = jnp.sqrt(_head_sum(kq * kq, bd_ref[...]))
    r_out[0] = r
    k_out[0] = k
    v_out[0] = v
    kn_out[0] = kq / jnp.maximum(norm, 1e-12)
    lw_out[0, 0] = lw[:, :d]
    lw_out[0, 1] = lw[:, d:]
    a_out[0, 0] = a[:, :d]
    a_out[0, 1] = a[:, d:]
    g_out[0] = g


def _block_diag2(m):
    z = jnp.zeros_like(m[0])
    return jnp.concatenate([jnp.concatenate([m[0], z], axis=1), jnp.concatenate([z, m[1]], axis=1)], axis=0)


def _head_ones(d):
    i = np.arange(d) // RW_HEAD
    return jnp.asarray(i[:, None] == i[None, :], BF16)


def _rwkv_prep(p_rkv, p_lora, v_first, v_res, mu_rkv, mu_lora, w0, w2, a0, a2, g2, k_k):
    bsz, seq_len, w3 = p_rkv.shape
    d = w3 // 3
    wl = p_lora.shape[2]
    ts = ROW_TILE
    assert 2 * w2.shape[1] == LANES and 2 * a2.shape[1] == LANES
    has_res = v_res is not None
    const = lambda shape: pl.BlockSpec(shape, lambda b, i: (0,) * len(shape))
    tile = pl.BlockSpec((1, ts, d), lambda b, i: (b, i, 0))
    pair = pl.BlockSpec((1, 2, ts, d), lambda b, i: (b, 0, i, 0))
    if has_res:
        v0, v1, v2 = v_res
        rank = _round_up(v1.shape[1], LANES)
        v0, v1, v2 = v0.reshape(1, d), _pad_to(v1, rank, 1).astype(BF16), _pad_to(v2, rank, 0).astype(BF16)
    else:
        v0, v1, v2 = jnp.zeros((1, d), F32), jnp.zeros((d, LANES), BF16), jnp.zeros((LANES, d), BF16)
    params = [mu_rkv, mu_lora, _block_diag2(w2).astype(BF16), w0.reshape(1, 2 * d),
              _block_diag2(a2).astype(BF16), a0.reshape(1, 2 * d),
              _pad_to(g2, wl - 2 * LANES, 0).astype(BF16), k_k.reshape(1, d), v0, v1, v2, _head_ones(d)]
    args = [p_rkv, p_rkv, p_rkv, p_lora, p_lora, p_lora] + ([v_first] if has_res else []) + params
    tok = jax.ShapeDtypeStruct((bsz, seq_len, d), F32)
    two = jax.ShapeDtypeStruct((bsz, 2, seq_len, d), F32)
    return pl.pallas_call(
        functools.partial(_rwkv_prep_kernel, has_res=has_res),
        grid=(bsz, seq_len // ts),
        in_specs=(_halo_specs(ts, w3, seq_len) + _halo_specs(ts, wl, seq_len) + ([tile] if has_res else [])
                  + [const(x.shape) for x in params]),
        out_specs=[tile, tile, tile, tile, pair, pair, tile],
        out_shape=[tok, tok, tok, tok, two, two, tok],
        compiler_params=pltpu.CompilerParams(dimension_semantics=("parallel", "parallel"),
                                             vmem_limit_bytes=VMEM_LIMIT),
        name="rwkv_prep",
    )(*args)


def _mix_kernel(yf_ref, yb_ref, r_ref, k_ref, v_ref, a_ref, g_ref, yh_ref, pg_ref, h_ref,
                ka_ref, rk_ref, lg_ref, lb_ref, bd_ref, why_ref, wrw_ref, wo_ref, n1g_ref, n1b_ref,
                wr_ref, br_ref, h_out, aff_out, *, alpha, n_exp):
    d_model = h_ref.shape[2]
    bd = bd_ref[...]
    inv = 1.0 / RW_HEAD
    y = yf_ref[0] + yb_ref[0]
    mu = _head_sum(y, bd) * inv
    dy = y - mu
    var = _head_sum(dy * dy, bd) * inv
    yn = dy * lax.rsqrt(var + GN_EPS) * lg_ref[...] + lb_ref[...]
    k = k_ref[0]
    ka = ka_ref[...]
    k_sum = k * (1.0 + (a_ref[0, 0] - 1.0) * ka) + k * (1.0 + (a_ref[0, 1] - 1.0) * ka)
    bonus = _head_sum(r_ref[0] * k_sum * rk_ref[...], bd) * v_ref[0]
    y_rw = (yn + bonus) * g_ref[0]
    o_hy = _dg(yh_ref[0].astype(BF16), why_ref[...], _NN)
    o_rw = _dg(y_rw.astype(BF16), wrw_ref[...], _NN)
    gates = jax.nn.sigmoid(pg_ref[0])
    merged = gates[:, :d_model] * o_hy + gates[:, d_model:] * o_rw
    mo = _dg(merged.astype(BF16), wo_ref[...], _NN)
    h1 = _layer_norm(alpha * h_ref[0] + mo, n1g_ref[...], n1b_ref[...])
    h_out[0] = h1
    logits = _dg(h1.astype(BF16), wr_ref[...], _NN) + br_ref[...]
    col = lax.broadcasted_iota(jnp.int32, logits.shape, 1)
    logits = jnp.where(col < n_exp, logits, -1e30)
    ex = jnp.exp(logits - jnp.max(logits, axis=1, keepdims=True))
    aff = ex / jnp.sum(ex, axis=1, keepdims=True)
    aff_out[0] = jnp.transpose(aff)[:n_exp, :]


def _mix(y_f, y_b, r, k, v, a, g, y_hy, p_gate, h, k_a, r_k, lnx_g, lnx_b, w_o_hy, w_o_rw, w_out, ln_g, ln_b,
         w_router, b_router, alpha):
    bsz, seq_len, d = r.shape
    d_model = h.shape[2]
    n_exp = w_router.shape[1]
    ts = MIX_TILE
    tile = lambda w: pl.BlockSpec((1, ts, w), lambda b, i: (b, i, 0))
    const = lambda shape: pl.BlockSpec(shape, lambda b, i: (0,) * len(shape))
    vec = lambda x: x.reshape(1, -1)
    params = [vec(k_a), vec(r_k), vec(lnx_g), vec(lnx_b), _head_ones(d), w_o_hy.astype(BF16),
              w_o_rw.astype(BF16), w_out.astype(BF16), vec(ln_g), vec(ln_b),
              _pad_to(w_router, LANES, 1).astype(BF16), _pad_to(vec(b_router), LANES, 1)]
    return pl.pallas_call(
        functools.partial(_mix_kernel, alpha=alpha, n_exp=n_exp),
        grid=(bsz, seq_len // ts),
        in_specs=[tile(d), tile(d),
                  tile(d), tile(d), tile(d), pl.BlockSpec((1, 2, ts, d), lambda b, i: (b, 0, i, 0)),
                  tile(d), tile(d), tile(2 * d_model), tile(d_model)] + [const(x.shape) for x in params],
        out_specs=[tile(d_model), pl.BlockSpec((1, n_exp, ts), lambda b, i: (b, 0, i))],
        out_shape=[jax.ShapeDtypeStruct((bsz, seq_len, d_model), F32),
                   jax.ShapeDtypeStruct((bsz, n_exp, seq_len), F32)],
        compiler_params=pltpu.CompilerParams(dimension_semantics=("parallel", "parallel"),
                                             vmem_limit_bytes=VMEM_LIMIT),
        name="mix_out",
    )(y_f, y_b, r, k, v, a, g, y_hy, p_gate, h, *params)


LN_EPS = 1e-5
GN_EPS = 64e-5
HY_ORDER = 2
HY_TARGET = 1e-2
HY_SHORT_DECAY_PCT = 0.3
HY_LONG_DECAY_PCT = 1.5
N_EXPERTS = 16
EC_CAPACITY = 2


def _layer_norm(x, g, b):
    mu = jnp.mean(x, -1, keepdims=True)
    var = jnp.mean(jnp.square(x - mu), -1, keepdims=True)
    return (x - mu) * lax.rsqrt(var + LN_EPS) * g + b


def _hyena_filters(seq_len, w1, b1, w2, b2, w3, freq, d_hy):
    emb = w1.shape[0]
    bands = (emb - 1) // 2
    t = jnp.linspace(0.0, 1.0, seq_len, dtype=F32)[:, None]
    w = 2.0 * math.pi * jnp.arange(seq_len, dtype=F32)[:, None] / seq_len
    f = jnp.linspace(1e-4, bands - 1, bands, dtype=F32)[None, :]
    z = jnp.concatenate([t, jnp.cos(f * w), -jnp.sin(f * w)], axis=-1)
    a = jnp.sin(freq * (z @ w1 + b1))
    for i in range(w2.shape[0]):
        a = jnp.sin(freq * (a @ w2[i] + b2[i]))
    filt = (a @ w3).reshape(seq_len, HY_ORDER, 2, d_hy)
    max_decay = math.log(HY_TARGET) / HY_SHORT_DECAY_PCT
    min_decay = math.log(HY_TARGET) / HY_LONG_DECAY_PCT
    deltas = jnp.linspace(min_decay, max_decay, d_hy, dtype=F32)
    filt = filt * jnp.exp(-t * jnp.abs(deltas))[:, None, None, :]
    filt = filt * lax.rsqrt(jnp.sum(filt * filt, axis=(0, 2), keepdims=True) + 1e-12)
    return filt


def _hyena_branch(p, conv_w, conv_b, filt, bias):
    seq_len = p.shape[1]
    v, x1, x2 = _shortconv(p, conv_w, conv_b)
    z = _long_conv_gated(v, x1, *_filter_spectrum(filt[:, 0], seq_len), bias[0])
    return _long_conv_gated(z, x2, *_filter_spectrum(filt[:, 1], seq_len), bias[1])


def kernel(x, ln0_g, ln0_b, w_in, hy_conv_w, hy_conv_b, hy_ffn_w1, hy_ffn_b1, hy_ffn_w2, hy_ffn_b2,
           hy_ffn_w3, hy_freq, hy_bias, rw_mu, rw_w0, rw_w2, rw_a0, rw_a2, rw_v0, rw_v1, rw_v2, rw_g2,
           rw_k_k, rw_k_a, rw_r_k, rw_lnx_g, rw_lnx_b, w_o_hy, w_o_rw, w_out, ln1_g, ln1_b,
           w_router, b_router, w_gate, w_up, w_down, ln2_g, ln2_b):
    bsz, seq_len, d_model = x.shape
    depth = w_in.shape[0]
    d_hy = hy_bias.shape[-1]
    d_rw = rw_w0.shape[-1]
    p_hy = 3 * d_hy
    p_rw = rw_mu.shape[-1]
    lora = p_rw - 3 * d_rw
    lora_pad = _round_up(lora, LANES)
    widths = (p_hy, 3 * d_rw, lora_pad, 2 * d_model)
    alpha = (2 * depth) ** 0.25
    cap = EC_CAPACITY * seq_len // N_EXPERTS
    tokens = bsz * seq_len
    h = _ln_rows(x.reshape(tokens, d_model), ln0_g, ln0_b).reshape(bsz, seq_len, d_model)
    v_first = None
    for l in range(depth):
        c0, c1, c2 = p_hy, p_hy + 3 * d_rw, p_hy + p_rw
        w = jnp.concatenate([w_in[l][:, :c1], _pad_to(w_in[l][:, c1:c2], lora_pad, 1), w_in[l][:, c2:]], axis=1)
        pa, pb, pl_, pg = [t.reshape(bsz, seq_len, -1) for t in _proj(h.reshape(tokens, d_model), w, widths)]
        filt = _hyena_filters(seq_len, hy_ffn_w1[l], hy_ffn_b1[l], hy_ffn_w2[l], hy_ffn_b2[l],
                              hy_ffn_w3[l], hy_freq[l], d_hy)
        y_hy = _hyena_branch(pa, hy_conv_w[l], hy_conv_b[l], filt, hy_bias[l])
        v_res = None if l == 0 else (rw_v0[l - 1], rw_v1[l - 1], rw_v2[l - 1])
        mu = rw_mu[l]
        r, k, v, kk, lw, a, g = _rwkv_prep(pb, pl_, v_first, v_res, mu[:, :3 * d_rw],
                                           _pad_to(mu[:, 3 * d_rw:], lora_pad, 1), rw_w0[l], rw_w2[l],
                                           rw_a0[l], rw_a2[l], rw_g2[l], rw_k_k[l])
        if l == 0:
            v_first = v
        y_f, y_b = wkv7(r, k, v, kk, lw, a, rw_k_a[l])
        h, aff_t = _mix(y_f, y_b, r, k, v, a, g, y_hy, pg, h, rw_k_a[l], rw_r_k[l], rw_lnx_g[l], rw_lnx_b[l],
                        w_o_hy[l], w_o_rw[l], w_out[l], ln1_g[l], ln1_b[l], w_router[l], b_router[l], alpha)
        idx, gate = _route(aff_t, cap)
        h = _moe_block(h, idx, gate, w_gate[l].astype(BF16), w_up[l].astype(BF16), w_down[l].astype(BF16),
                       ln2_g[l], ln2_b[l], alpha)
    return h
```

```python
import functools
import math

import jax
import jax.numpy as jnp
import numpy as np
from jax import lax
from jax.experimental import pallas as pl
from jax.experimental.pallas import tpu as pltpu

F32 = jnp.float32
BF16 = jnp.bfloat16

RW_HEAD = 64
LANES = 128
WKV_CHUNK = 64
WKV_CHUNKS_PER_STEP = 4
VMEM_LIMIT = 56 * 1024 * 1024


def _split(x):
    hi = x.astype(BF16)
    lo = (x - hi.astype(F32)).astype(BF16)
    return hi, lo


def _dg(a, b, dims):
    return lax.dot_general(a, b, (dims, ((), ())), preferred_element_type=F32)


def _mm(a, b, dims=((1,), (0,)), passes=1):
    if passes == 1:
        return _dg(a.astype(BF16), b.astype(BF16), dims)
    ah, al = _split(a)
    bh, bl = _split(b)
    return _dg(ah, bh, dims) + (_dg(ah, bl, dims) + _dg(al, bh, dims))


_NN = ((1,), (0,))
_NT = ((1,), (1,))
_TN = ((0,), (0,))


def _wkv_operands(r, k, v, kk, lw, a, ka, reverse):
    C = WKV_CHUNK
    row = lax.broadcasted_iota(jnp.int32, (C, C), 0)
    col = lax.broadcasted_iota(jnp.int32, (C, C), 1)
    incl = (col >= row) if reverse else (col <= row)

    lw_hi, lw_lo = _split(lw)
    tri = incl.astype(BF16)
    cin = _dg(tri, lw_hi, _NN) + _dg(tri, lw_lo, _NN)
    tot = jnp.sum(lw, axis=0, keepdims=True)
    half = 0.5 * tot
    e0 = jnp.exp(half)
    e1 = jnp.exp(cin - half)
    e2 = jnp.exp(half - cin)
    ew = jnp.exp(-lw)
    kdir = k * (1.0 + (a - 1.0) * ka)
    rt = r * e1
    bt = kk * (e1 * ew)
    kt = kdir * e2
    at = -(kk * a) * e2
    r0 = rt * e0
    b0 = bt * e0
    kh = kt * e0
    ah = at * e0
    return dict(bt=bt, rt=rt, kt=kt, at=at, b0=b0, r0=r0, kh=kh, ah=ah, e0sq=e0 * e0, v=v)


def _wkv_kernel(rf_ref, kf_ref, vf_ref, nf_ref, rb_ref, kb_ref, vb_ref, nb_ref, lwf_ref, lwb_ref,
                af_ref, ab_ref, ka_ref, yf_ref, yb_ref, h_ref):
    C = WKV_CHUNK

    @pl.when(pl.program_id(1) == 0)
    def _():
        h_ref[...] = jnp.zeros_like(h_ref)

    ka = ka_ref[...]
    n_pairs = ka.shape[1] // LANES
    outs = [yf_ref, yb_ref]

    row2 = lax.broadcasted_iota(jnp.int32, (C, 2 * C), 0)
    col2 = lax.broadcasted_iota(jnp.int32, (C, 2 * C), 1) % C
    earlier2 = [col2 < row2, col2 > row2]
    incl2 = [col2 <= row2, col2 >= row2]
    lane = lax.broadcasted_iota(jnp.int32, (1, LANES), 1)
    first = lane < RW_HEAD
    ri = lax.broadcasted_iota(jnp.int32, (LANES, LANES), 0)
    ci = lax.broadcasted_iota(jnp.int32, (LANES, LANES), 1)
    eye = ri == ci
    same_head = (ri // RW_HEAD) == (ci // RW_HEAD)

    def stack2(x):
        return jnp.concatenate([jnp.where(first, x, 0.0), jnp.where(first, 0.0, x)], axis=0)

    chains = [(d, p) for d in range(2) for p in range(n_pairs)]
    n = range(len(chains))
    hp = [h_ref[q] for q in n]
    n_sub = rf_ref.shape[1] // C
    for s in range(n_sub):
        fs = slice(s * C, (s + 1) * C)
        bs = slice((n_sub - 1 - s) * C, (n_sub - s) * C)
        rows = [fs, bs]
        ops = [_wkv_operands(rf_ref[0, fs, :], kf_ref[0, fs, :], vf_ref[0, fs, :], nf_ref[0, fs, :],
                             lwf_ref[0, 0, fs, :], af_ref[0, 0, fs, :], ka, False),
               _wkv_operands(rb_ref[0, bs, :], kb_ref[0, bs, :], vb_ref[0, bs, :], nb_ref[0, bs, :],
                             lwb_ref[0, 0, bs, :], ab_ref[0, 0, bs, :], ka, True)]

        def op(name, q):
            d, p = chains[q]
            return ops[d][name][:, p * LANES:(p + 1) * LANES]

        g = [_mm(jnp.concatenate([op("bt", q), op("rt", q)], axis=0),
                 jnp.concatenate([stack2(op("kt", q)), stack2(op("at", q))], axis=0), _NT) for q in n]
        bh = [_mm(jnp.concatenate([op("b0", q), op("r0", q)], axis=0), hp[q]) for q in n]
        v2 = [stack2(op("v", q)) for q in n]
        x = [bh[q][:C] + _mm(jnp.where(earlier2[chains[q][0]], g[q][:C, :2 * C], 0.0), v2[q]) for q in n]
        pw = [jnp.where(earlier2[chains[q][0]], g[q][:C, 2 * C:], 0.0) for q in n]
        n_steps = C.bit_length() - 1
        for i in range(n_steps):
            x = [x[q] + _mm(pw[q], stack2(x[q])) for q in n]
            if i + 1 < n_steps:
                pw = [_mm(pw[q], stack2(pw[q])) for q in n]
        for q in n:
            d, p = chains[q]
            a_r = jnp.concatenate([jnp.where(incl2[d], g[q][C:, :2 * C], 0.0),
                                   jnp.where(incl2[d], g[q][C:, 2 * C:], 0.0)], axis=1)
            outs[d][0, rows[d], p * LANES:(p + 1) * LANES] = bh[q][C:] + _mm(
                a_r, jnp.concatenate([v2[q], stack2(x[q])], axis=0))
        new = []
        for q in n:
            dg = jnp.where(eye, jnp.broadcast_to(op("e0sq", q), (LANES, LANES)), 0.0)
            hn = _mm(jnp.concatenate([op("kh", q), op("ah", q), dg], axis=0),
                     jnp.concatenate([op("v", q), x[q], hp[q]], axis=0), _TN)
            new.append(jnp.where(same_head, hn, 0.0))
        hp = new
    for q in n:
        h_ref[q] = hp[q]


def wkv7(r, k, v, kk, lw, a, k_a):
    B, S, D = r.shape
    C = WKV_CHUNK * WKV_CHUNKS_PER_STEP
    nc = S // C
    assert S % C == 0 and D % LANES == 0
    fwd = pl.BlockSpec((1, C, D), lambda b, c: (b, c, 0))
    bwd = pl.BlockSpec((1, C, D), lambda b, c: (b, nc - 1 - c, 0))
    fwd2 = pl.BlockSpec((1, 1, C, D), lambda b, c: (b, 0, c, 0))
    bwd2 = pl.BlockSpec((1, 1, C, D), lambda b, c: (b, 1, nc - 1 - c, 0))
    out = jax.ShapeDtypeStruct((B, S, D), F32)
    return pl.pallas_call(
        _wkv_kernel,
        grid=(B, nc),
        in_specs=[fwd] * 4 + [bwd] * 4 + [fwd2, bwd2, fwd2, bwd2, pl.BlockSpec((1, D), lambda b, c: (0, 0))],
        out_specs=[fwd, bwd],
        out_shape=[out, out],
        scratch_shapes=[pltpu.VMEM((2 * D // LANES, LANES, LANES), F32)],
        compiler_params=pltpu.CompilerParams(dimension_semantics=("parallel", "arbitrary")),
        name="wkv7_chunked",
    )(r, k, v, kk, r, k, v, kk, lw, lw, a, a, k_a.reshape(1, D))


FFT_N2 = 64
FFT_K1_BLOCK = 8


def _round_up(n, m):
    return (n + m - 1) // m * m


def _twice(m, axis):
    hi = jnp.asarray(m, F32).astype(BF16)
    return jnp.concatenate([hi, hi], axis=axis)


def _hilo(x):
    hi, lo = _split(x)
    return jnp.concatenate([hi, lo], axis=0)


@functools.lru_cache(maxsize=None)
def _fft_tables(seq_len):
    n = 2 * seq_len
    n2 = FFT_N2
    n1 = n // n2
    h1 = n1 // 2
    k1n = h1 + 1
    k1p = _round_up(k1n, FFT_K1_BLOCK)
    k1 = np.arange(k1n)[:, None]
    th1 = 2 * np.pi * k1 * np.arange(h1)[None, :] / n1
    f1 = np.zeros((2 * k1p, h1))
    f1[:k1n] = np.cos(th1)
    f1[k1p:k1p + k1n] = -np.sin(th1)
    m = np.arange(n2)
    th2 = 2 * np.pi * (m[None, None, :] * m[None, :, None] / n2 + m[None, None, :] * np.arange(k1n)[:, None, None] / n)
    gr, gi = np.cos(th2), -np.sin(th2)
    g = np.zeros((k1p, 2 * n2, 2 * n2))
    g[:k1n] = np.block([[gr, -gi], [gi, gr]])
    gt = np.transpose(g, (0, 2, 1))
    coef = np.full((k1n,), 2.0)
    coef[0] = 1.0
    coef[-1] = 1.0
    th3 = th1.T
    m3 = np.zeros((h1, 2 * k1p))
    m3[:, :k1n] = coef * np.cos(th3) / n
    m3[:, k1p:k1p + k1n] = -coef * np.sin(th3) / n
    th1f = 2 * np.pi * k1 * np.arange(n1)[None, :] / n1
    f1_full = np.zeros((2 * k1p, n1))
    f1_full[:k1n] = np.cos(th1f)
    f1_full[k1p:k1p + k1n] = -np.sin(th1f)
    return dict(n1=n1, h1=h1, k1n=k1n, k1p=k1p, f1=f1, g=g, gt=gt, m3=m3, f1_full=f1_full)


FFT_N2_BLOCK = 8


def _rows_to_lanes(ref):
    return jnp.concatenate([ref[0, :, i, :] for i in range(FFT_N2_BLOCK)], axis=1)


def _lanes_to_rows(ref, val):
    ch = ref.shape[3]
    for i in range(FFT_N2_BLOCK):
        ref[0, :, i, :] = val[:, i * ch:(i + 1) * ch]


def _fft_s1_kernel(f_ref, x_ref, o_ref):
    _lanes_to_rows(o_ref, jnp.dot(f_ref[...], _hilo(_rows_to_lanes(x_ref)), preferred_element_type=F32))


def _fft_mid_kernel(g_ref, gt_ref, kr_ref, ki_ref, a_ref, o_ref):
    n2 = FFT_N2
    for j in range(FFT_K1_BLOCK):
        xin = jnp.concatenate([a_ref[0, 0, j], a_ref[0, 1, j]], axis=0)
        z = jnp.dot(g_ref[j], _hilo(xin), preferred_element_type=F32)
        zr, zi = z[:n2], z[n2:]
        kr, ki = kr_ref[j], ki_ref[j]
        y = jnp.concatenate([zr * kr - zi * ki, zr * ki + zi * kr], axis=0)
        b = jnp.dot(gt_ref[j], _hilo(y), preferred_element_type=F32)
        o_ref[0, 0, j] = b[:n2]
        o_ref[0, 1, j] = b[n2:]


def _fft_fwd_kernel(g_ref, a_ref, re_ref, im_ref):
    n2 = FFT_N2
    for j in range(FFT_K1_BLOCK):
        xin = jnp.concatenate([a_ref[0, 0, j], a_ref[0, 1, j]], axis=0)
        z = jnp.dot(g_ref[j], _hilo(xin), preferred_element_type=F32)
        re_ref[j] = z[:n2]
        im_ref[j] = z[n2:]


def _fft_s3_kernel(m_ref, b_ref, u_ref, x_ref, bias_ref, o_ref):
    y = jnp.dot(m_ref[...], _hilo(_rows_to_lanes(b_ref)), preferred_element_type=F32)
    _lanes_to_rows(o_ref, _rows_to_lanes(x_ref) * (y + _rows_to_lanes(u_ref) * bias_ref[...]))


def _long_conv_gated(u, gate, kf_re, kf_im, bias):
    bsz, seq_len, ch = u.shape
    t = _fft_tables(seq_len)
    n2, h1, k1p = FFT_N2, t["h1"], t["k1p"]
    lanes = n2 * ch
    nb = FFT_N2_BLOCK
    lt = nb * ch
    assert seq_len == h1 * n2 and ch % LANES == 0
    params = pltpu.CompilerParams(dimension_semantics=("parallel", "parallel"),
                                  vmem_limit_bytes=VMEM_LIMIT)
    spec4 = pl.BlockSpec((1, 2 * k1p, nb, ch), lambda b, j: (b, 0, j, 0))
    row = pl.BlockSpec((1, h1, nb, ch), lambda b, j: (b, 0, j, 0))
    f1 = _twice(t["f1"], 1)
    a = pl.pallas_call(
        _fft_s1_kernel,
        grid=(bsz, n2 // nb),
        in_specs=[pl.BlockSpec(f1.shape, lambda b, j: (0, 0)), row],
        out_specs=spec4,
        out_shape=jax.ShapeDtypeStruct((bsz, 2 * k1p, n2, ch), F32),
        compiler_params=params, name="hyena_dft1",
    )(f1, u.reshape(bsz, h1, n2, ch))
    kb = FFT_K1_BLOCK
    g = _twice(t["g"], 2)
    gt = _twice(t["gt"], 2)
    mat_spec = pl.BlockSpec((kb,) + g.shape[1:], lambda b, j: (j, 0, 0))
    kf_spec = pl.BlockSpec((kb, n2, ch), lambda b, j: (j, 0, 0))
    blk = pl.BlockSpec((1, 2, kb, n2, ch), lambda b, j: (b, 0, j, 0, 0))
    bm = pl.pallas_call(
        _fft_mid_kernel,
        grid=(bsz, k1p // kb),
        in_specs=[mat_spec, mat_spec, kf_spec, kf_spec, blk],
        out_specs=blk,
        out_shape=jax.ShapeDtypeStruct((bsz, 2, k1p, n2, ch), F32),
        compiler_params=params, name="hyena_dft2",
    )(g, gt, kf_re, kf_im, a.reshape(bsz, 2, k1p, n2, ch))
    m3 = _twice(t["m3"], 1)
    out = pl.pallas_call(
        _fft_s3_kernel,
        grid=(bsz, n2 // nb),
        in_specs=[pl.BlockSpec(m3.shape, lambda b, j: (0, 0)), spec4,
                  row, row, pl.BlockSpec((1, lt), lambda b, j: (0, j))],
        out_specs=row,
        out_shape=jax.ShapeDtypeStruct((bsz, h1, n2, ch), F32),
        compiler_params=params, name="hyena_dft3",
    )(m3, bm.reshape(bsz, 2 * k1p, n2, ch), u.reshape(bsz, h1, n2, ch), gate.reshape(bsz, h1, n2, ch),
      jnp.tile(bias, n2).reshape(1, lanes))
    return out.reshape(bsz, seq_len, ch)


def _filter_spectrum(filt_o, seq_len):
    t = _fft_tables(seq_len)
    n1, n2, k1p = t["n1"], FFT_N2, t["k1p"]
    nb, kb = FFT_N2_BLOCK, FFT_K1_BLOCK
    ch = filt_o.shape[-1]
    k_full = jnp.concatenate([filt_o[:, 0], filt_o[::-1, 1]], axis=0)
    f1 = _twice(t["f1_full"], 1)
    params = pltpu.CompilerParams(dimension_semantics=("parallel", "parallel"),
                                  vmem_limit_bytes=VMEM_LIMIT)
    a = pl.pallas_call(
        _fft_s1_kernel,
        grid=(1, n2 // nb),
        in_specs=[pl.BlockSpec(f1.shape, lambda b, j: (0, 0)),
                  pl.BlockSpec((1, n1, nb, ch), lambda b, j: (b, 0, j, 0))],
        out_specs=pl.BlockSpec((1, 2 * k1p, nb, ch), lambda b, j: (b, 0, j, 0)),
        out_shape=jax.ShapeDtypeStruct((1, 2 * k1p, n2, ch), F32),
        compiler_params=params, name="filter_dft1",
    )(f1, k_full.reshape(1, n1, n2, ch))
    g = _twice(t["g"], 2)
    kf_spec = pl.BlockSpec((kb, n2, ch), lambda b, j: (j, 0, 0))
    out = jax.ShapeDtypeStruct((k1p, n2, ch), F32)
    return pl.pallas_call(
        _fft_fwd_kernel,
        grid=(1, k1p // kb),
        in_specs=[pl.BlockSpec((kb,) + g.shape[1:], lambda b, j: (j, 0, 0)),
                  pl.BlockSpec((1, 2, kb, n2, ch), lambda b, j: (b, 0, j, 0, 0))],
        out_specs=[kf_spec, kf_spec],
        out_shape=[out, out],
        compiler_params=params, name="filter_dft2",
    )(g, a.reshape(1, 2, k1p, n2, ch))


ROUTE_COLS = 16
ROUTE_TOKEN_CHUNK = 512


def _route_kernel(aff_ref, o_ref, key_ref, val_ref, *, cap):
    a = aff_ref[0]
    n_exp, seq_len = a.shape
    bits = pltpu.bitcast(a, jnp.int32)

    def bisect(i, t):
        cand = t | jnp.left_shift(1, 30 - i)
        cnt = jnp.sum((bits >= cand).astype(jnp.int32), axis=1, keepdims=True)
        return jnp.where(cnt >= cap, cand, t)

    thr = lax.fori_loop(0, 31, bisect, jnp.zeros((n_exp, 1), jnp.int32))
    gt = bits > thr
    eq = bits == thr

    ri = lax.broadcasted_iota(jnp.int32, (LANES, LANES), 0)
    ci = lax.broadcasted_iota(jnp.int32, (LANES, LANES), 1)
    upper = (ri < ci).astype(BF16)

    def prefix(mask):
        m = mask.astype(F32)
        run = jnp.zeros((n_exp, 1), F32)
        parts = []
        for j in range(seq_len // LANES):
            tile = m[:, j * LANES:(j + 1) * LANES]
            parts.append(run + jnp.dot(tile.astype(BF16), upper, preferred_element_type=F32))
            run = run + jnp.sum(tile, axis=1, keepdims=True)
        return jnp.concatenate(parts, axis=1), run

    eq_before, _ = prefix(eq)
    n_gt = jnp.sum(gt.astype(F32), axis=1, keepdims=True)
    sel = jnp.logical_or(gt, jnp.logical_and(eq, eq_before < cap - n_gt))
    pos, _ = prefix(sel)
    key_ref[...] = jnp.where(sel, pos, -1.0)
    val_ref[...] = a

    tok = lax.broadcasted_iota(jnp.int32, (1, seq_len), 1)
    t_hi = (tok // 64).astype(F32)
    t_lo = (tok % 64).astype(F32)
    slot = lax.broadcasted_iota(jnp.int32, (cap, ROUTE_TOKEN_CHUNK), 0).astype(F32)
    zeros = jnp.zeros((ROUTE_COLS - 5, seq_len), F32)

    def compact(e, carry):
        key = key_ref[pl.ds(e, 1), :]
        g = val_ref[pl.ds(e, 1), :]
        g_hi = g.astype(BF16).astype(F32)
        g_mid = (g - g_hi).astype(BF16).astype(F32)
        g_lo = g - g_hi - g_mid
        rows = jnp.concatenate([t_hi, t_lo, g_hi, g_mid, g_lo, zeros], axis=0).astype(BF16)
        acc = jnp.zeros((cap, ROUTE_COLS), F32)
        for j in range(seq_len // ROUTE_TOKEN_CHUNK):
            sl = slice(j * ROUTE_TOKEN_CHUNK, (j + 1) * ROUTE_TOKEN_CHUNK)
            onehot = jnp.where(key[:, sl] == slot, 1.0, 0.0).astype(BF16)
            acc = acc + _dg(onehot, rows[:, sl], _NT)
        o_ref[0, pl.ds(e, 1)] = acc[None]
        return carry

    lax.fori_loop(0, n_exp, compact, 0)


def _route(aff_t, cap):
    bsz, n_exp, seq_len = aff_t.shape
    assert seq_len % ROUTE_TOKEN_CHUNK == 0 and seq_len <= 64 * 256
    out = pl.pallas_call(
        functools.partial(_route_kernel, cap=cap),
        grid=(bsz,),
        in_specs=[pl.BlockSpec((1, n_exp, seq_len), lambda b: (b, 0, 0))],
        out_specs=pl.BlockSpec((1, n_exp, cap, ROUTE_COLS), lambda b: (b, 0, 0, 0)),
        out_shape=jax.ShapeDtypeStruct((bsz, n_exp, cap, ROUTE_COLS), F32),
        scratch_shapes=[pltpu.VMEM((n_exp, seq_len), F32), pltpu.VMEM((n_exp, seq_len), F32)],
        compiler_params=pltpu.CompilerParams(dimension_semantics=("parallel",),
                                             vmem_limit_bytes=VMEM_LIMIT),
        name="ec_route",
    )(aff_t)
    idx = (out[..., 0] * 64.0 + out[..., 1]).astype(jnp.int32)
    gate = out[..., 2] + out[..., 3] + out[..., 4]
    return idx, gate


MOE_FF_TILE = 1024
MOE_GROUP = 16
LN_ROWS = 256


def _moe_kernel(idx_ref, gate_ref, h_ref, wg_ref, wu_ref, wd_ref, lg_ref, lb_ref, o_ref,
                xb_ref, y_ref, *, alpha):
    e = pl.program_id(1)
    f = pl.program_id(2)
    last_e = pl.num_programs(1) - 1
    last_f = pl.num_programs(2) - 1
    cap = xb_ref.shape[0]
    seq_len = h_ref.shape[1]

    @pl.when(jnp.logical_and(e == 0, f == 0))
    def _():
        o_ref[...] = jnp.zeros_like(o_ref)

    @pl.when(f == 0)
    def _():
        def gather(i, carry):
            base = pl.multiple_of(i * MOE_GROUP, MOE_GROUP)
            rows = [h_ref[0, pl.ds(idx_ref[0, 0, base + j], 1), :] for j in range(MOE_GROUP)]
            xb_ref[pl.ds(base, MOE_GROUP), :] = jnp.concatenate(rows, axis=0).astype(BF16)
            return carry

        lax.fori_loop(0, cap // MOE_GROUP, gather, 0)

    xb = xb_ref[...]
    a = jnp.dot(xb, wg_ref[0], preferred_element_type=F32)
    u = jnp.dot(xb, wu_ref[0], preferred_element_type=F32)
    act = (a * jax.nn.sigmoid(a) * u).astype(BF16)
    part = jnp.dot(act, wd_ref[0], preferred_element_type=F32)

    @pl.when(f == 0)
    def _():
        y_ref[...] = part

    @pl.when(f > 0)
    def _():
        y_ref[...] += part

    @pl.when(f == last_f)
    def _():
        def scatter(i, carry):
            base = pl.multiple_of(i * 8, 8)
            y8 = y_ref[pl.ds(base, 8), :]
            toks = [idx_ref[0, 0, base + j] for j in range(8)]
            rows = [o_ref[0, pl.ds(toks[j], 1), :] + gate_ref[0, 0, base + j] * y8[j:j + 1, :]
                    for j in range(8)]
            for j in range(8):
                o_ref[0, pl.ds(toks[j], 1), :] = rows[j]
            return carry

        lax.fori_loop(0, cap // 8, scatter, 0)

    @pl.when(jnp.logical_and(e == last_e, f == last_f))
    def _():
        def norm(i, carry):
            rows = pl.ds(pl.multiple_of(i * LN_ROWS, LN_ROWS), LN_ROWS)
            z = alpha * h_ref[0, rows, :] + o_ref[0, rows, :]
            o_ref[0, rows, :] = _layer_norm(z, lg_ref[...], lb_ref[...])
            return carry

        lax.fori_loop(0, seq_len // LN_ROWS, norm, 0)


def _moe_block(h, idx, gate, wg, wu, wd, ln_g, ln_b, alpha):
    bsz, seq_len, d = h.shape
    n_exp, _, ff = wg.shape
    cap = idx.shape[-1]
    ft = min(MOE_FF_TILE, ff)
    assert ff % ft == 0 and seq_len % LN_ROWS == 0
    smem = pl.BlockSpec((1, 1, cap), lambda b, e, f: (b * n_exp + e, 0, 0), memory_space=pltpu.SMEM)
    whole = pl.BlockSpec((1, seq_len, d), lambda b, e, f: (b, 0, 0), pipeline_mode=pl.Buffered(1))
    vec = pl.BlockSpec((1, d), lambda b, e, f: (0, 0))
    return pl.pallas_call(
        functools.partial(_moe_kernel, alpha=alpha),
        grid=(bsz, n_exp, ff // ft),
        in_specs=[smem, smem, whole,
                  pl.BlockSpec((1, d, ft), lambda b, e, f: (e, 0, f)),
                  pl.BlockSpec((1, d, ft), lambda b, e, f: (e, 0, f)),
                  pl.BlockSpec((1, ft, d), lambda b, e, f: (e, f, 0)),
                  vec, vec],
        out_specs=whole,
        out_shape=jax.ShapeDtypeStruct((bsz, seq_len, d), F32),
        scratch_shapes=[pltpu.VMEM((cap, d), BF16), pltpu.VMEM((cap, d), F32)],
        compiler_params=pltpu.CompilerParams(
            dimension_semantics=("arbitrary", "arbitrary", "arbitrary"), vmem_limit_bytes=VMEM_LIMIT),
        name="ec_moe",
    )(idx.reshape(bsz * n_exp, 1, cap), gate.reshape(bsz * n_exp, 1, cap), h, wg, wu, wd,
      ln_g.reshape(1, d), ln_b.reshape(1, d))


ROW_TILE = 512
MIX_TILE = 512


def _proj_kernel(x_ref, w_ref, *o_refs):
    xb = x_ref[...].astype(BF16)
    off = 0
    for o_ref in o_refs:
        n = o_ref.shape[1]
        o_ref[...] = jnp.dot(xb, w_ref[:, off:off + n], preferred_element_type=F32)
        off += n


def _proj(x, w, widths):
    M, K = x.shape
    tm = ROW_TILE
    assert M % tm == 0 and sum(widths) == w.shape[1] and all(n % LANES == 0 for n in widths)
    return pl.pallas_call(
        _proj_kernel,
        grid=(M // tm,),
        in_specs=[pl.BlockSpec((tm, K), lambda i: (i, 0)),
                  pl.BlockSpec(w.shape, lambda i: (0, 0), pipeline_mode=pl.Buffered(1))],
        out_specs=[pl.BlockSpec((tm, n), lambda i: (i, 0)) for n in widths],
        out_shape=[jax.ShapeDtypeStruct((M, n), F32) for n in widths],
        compiler_params=pltpu.CompilerParams(dimension_semantics=("parallel",),
                                             vmem_limit_bytes=VMEM_LIMIT),
        name="proj",
    )(x, w.astype(BF16))


def _pad_to(w, n, axis):
    pad = [(0, 0)] * w.ndim
    pad[axis] = (0, n - w.shape[axis])
    return jnp.pad(w, pad)


def _ln_kernel(x_ref, g_ref, b_ref, o_ref):
    o_ref[...] = _layer_norm(x_ref[...], g_ref[...], b_ref[...])


def _ln_rows(x, g, b):
    M, D = x.shape
    row = pl.BlockSpec((ROW_TILE, D), lambda i: (i, 0))
    vec = pl.BlockSpec((1, D), lambda i: (0, 0))
    return pl.pallas_call(
        _ln_kernel, grid=(M // ROW_TILE,), in_specs=[row, vec, vec], out_specs=row,
        out_shape=jax.ShapeDtypeStruct((M, D), F32),
        compiler_params=pltpu.CompilerParams(dimension_semantics=("parallel",)),
        name="ln_in",
    )(x, g.reshape(1, D), b.reshape(1, D))


def _halo_specs(ts, width, seq_len):
    per = ts // 8
    last = seq_len // 8 - 1
    cur = pl.BlockSpec((1, ts, width), lambda b, i: (b, i, 0))
    prev = pl.BlockSpec((1, 8, width), lambda b, i: (b, jnp.maximum(i * per - 1, 0), 0))
    nxt = pl.BlockSpec((1, 8, width), lambda b, i: (b, jnp.minimum((i + 1) * per, last), 0))
    return [cur, prev, nxt]


def _neighbours(x, prev_blk, next_blk):
    ts = x.shape[0]
    i = pl.program_id(1)
    rid = lax.broadcasted_iota(jnp.int32, x.shape, 0)
    prev_row = jnp.where(i > 0, prev_blk[7:8, :], 0.0)
    next_row = jnp.where(i < pl.num_programs(1) - 1, next_blk[0:1, :], 0.0)
    zp = jnp.where(rid == 0, prev_row, pltpu.roll(x, 1, 0))
    zn = jnp.where(rid == ts - 1, next_row, pltpu.roll(x, ts - 1, 0))
    return zp, zn


def _shortconv_kernel(p_ref, pp_ref, pn_ref, w_ref, b_ref, v_ref, x1_ref, x2_ref):
    p = p_ref[0]
    zp, zn = _neighbours(p, pp_ref[0], pn_ref[0])
    q = w_ref[0:1, :] * zp + w_ref[1:2, :] * p + w_ref[2:3, :] * zn + b_ref[...]
    d = v_ref.shape[2]
    v_ref[0] = q[:, :d]
    x1_ref[0] = q[:, d:2 * d]
    x2_ref[0] = q[:, 2 * d:]


def _shortconv(p, conv_w, conv_b):
    bsz, seq_len, width = p.shape
    d = width // 3
    ts = ROW_TILE
    out = pl.BlockSpec((1, ts, d), lambda b, i: (b, i, 0))
    return pl.pallas_call(
        _shortconv_kernel,
        grid=(bsz, seq_len // ts),
        in_specs=_halo_specs(ts, width, seq_len) + [pl.BlockSpec((3, width), lambda b, i: (0, 0)),
                                                     pl.BlockSpec((1, width), lambda b, i: (0, 0))],
        out_specs=[out, out, out],
        out_shape=[jax.ShapeDtypeStruct((bsz, seq_len, d), F32)] * 3,
        compiler_params=pltpu.CompilerParams(dimension_semantics=("parallel", "parallel")),
        name="hyena_shortconv",
    )(p, p, p, conv_w, conv_b.reshape(1, width))


def _head_sum(x, bd):
    hi, lo = _split(x)
    return _dg(hi, bd, _NN) + _dg(lo, bd, _NN)


def _softplus(z):
    return jnp.maximum(z, 0.0) + jnp.log(1.0 + jnp.exp(-jnp.abs(z)))


def _rwkv_prep_kernel(*refs, has_res):
    (p_ref, pp_ref, pn_ref, q_ref, qp_ref, qn_ref) = refs[:6]
    refs = refs[6:]
    if has_res:
        vf_ref, refs = refs[0], refs[1:]
    (mu_ref, mul_ref, w2_ref, w0_ref, a2_ref, a0_ref, g2_ref, kk_ref, v0_ref, v1_ref, v2_ref, bd_ref,
     r_out, k_out, v_out, kn_out, lw_out, a_out, g_out) = refs
    d = r_out.shape[2]
    p = p_ref[0]
    zp, zn = _neighbours(p, pp_ref[0], pn_ref[0])
    p = p + mu_ref[0:1, :] * (zp - p) + mu_ref[1:2, :] * (zn - p)
    q = q_ref[0]
    zp, zn = _neighbours(q, qp_ref[0], qn_ref[0])
    q = q + mul_ref[0:1, :] * (zp - q) + mul_ref[1:2, :] * (zn - q)
    r, k, v = p[:, :d], p[:, d:2 * d], p[:, 2 * d:]
    if has_res:
        mix = _dg(_dg(v.astype(BF16), v1_ref[...], _NN).astype(BF16), v2_ref[...], _NN)
        v = v + (vf_ref[0] - v) * jax.nn.sigmoid(v0_ref[...] + mix)
    wd, ad, gd = q[:, :LANES], q[:, LANES:2 * LANES], q[:, 2 * LANES:]
    wpre = w0_ref[...] + _dg(jnp.tanh(wd).astype(BF16), w2_ref[...], _NN)
    lw = -jnp.exp(-_softplus(-wpre) - 0.5)
    a = jax.nn.sigmoid(a0_ref[...] + _dg(ad.astype(BF16), a2_ref[...], _NN))
    g = _dg(jax.nn.sigmoid(gd).astype(BF16), g2_ref[...], _NN)
    kq = k * kk_ref[...]
    norm = jnp.sqrt(_head_sum(kq * kq, bd_ref[...]))
    r_out[0] = r
    k_out[0] = k
    v_out[0] = v
    kn_out[0] = kq / jnp.maximum(norm, 1e-12)
    lw_out[0, 0] = lw[:, :d]
    lw_out[0, 1] = lw[:, d:]
    a_out[0, 0] = a[:, :d]
    a_out[0, 1] = a[:, d:]
    g_out[0] = g


def _block_diag2(m):
    z = jnp.zeros_like(m[0])
    return jnp.concatenate([jnp.concatenate([m[0], z], axis=1), jnp.concatenate([z, m[1]], axis=1)], axis=0)


def _head_ones(d):
    i = np.arange(d) // RW_HEAD
    return jnp.asarray(i[:, None] == i[None, :], BF16)


def _rwkv_prep(p_rkv, p_lora, v_first, v_res, mu_rkv, mu_lora, w0, w2, a0, a2, g2, k_k):
    bsz, seq_len, w3 = p_rkv.shape
    d = w3 // 3
    wl = p_lora.shape[2]
    ts = ROW_TILE
    assert 2 * w2.shape[1] == LANES and 2 * a2.shape[1] == LANES
    has_res = v_res is not None
    const = lambda shape: pl.BlockSpec(shape, lambda b, i: (0,) * len(shape))
    tile = pl.BlockSpec((1, ts, d), lambda b, i: (b, i, 0))
    pair = pl.BlockSpec((1, 2, ts, d), lambda b, i: (b, 0, i, 0))
    if has_res:
        v0, v1, v2 = v_res
        rank = _round_up(v1.shape[1], LANES)
        v0, v1, v2 = v0.reshape(1, d), _pad_to(v1, rank, 1).astype(BF16), _pad_to(v2, rank, 0).astype(BF16)
    else:
        v0, v1, v2 = jnp.zeros((1, d), F32), jnp.zeros((d, LANES), BF16), jnp.zeros((LANES, d), BF16)
    params = [mu_rkv, mu_lora, _block_diag2(w2).astype(BF16), w0.reshape(1, 2 * d),
              _block_diag2(a2).astype(BF16), a0.reshape(1, 2 * d),
              _pad_to(g2, wl - 2 * LANES, 0).astype(BF16), k_k.reshape(1, d), v0, v1, v2, _head_ones(d)]
    args = [p_rkv, p_rkv, p_rkv, p_lora, p_lora, p_lora] + ([v_first] if has_res else []) + params
    tok = jax.ShapeDtypeStruct((bsz, seq_len, d), F32)
    two = jax.ShapeDtypeStruct((bsz, 2, seq_len, d), F32)
    return pl.pallas_call(
        functools.partial(_rwkv_prep_kernel, has_res=has_res),
        grid=(bsz, seq_len // ts),
        in_specs=(_halo_specs(ts, w3, seq_len) + _halo_specs(ts, wl, seq_len) + ([tile] if has_res else [])
                  + [const(x.shape) for x in params]),
        out_specs=[tile, tile, tile, tile, pair, pair, tile],
        out_shape=[tok, tok, tok, tok, two, two, tok],
        compiler_params=pltpu.CompilerParams(dimension_semantics=("parallel", "parallel"),
                                             vmem_limit_bytes=VMEM_LIMIT),
        name="rwkv_prep",
    )(*args)


def _mix_kernel(yf_ref, yb_ref, r_ref, k_ref, v_ref, a_ref, g_ref, yh_ref, pg_ref, h_ref,
                ka_ref, rk_ref, lg_ref, lb_ref, bd_ref, why_ref, wrw_ref, wo_ref, n1g_ref, n1b_ref,
                wr_ref, br_ref, h_out, aff_out, *, alpha, n_exp):
    d_model = h_ref.shape[2]
    bd = bd_ref[...]
    inv = 1.0 / RW_HEAD
    y = yf_ref[0] + yb_ref[0]
    mu = _head_sum(y, bd) * inv
    dy = y - mu
    var = _head_sum(dy * dy, bd) * inv
    yn = dy * lax.rsqrt(var + GN_EPS) * lg_ref[...] + lb_ref[...]
    k = k_ref[0]
    ka = ka_ref[...]
    k_sum = k * (1.0 + (a_ref[0, 0] - 1.0) * ka) + k * (1.0 + (a_ref[0, 1] - 1.0) * ka)
    bonus = _head_sum(r_ref[0] * k_sum * rk_ref[...], bd) * v_ref[0]
    y_rw = (yn + bonus) * g_ref[0]
    o_hy = _dg(yh_ref[0].astype(BF16), why_ref[...], _NN)
    o_rw = _dg(y_rw.astype(BF16), wrw_ref[...], _NN)
    gates = jax.nn.sigmoid(pg_ref[0])
    merged = gates[:, :d_model] * o_hy + gates[:, d_model:] * o_rw
    mo = _dg(merged.astype(BF16), wo_ref[...], _NN)
    h1 = _layer_norm(alpha * h_ref[0] + mo, n1g_ref[...], n1b_ref[...])
    h_out[0] = h1
    logits = _dg(h1.astype(BF16), wr_ref[...], _NN) + br_ref[...]
    col = lax.broadcasted_iota(jnp.int32, logits.shape, 1)
    logits = jnp.where(col < n_exp, logits, -1e30)
    ex = jnp.exp(logits - jnp.max(logits, axis=1, keepdims=True))
    aff = ex / jnp.sum(ex, axis=1, keepdims=True)
    aff_out[0] = jnp.transpose(aff)[:n_exp, :]


def _mix(y_f, y_b, r, k, v, a, g, y_hy, p_gate, h, k_a, r_k, lnx_g, lnx_b, w_o_hy, w_o_rw, w_out, ln_g, ln_b,
         w_router, b_router, alpha):
    bsz, seq_len, d = r.shape
    d_model = h.shape[2]
    n_exp = w_router.shape[1]
    ts = MIX_TILE
    tile = lambda w: pl.BlockSpec((1, ts, w), lambda b, i: (b, i, 0))
    const = lambda shape: pl.BlockSpec(shape, lambda b, i: (0,) * len(shape))
    vec = lambda x: x.reshape(1, -1)
    params = [vec(k_a), vec(r_k), vec(lnx_g), vec(lnx_b), _head_ones(d), w_o_hy.astype(BF16),
              w_o_rw.astype(BF16), w_out.astype(BF16), vec(ln_g), vec(ln_b),
              _pad_to(w_router, LANES, 1).astype(BF16), _pad_to(vec(b_router), LANES, 1)]
    return pl.pallas_call(
        functools.partial(_mix_kernel, alpha=alpha, n_exp=n_exp),
        grid=(bsz, seq_len // ts),
        in_specs=[tile(d), tile(d),
                  tile(d), tile(d), tile(d), pl.BlockSpec((1, 2, ts, d), lambda b, i: (b, 0, i, 0)),
                  tile(d), tile(d), tile(2 * d_model), tile(d_model)] + [const(x.shape) for x in params],
        out_specs=[tile(d_model), pl.BlockSpec((1, n_exp, ts), lambda b, i: (b, 0, i))],
        out_shape=[jax.ShapeDtypeStruct((bsz, seq_len, d_model), F32),
                   jax.ShapeDtypeStruct((bsz, n_exp, seq_len), F32)],
        compiler_params=pltpu.CompilerParams(dimension_semantics=("parallel", "parallel"),
                                             vmem_limit_bytes=VMEM_LIMIT),
        name="mix_out",
    )(y_f, y_b, r, k, v, a, g, y_hy, p_gate, h, *params)


LN_EPS = 1e-5
GN_EPS = 64e-5
HY_ORDER = 2
HY_TARGET = 1e-2
HY_SHORT_DECAY_PCT = 0.3
HY_LONG_DECAY_PCT = 1.5
N_EXPERTS = 16
EC_CAPACITY = 2


def _layer_norm(x, g, b):
    mu = jnp.mean(x, -1, keepdims=True)
    var = jnp.mean(jnp.square(x - mu), -1, keepdims=True)
    return (x - mu) * lax.rsqrt(var + LN_EPS) * g + b


def _hyena_filters(seq_len, w1, b1, w2, b2, w3, freq, d_hy):
    emb = w1.shape[0]
    bands = (emb - 1) // 2
    t = jnp.linspace(0.0, 1.0, seq_len, dtype=F32)[:, None]
    w = 2.0 * math.pi * jnp.arange(seq_len, dtype=F32)[:, None] / seq_len
    f = jnp.linspace(1e-4, bands - 1, bands, dtype=F32)[None, :]
    z = jnp.concatenate([t, jnp.cos(f * w), -jnp.sin(f * w)], axis=-1)
    a = jnp.sin(freq * (z @ w1 + b1))
    for i in range(w2.shape[0]):
        a = jnp.sin(freq * (a @ w2[i] + b2[i]))
    filt = (a @ w3).reshape(seq_len, HY_ORDER, 2, d_hy)
    max_decay = math.log(HY_TARGET) / HY_SHORT_DECAY_PCT
    min_decay = math.log(HY_TARGET) / HY_LONG_DECAY_PCT
    deltas = jnp.linspace(min_decay, max_decay, d_hy, dtype=F32)
    filt = filt * jnp.exp(-t * jnp.abs(deltas))[:, None, None, :]
    filt = filt * lax.rsqrt(jnp.sum(filt * filt, axis=(0, 2), keepdims=True) + 1e-12)
    return filt


def _hyena_branch(p, conv_w, conv_b, filt, bias):
    seq_len = p.shape[1]
    v, x1, x2 = _shortconv(p, conv_w, conv_b)
    z = _long_conv_gated(v, x1, *_filter_spectrum(filt[:, 0], seq_len), bias[0])
    return _long_conv_gated(z, x2, *_filter_spectrum(filt[:, 1], seq_len), bias[1])


def kernel(x, ln0_g, ln0_b, w_in, hy_conv_w, hy_conv_b, hy_ffn_w1, hy_ffn_b1, hy_ffn_w2, hy_ffn_b2,
           hy_ffn_w3, hy_freq, hy_bias, rw_mu, rw_w0, rw_w2, rw_a0, rw_a2, rw_v0, rw_v1, rw_v2, rw_g2,
           rw_k_k, rw_k_a, rw_r_k, rw_lnx_g, rw_lnx_b, w_o_hy, w_o_rw, w_out, ln1_g, ln1_b,
           w_router, b_router, w_gate, w_up, w_down, ln2_g, ln2_b):
    bsz, seq_len, d_model = x.shape
    depth = w_in.shape[0]
    d_hy = hy_bias.shape[-1]
    d_rw = rw_w0.shape[-1]
    p_hy = 3 * d_hy
    p_rw = rw_mu.shape[-1]
    lora = p_rw - 3 * d_rw
    lora_pad = _round_up(lora, LANES)
    widths = (p_hy, 3 * d_rw, lora_pad, 2 * d_model)
    alpha = (2 * depth) ** 0.25
    cap = EC_CAPACITY * seq_len // N_EXPERTS
    tokens = bsz * seq_len
    h = _ln_rows(x.reshape(tokens, d_model), ln0_g, ln0_b).reshape(bsz, seq_len, d_model)
    v_first = None
    for l in range(depth):
        c0, c1, c2 = p_hy, p_hy + 3 * d_rw, p_hy + p_rw
        w = jnp.concatenate([w_in[l][:, :c1], _pad_to(w_in[l][:, c1:c2], lora_pad, 1), w_in[l][:, c2:]], axis=1)
        pa, pb, pl_, pg = [t.reshape(bsz, seq_len, -1) for t in _proj(h.reshape(tokens, d_model), w, widths)]
        filt = _hyena_filters(seq_len, hy_ffn_w1[l], hy_ffn_b1[l], hy_ffn_w2[l], hy_ffn_b2[l],
                              hy_ffn_w3[l], hy_freq[l], d_hy)
        y_hy = _hyena_branch(pa, hy_conv_w[l], hy_conv_b[l], filt, hy_bias[l])
        v_res = None if l == 0 else (rw_v0[l - 1], rw_v1[l - 1], rw_v2[l - 1])
        mu = rw_mu[l]
        r, k, v, kk, lw, a, g = _rwkv_prep(pb, pl_, v_first, v_res, mu[:, :3 * d_rw],
                                           _pad_to(mu[:, 3 * d_rw:], lora_pad, 1), rw_w0[l], rw_w2[l],
                                           rw_a0[l], rw_a2[l], rw_g2[l], rw_k_k[l])
        if l == 0:
            v_first = v
        y_f, y_b = wkv7(r, k, v, kk, lw, a, rw_k_a[l])
        h, aff_t = _mix(y_f, y_b, r, k, v, a, g, y_hy, pg, h, rw_k_a[l], rw_r_k[l], rw_lnx_g[l], rw_lnx_b[l],
                        w_o_hy[l], w_o_rw[l], w_out[l], ln1_g[l], ln1_b[l], w_router[l], b_router[l], alpha)
        idx, gate = _route(aff_t, cap)
        h = _moe_block(h, idx, gate, w_gate[l].astype(BF16), w_up[l].astype(BF16), w_down[l].astype(BF16),
                       ln2_g[l], ln2_b[l], alpha)
    return h
```

```python
import functools
import math

import jax
import jax.numpy as jnp
import numpy as np
from jax import lax
from jax.experimental import pallas as pl
from jax.experimental.pallas import tpu as pltpu

F32 = jnp.float32
BF16 = jnp.bfloat16

RW_HEAD = 64
LANES = 128
WKV_CHUNK = 64
WKV_CHUNKS_PER_STEP = 4
VMEM_LIMIT = 56 * 1024 * 1024


def _split(x):
    hi = x.astype(BF16)
    lo = (x - hi.astype(F32)).astype(BF16)
    return hi, lo


def _dg(a, b, dims):
    return lax.dot_general(a, b, (dims, ((), ())), preferred_element_type=F32)


def _mm(a, b, dims=((1,), (0,)), passes=1):
    if passes == 1:
        return _dg(a.astype(BF16), b.astype(BF16), dims)
    ah, al = _split(a)
    bh, bl = _split(b)
    return _dg(ah, bh, dims) + (_dg(ah, bl, dims) + _dg(al, bh, dims))


_NN = ((1,), (0,))
_NT = ((1,), (1,))
_TN = ((0,), (0,))


def _wkv_operands(r, k, v, kk, lw, a, ka, reverse):
    C = WKV_CHUNK
    row = lax.broadcasted_iota(jnp.int32, (C, C), 0)
    col = lax.broadcasted_iota(jnp.int32, (C, C), 1)
    incl = (col >= row) if reverse else (col <= row)

    lw_hi, lw_lo = _split(lw)
    tri = incl.astype(BF16)
    cin = _dg(tri, lw_hi, _NN) + _dg(tri, lw_lo, _NN)
    tot = jnp.sum(lw, axis=0, keepdims=True)
    half = 0.5 * tot
    e0 = jnp.exp(half)
    e1 = jnp.exp(cin - half)
    e2 = jnp.exp(half - cin)
    ew = jnp.exp(-lw)
    kdir = k * (1.0 + (a - 1.0) * ka)
    rt = r * e1
    bt = kk * (e1 * ew)
    kt = kdir * e2
    at = -(kk * a) * e2
    r0 = rt * e0
    b0 = bt * e0
    kh = kt * e0
    ah = at * e0
    return dict(bt=bt, rt=rt, kt=kt, at=at, b0=b0, r0=r0, kh=kh, ah=ah, e0sq=e0 * e0, v=v)


def _wkv_kernel(rf_ref, kf_ref, vf_ref, nf_ref, rb_ref, kb_ref, vb_ref, nb_ref, lwf_ref, lwb_ref,
                af_ref, ab_ref, ka_ref, yf_ref, yb_ref, h_ref):
    C = WKV_CHUNK

    @pl.when(pl.program_id(1) == 0)
    def _():
        h_ref[...] = jnp.zeros_like(h_ref)

    ka = ka_ref[...]
    n_pairs = ka.shape[1] // LANES
    outs = [yf_ref, yb_ref]

    row2 = lax.broadcasted_iota(jnp.int32, (C, 2 * C), 0)
    col2 = lax.broadcasted_iota(jnp.int32, (C, 2 * C), 1) % C
    earlier2 = [col2 < row2, col2 > row2]
    incl2 = [col2 <= row2, col2 >= row2]
    lane = lax.broadcasted_iota(jnp.int32, (1, LANES), 1)
    first = lane < RW_HEAD
    ri = lax.broadcasted_iota(jnp.int32, (LANES, LANES), 0)
    ci = lax.broadcasted_iota(jnp.int32, (LANES, LANES), 1)
    eye = ri == ci
    same_head = (ri // RW_HEAD) == (ci // RW_HEAD)

    def stack2(x):
        return jnp.concatenate([jnp.where(first, x, 0.0), jnp.where(first, 0.0, x)], axis=0)

    chains = [(d, p) for d in range(2) for p in range(n_pairs)]
    n = range(len(chains))
    hp = [h_ref[q] for q in n]
    n_sub = rf_ref.shape[1] // C
    for s in range(n_sub):
        fs = slice(s * C, (s + 1) * C)
        bs = slice((n_sub - 1 - s) * C, (n_sub - s) * C)
        rows = [fs, bs]
        ops = [_wkv_operands(rf_ref[0, fs, :], kf_ref[0, fs, :], vf_ref[0, fs, :], nf_ref[0, fs, :],
                             lwf_ref[0, 0, fs, :], af_ref[0, 0, fs, :], ka, False),
               _wkv_operands(rb_ref[0, bs, :], kb_ref[0, bs, :], vb_ref[0, bs, :], nb_ref[0, bs, :],
                             lwb_ref[0, 0, bs, :], ab_ref[0, 0, bs, :], ka, True)]

        def op(name, q):
            d, p = chains[q]
            return ops[d][name][:, p * LANES:(p + 1) * LANES]

        g = [_mm(jnp.concatenate([op("bt", q), op("rt", q)], axis=0),
                 jnp.concatenate([stack2(op("kt", q)), stack2(op("at", q))], axis=0), _NT) for q in n]
        bh = [_mm(jnp.concatenate([op("b0", q), op("r0", q)], axis=0), hp[q]) for q in n]
        v2 = [stack2(op("v", q)) for q in n]
        x = [bh[q][:C] + _mm(jnp.where(earlier2[chains[q][0]], g[q][:C, :2 * C], 0.0), v2[q]) for q in n]
        pw = [jnp.where(earlier2[chains[q][0]], g[q][:C, 2 * C:], 0.0) for q in n]
        n_steps = C.bit_length() - 1
        for i in range(n_steps):
            x = [x[q] + _mm(pw[q], stack2(x[q])) for q in n]
            if i + 1 < n_steps:
                pw = [_mm(pw[q], stack2(pw[q])) for q in n]
        for q in n:
            d, p = chains[q]
            a_r = jnp.concatenate([jnp.where(incl2[d], g[q][C:, :2 * C], 0.0),
                                   jnp.where(incl2[d], g[q][C:, 2 * C:], 0.0)], axis=1)
            outs[d][0, rows[d], p * LANES:(p + 1) * LANES] = bh[q][C:] + _mm(
                a_r, jnp.concatenate([v2[q], stack2(x[q])], axis=0))
        new = []
        for q in n:
            dg = jnp.where(eye, jnp.broadcast_to(op("e0sq", q), (LANES, LANES)), 0.0)
            hn = _mm(jnp.concatenate([op("kh", q), op("ah", q), dg], axis=0),
                     jnp.concatenate([op("v", q), x[q], hp[q]], axis=0), _TN)
            new.append(jnp.where(same_head, hn, 0.0))
        hp = new
    for q in n:
        h_ref[q] = hp[q]


def wkv7(r, k, v, kk, lw, a, k_a):
    B, S, D = r.shape
    C = WKV_CHUNK * WKV_CHUNKS_PER_STEP
    nc = S // C
    assert S % C == 0 and D % LANES == 0
    fwd = pl.BlockSpec((1, C, D), lambda b, c: (b, c, 0))
    bwd = pl.BlockSpec((1, C, D), lambda b, c: (b, nc - 1 - c, 0))
    fwd2 = pl.BlockSpec((1, 1, C, D), lambda b, c: (b, 0, c, 0))
    bwd2 = pl.BlockSpec((1, 1, C, D), lambda b, c: (b, 1, nc - 1 - c, 0))
    out = jax.ShapeDtypeStruct((B, S, D), F32)
    return pl.pallas_call(
        _wkv_kernel,
        grid=(B, nc),
        in_specs=[fwd] * 4 + [bwd] * 4 + [fwd2, bwd2, fwd2, bwd2, pl.BlockSpec((1, D), lambda b, c: (0, 0))],
        out_specs=[fwd, bwd],
        out_shape=[out, out],
        scratch_shapes=[pltpu.VMEM((2 * D // LANES, LANES, LANES), F32)],
        compiler_params=pltpu.CompilerParams(dimension_semantics=("parallel", "arbitrary")),
        name="wkv7_chunked",
    )(r, k, v, kk, r, k, v, kk, lw, lw, a, a, k_a.reshape(1, D))


FFT_N2 = 64
FFT_K1_BLOCK = 8


def _round_up(n, m):
    return (n + m - 1) // m * m


def _twice(m, axis):
    hi = jnp.asarray(m, F32).astype(BF16)
    return jnp.concatenate([hi, hi], axis=axis)


def _hilo(x):
    hi, lo = _split(x)
    return jnp.concatenate([hi, lo], axis=0)


@functools.lru_cache(maxsize=None)
def _fft_tables(seq_len):
    n = 2 * seq_len
    n2 = FFT_N2
    n1 = n // n2
    h1 = n1 // 2
    k1n = h1 + 1
    k1p = _round_up(k1n, FFT_K1_BLOCK)
    k1 = np.arange(k1n)[:, None]
    th1 = 2 * np.pi * k1 * np.arange(h1)[None, :] / n1
    f1 = np.zeros((2 * k1p, h1))
    f1[:k1n] = np.cos(th1)
    f1[k1p:k1p + k1n] = -np.sin(th1)
    m = np.arange(n2)
    th2 = 2 * np.pi * (m[None, None, :] * m[None, :, None] / n2 + m[None, None, :] * np.arange(k1n)[:, None, None] / n)
    gr, gi = np.cos(th2), -np.sin(th2)
    g = np.zeros((k1p, 2 * n2, 2 * n2))
    g[:k1n] = np.block([[gr, -gi], [gi, gr]])
    gt = np.transpose(g, (0, 2, 1))
    coef = np.full((k1n,), 2.0)
    coef[0] = 1.0
    coef[-1] = 1.0
    th3 = th1.T
    m3 = np.zeros((h1, 2 * k1p))
    m3[:, :k1n] = coef * np.cos(th3) / n
    m3[:, k1p:k1p + k1n] = -coef * np.sin(th3) / n
    th1f = 2 * np.pi * k1 * np.arange(n1)[None, :] / n1
    f1_full = np.zeros((2 * k1p, n1))
    f1_full[:k1n] = np.cos(th1f)
    f1_full[k1p:k1p + k1n] = -np.sin(th1f)
    return dict(n1=n1, h1=h1, k1n=k1n, k1p=k1p, f1=f1, g=g, gt=gt, m3=m3, f1_full=f1_full)


FFT_N2_BLOCK = 8


def _rows_to_lanes(ref):
    return jnp.concatenate([ref[0, :, i, :] for i in range(FFT_N2_BLOCK)], axis=1)


def _lanes_to_rows(ref, val):
    ch = ref.shape[3]
    for i in range(FFT_N2_BLOCK):
        ref[0, :, i, :] = val[:, i * ch:(i + 1) * ch]


def _fft_s1_kernel(f_ref, x_ref, o_ref):
    _lanes_to_rows(o_ref, jnp.dot(f_ref[...], _hilo(_rows_to_lanes(x_ref)), preferred_element_type=F32))


def _fft_mid_kernel(g_ref, gt_ref, kr_ref, ki_ref, a_ref, o_ref):
    n2 = FFT_N2
    for j in range(FFT_K1_BLOCK):
        xin = jnp.concatenate([a_ref[0, 0, j], a_ref[0, 1, j]], axis=0)
        z = jnp.dot(g_ref[j], _hilo(xin), preferred_element_type=F32)
        zr, zi = z[:n2], z[n2:]
        kr, ki = kr_ref[j], ki_ref[j]
        y = jnp.concatenate([zr * kr - zi * ki, zr * ki + zi * kr], axis=0)
        b = jnp.dot(gt_ref[j], _hilo(y), preferred_element_type=F32)
        o_ref[0, 0, j] = b[:n2]
        o_ref[0, 1, j] = b[n2:]


def _fft_fwd_kernel(g_ref, a_ref, re_ref, im_ref):
    n2 = FFT_N2
    for j in range(FFT_K1_BLOCK):
        xin = jnp.concatenate([a_ref[0, 0, j], a_ref[0, 1, j]], axis=0)
        z = jnp.dot(g_ref[j], _hilo(xin), preferred_element_type=F32)
        re_ref[j] = z[:n2]
        im_ref[j] = z[n2:]


def _fft_s3_kernel(m_ref, b_ref, u_ref, x_ref, bias_ref, o_ref):
    y = jnp.dot(m_ref[...], _hilo(_rows_to_lanes(b_ref)), preferred_element_type=F32)
    _lanes_to_rows(o_ref, _rows_to_lanes(x_ref) * (y + _rows_to_lanes(u_ref) * bias_ref[...]))


def _long_conv_gated(u, gate, kf_re, kf_im, bias):
    bsz, seq_len, ch = u.shape
    t = _fft_tables(seq_len)
    n2, h1, k1p = FFT_N2, t["h1"], t["k1p"]
    lanes = n2 * ch
    nb = FFT_N2_BLOCK
    lt = nb * ch
    assert seq_len == h1 * n2 and ch % LANES == 0
    params = pltpu.CompilerParams(dimension_semantics=("parallel", "parallel"),
                                  vmem_limit_bytes=VMEM_LIMIT)
    spec4 = pl.BlockSpec((1, 2 * k1p, nb, ch), lambda b, j: (b, 0, j, 0))
    row = pl.BlockSpec((1, h1, nb, ch), lambda b, j: (b, 0, j, 0))
    f1 = _twice(t["f1"], 1)
    a = pl.pallas_call(
        _fft_s1_kernel,
        grid=(bsz, n2 // nb),
        in_specs=[pl.BlockSpec(f1.shape, lambda b, j: (0, 0)), row],
        out_specs=spec4,
        out_shape=jax.ShapeDtypeStruct((bsz, 2 * k1p, n2, ch), F32),
        compiler_params=params, name="hyena_dft1",
    )(f1, u.reshape(bsz, h1, n2, ch))
    kb = FFT_K1_BLOCK
    g = _twice(t["g"], 2)
    gt = _twice(t["gt"], 2)
    mat_spec = pl.BlockSpec((kb,) + g.shape[1:], lambda b, j: (j, 0, 0))
    kf_spec = pl.BlockSpec((kb, n2, ch), lambda b, j: (j, 0, 0))
    blk = pl.BlockSpec((1, 2, kb, n2, ch), lambda b, j: (b, 0, j, 0, 0))
    bm = pl.pallas_call(
        _fft_mid_kernel,
        grid=(bsz, k1p // kb),
        in_specs=[mat_spec, mat_spec, kf_spec, kf_spec, blk],
        out_specs=blk,
        out_shape=jax.ShapeDtypeStruct((bsz, 2, k1p, n2, ch), F32),
        compiler_params=params, name="hyena_dft2",
    )(g, gt, kf_re, kf_im, a.reshape(bsz, 2, k1p, n2, ch))
    m3 = _twice(t["m3"], 1)
    out = pl.pallas_call(
        _fft_s3_kernel,
        grid=(bsz, n2 // nb),
        in_specs=[pl.BlockSpec(m3.shape, lambda b, j: (0, 0)), spec4,
                  row, row, pl.BlockSpec((1, lt), lambda b, j: (0, j))],
        out_specs=row,
        out_shape=jax.ShapeDtypeStruct((bsz, h1, n2, ch), F32),
        compiler_params=params, name="hyena_dft3",
    )(m3, bm.reshape(bsz, 2 * k1p, n2, ch), u.reshape(bsz, h1, n2, ch), gate.reshape(bsz, h1, n2, ch),
      jnp.tile(bias, n2).reshape(1, lanes))
    return out.reshape(bsz, seq_len, ch)


def _filter_spectrum(filt_o, seq_len):
    t = _fft_tables(seq_len)
    n1, n2, k1p = t["n1"], FFT_N2, t["k1p"]
    nb, kb = FFT_N2_BLOCK, FFT_K1_BLOCK
    ch = filt_o.shape[-1]
    k_full = jnp.concatenate([filt_o[:, 0], filt_o[::-1, 1]], axis=0)
    f1 = _twice(t["f1_full"], 1)
    params = pltpu.CompilerParams(dimension_semantics=("parallel", "parallel"),
                                  vmem_limit_bytes=VMEM_LIMIT)
    a = pl.pallas_call(
        _fft_s1_kernel,
        grid=(1, n2 // nb),
        in_specs=[pl.BlockSpec(f1.shape, lambda b, j: (0, 0)),
                  pl.BlockSpec((1, n1, nb, ch), lambda b, j: (b, 0, j, 0))],
        out_specs=pl.BlockSpec((1, 2 * k1p, nb, ch), lambda b, j: (b, 0, j, 0)),
        out_shape=jax.ShapeDtypeStruct((1, 2 * k1p, n2, ch), F32),
        compiler_params=params, name="filter_dft1",
    )(f1, k_full.reshape(1, n1, n2, ch))
    g = _twice(t["g"], 2)
    kf_spec = pl.BlockSpec((kb, n2, ch), lambda b, j: (j, 0, 0))
    out = jax.ShapeDtypeStruct((k1p, n2, ch), F32)
    return pl.pallas_call(
        _fft_fwd_kernel,
        grid=(1, k1p // kb),
        in_specs=[pl.BlockSpec((kb,) + g.shape[1:], lambda b, j: (j, 0, 0)),
                  pl.BlockSpec((1, 2, kb, n2, ch), lambda b, j: (b, 0, j, 0, 0))],
        out_specs=[kf_spec, kf_spec],
        out_shape=[out, out],
        compiler_params=params, name="filter_dft2",
    )(g, a.reshape(1, 2, k1p, n2, ch))


ROUTE_COLS = 16
ROUTE_TOKEN_CHUNK = 512


def _route_kernel(aff_ref, o_ref, key_ref, val_ref, *, cap):
    a = aff_ref[0]
    n_exp, seq_len = a.shape
    bits = pltpu.bitcast(a, jnp.int32)

    def bisect(i, t):
        cand = t | jnp.left_shift(1, 30 - i)
        cnt = jnp.sum((bits >= cand).astype(jnp.int32), axis=1, keepdims=True)
        return jnp.where(cnt >= cap, cand, t)

    thr = lax.fori_loop(0, 31, bisect, jnp.zeros((n_exp, 1), jnp.int32))
    gt = bits > thr
    eq = bits == thr

    ri = lax.broadcasted_iota(jnp.int32, (LANES, LANES), 0)
    ci = lax.broadcasted_iota(jnp.int32, (LANES, LANES), 1)
    upper = (ri < ci).astype(BF16)

    def prefix(mask):
        m = mask.astype(F32)
        run = jnp.zeros((n_exp, 1), F32)
        parts = []
        for j in range(seq_len // LANES):
            tile = m[:, j * LANES:(j + 1) * LANES]
            parts.append(run + jnp.dot(tile.astype(BF16), upper, preferred_element_type=F32))
            run = run + jnp.sum(tile, axis=1, keepdims=True)
        return jnp.concatenate(parts, axis=1), run

    eq_before, _ = prefix(eq)
    n_gt = jnp.sum(gt.astype(F32), axis=1, keepdims=True)
    sel = jnp.logical_or(gt, jnp.logical_and(eq, eq_before < cap - n_gt))
    pos, _ = prefix(sel)
    key_ref[...] = jnp.where(sel, pos, -1.0)
    val_ref[...] = a

    tok = lax.broadcasted_iota(jnp.int32, (1, seq_len), 1)
    t_hi = (tok // 64).astype(F32)
    t_lo = (tok % 64).astype(F32)
    slot = lax.broadcasted_iota(jnp.int32, (cap, ROUTE_TOKEN_CHUNK), 0).astype(F32)
    zeros = jnp.zeros((ROUTE_COLS - 5, seq_len), F32)

    def compact(e, carry):
        key = key_ref[pl.ds(e, 1), :]
        g = val_ref[pl.ds(e, 1), :]
        g_hi = g.astype(BF16).astype(F32)
        g_mid = (g - g_hi).astype(BF16).astype(F32)
        g_lo = g - g_hi - g_mid
        rows = jnp.concatenate([t_hi, t_lo, g_hi, g_mid, g_lo, zeros], axis=0).astype(BF16)
        acc = jnp.zeros((cap, ROUTE_COLS), F32)
        for j in range(seq_len // ROUTE_TOKEN_CHUNK):
            sl = slice(j * ROUTE_TOKEN_CHUNK, (j + 1) * ROUTE_TOKEN_CHUNK)
            onehot = jnp.where(key[:, sl] == slot, 1.0, 0.0).astype(BF16)
            acc = acc + _dg(onehot, rows[:, sl], _NT)
        o_ref[0, pl.ds(e, 1)] = acc[None]
        return carry

    lax.fori_loop(0, n_exp, compact, 0)


def _route(aff_t, cap):
    bsz, n_exp, seq_len = aff_t.shape
    assert seq_len % ROUTE_TOKEN_CHUNK == 0 and seq_len <= 64 * 256
    out = pl.pallas_call(
        functools.partial(_route_kernel, cap=cap),
        grid=(bsz,),
        in_specs=[pl.BlockSpec((1, n_exp, seq_len), lambda b: (b, 0, 0))],
        out_specs=pl.BlockSpec((1, n_exp, cap, ROUTE_COLS), lambda b: (b, 0, 0, 0)),
        out_shape=jax.ShapeDtypeStruct((bsz, n_exp, cap, ROUTE_COLS), F32),
        scratch_shapes=[pltpu.VMEM((n_exp, seq_len), F32), pltpu.VMEM((n_exp, seq_len), F32)],
        compiler_params=pltpu.CompilerParams(dimension_semantics=("parallel",),
                                             vmem_limit_bytes=VMEM_LIMIT),
        name="ec_route",
    )(aff_t)
    idx = (out[..., 0] * 64.0 + out[..., 1]).astype(jnp.int32)
    gate = out[..., 2] + out[..., 3] + out[..., 4]
    return idx, gate


MOE_FF_TILE = 1024
MOE_GROUP = 16
LN_ROWS = 256


def _gather_rows(h_ref, idx_ref, xb_ref, base):
    rows = [h_ref[0, pl.ds(idx_ref[0, 0, base + j], 1), :] for j in range(MOE_GROUP)]
    xb_ref[pl.ds(base, MOE_GROUP), :] = jnp.concatenate(rows, axis=0).astype(BF16)


def _scatter_rows(o_ref, idx_ref, gate_ref, y_ref, base, enabled):
    y8 = y_ref[pl.ds(base, 8), :]
    toks = [idx_ref[0, 0, base + j] for j in range(8)]
    rows = [o_ref[0, pl.ds(toks[j], 1), :] + jnp.where(enabled, gate_ref[0, 0, base + j], 0.0) * y8[j:j + 1, :]
            for j in range(8)]
    for j in range(8):
        o_ref[0, pl.ds(toks[j], 1), :] = rows[j]


def _moe_kernel(idx_ref, gate_ref, idxn_ref, idxp_ref, gatep_ref, h_ref, wg_ref, wu_ref, wd_ref,
                lg_ref, lb_ref, o_ref, xb_ref, y_ref, *, alpha, n_f):
    e = pl.program_id(1)
    f = pl.program_id(2)
    last_e = pl.num_programs(1) - 1
    last_f = n_f - 1
    cap = xb_ref.shape[1]
    seq_len = h_ref.shape[1]
    share = cap // n_f
    slot = e % 2
    other = 1 - slot
    first_step = jnp.logical_and(e == 0, f == 0)

    @pl.when(first_step)
    def _():
        o_ref[...] = jnp.zeros_like(o_ref)
        y_ref[...] = jnp.zeros_like(y_ref)

        def gather(i, carry):
            _gather_rows(h_ref, idx_ref, xb_ref.at[0], pl.multiple_of(i * MOE_GROUP, MOE_GROUP))
            return carry

        lax.fori_loop(0, cap // MOE_GROUP, gather, 0)

    xb = xb_ref[slot]
    a = jnp.dot(xb, wg_ref[0], preferred_element_type=F32)
    u = jnp.dot(xb, wu_ref[0], preferred_element_type=F32)
    act = (a * jax.nn.sigmoid(a) * u).astype(BF16)
    part = jnp.dot(act, wd_ref[0], preferred_element_type=F32)

    for i in range(share // MOE_GROUP):
        _gather_rows(h_ref, idxn_ref, xb_ref.at[other], pl.multiple_of(f * share + i * MOE_GROUP, MOE_GROUP))
    for i in range(share // 8):
        _scatter_rows(o_ref, idxp_ref, gatep_ref, y_ref.at[other], pl.multiple_of(f * share + i * 8, 8), e > 0)

    @pl.when(f == 0)
    def _():
        y_ref[slot] = part

    @pl.when(f > 0)
    def _():
        y_ref[slot] += part

    @pl.when(jnp.logical_and(e == last_e, f == last_f))
    def _():
        def scatter(i, carry):
            _scatter_rows(o_ref, idx_ref, gate_ref, y_ref.at[slot], pl.multiple_of(i * 8, 8), True)
            return carry

        lax.fori_loop(0, cap // 8, scatter, 0)

    @pl.when(jnp.logical_and(e == last_e, f == last_f))
    def _():
        def norm(i, carry):
            rows = pl.ds(pl.multiple_of(i * LN_ROWS, LN_ROWS), LN_ROWS)
            z = alpha * h_ref[0, rows, :] + o_ref[0, rows, :]
            o_ref[0, rows, :] = _layer_norm(z, lg_ref[...], lb_ref[...])
            return carry

        lax.fori_loop(0, seq_len // LN_ROWS, norm, 0)


def _moe_block(h, idx, gate, wg, wu, wd, ln_g, ln_b, alpha):
    bsz, seq_len, d = h.shape
    n_exp, _, ff = wg.shape
    cap = idx.shape[-1]
    ft = min(MOE_FF_TILE, ff)
    assert ff % ft == 0 and seq_len % LN_ROWS == 0
    n_f = ff // ft
    assert cap % (n_f * MOE_GROUP) == 0

    def smem(shift):
        def index(b, e, f):
            return (b * n_exp + jnp.clip(e + shift, 0, n_exp - 1), 0, 0)
        return pl.BlockSpec((1, 1, cap), index, memory_space=pltpu.SMEM)

    whole = pl.BlockSpec((1, seq_len, d), lambda b, e, f: (b, 0, 0), pipeline_mode=pl.Buffered(1))
    vec = pl.BlockSpec((1, d), lambda b, e, f: (0, 0))
    idx3 = idx.reshape(bsz * n_exp, 1, cap)
    gate3 = gate.reshape(bsz * n_exp, 1, cap)
    return pl.pallas_call(
        functools.partial(_moe_kernel, alpha=alpha, n_f=n_f),
        grid=(bsz, n_exp, n_f),
        in_specs=[smem(0), smem(0), smem(1), smem(-1), smem(-1), whole,
                  pl.BlockSpec((1, d, ft), lambda b, e, f: (e, 0, f)),
                  pl.BlockSpec((1, d, ft), lambda b, e, f: (e, 0, f)),
                  pl.BlockSpec((1, ft, d), lambda b, e, f: (e, f, 0)),
                  vec, vec],
        out_specs=whole,
        out_shape=jax.ShapeDtypeStruct((bsz, seq_len, d), F32),
        scratch_shapes=[pltpu.VMEM((2, cap, d), BF16), pltpu.VMEM((2, cap, d), F32)],
        compiler_params=pltpu.CompilerParams(
            dimension_semantics=("arbitrary", "arbitrary", "arbitrary"), vmem_limit_bytes=VMEM_LIMIT),
        name="ec_moe",
    )(idx3, gate3, idx3, idx3, gate3, h, wg, wu, wd, ln_g.reshape(1, d), ln_b.reshape(1, d))


ROW_TILE = 512
MIX_TILE = 512


def _proj_kernel(x_ref, w_ref, *o_refs):
    xb = x_ref[...].astype(BF16)
    off = 0
    for o_ref in o_refs:
        n = o_ref.shape[1]
        o_ref[...] = jnp.dot(xb, w_ref[:, off:off + n], preferred_element_type=F32)
        off += n


def _proj(x, w, widths):
    M, K = x.shape
    tm = ROW_TILE
    assert M % tm == 0 and sum(widths) == w.shape[1] and all(n % LANES == 0 for n in widths)
    return pl.pallas_call(
        _proj_kernel,
        grid=(M // tm,),
        in_specs=[pl.BlockSpec((tm, K), lambda i: (i, 0)),
                  pl.BlockSpec(w.shape, lambda i: (0, 0), pipeline_mode=pl.Buffered(1))],
        out_specs=[pl.BlockSpec((tm, n), lambda i: (i, 0)) for n in widths],
        out_shape=[jax.ShapeDtypeStruct((M, n), F32) for n in widths],
        compiler_params=pltpu.CompilerParams(dimension_semantics=("parallel",),
                                             vmem_limit_bytes=VMEM_LIMIT),
        name="proj",
    )(x, w.astype(BF16))


def _pad_to(w, n, axis):
    pad = [(0, 0)] * w.ndim
    pad[axis] = (0, n - w.shape[axis])
    return jnp.pad(w, pad)


def _ln_kernel(x_ref, g_ref, b_ref, o_ref):
    o_ref[...] = _layer_norm(x_ref[...], g_ref[...], b_ref[...])


def _ln_rows(x, g, b):
    M, D = x.shape
    row = pl.BlockSpec((ROW_TILE, D), lambda i: (i, 0))
    vec = pl.BlockSpec((1, D), lambda i: (0, 0))
    return pl.pallas_call(
        _ln_kernel, grid=(M // ROW_TILE,), in_specs=[row, vec, vec], out_specs=row,
        out_shape=jax.ShapeDtypeStruct((M, D), F32),
        compiler_params=pltpu.CompilerParams(dimension_semantics=("parallel",)),
        name="ln_in",
    )(x, g.reshape(1, D), b.reshape(1, D))


def _halo_specs(ts, width, seq_len):
    per = ts // 8
    last = seq_len // 8 - 1
    cur = pl.BlockSpec((1, ts, width), lambda b, i: (b, i, 0))
    prev = pl.BlockSpec((1, 8, width), lambda b, i: (b, jnp.maximum(i * per - 1, 0), 0))
    nxt = pl.BlockSpec((1, 8, width), lambda b, i: (b, jnp.minimum((i + 1) * per, last), 0))
    return [cur, prev, nxt]


def _neighbours(x, prev_blk, next_blk):
    ts = x.shape[0]
    i = pl.program_id(1)
    rid = lax.broadcasted_iota(jnp.int32, x.shape, 0)
    prev_row = jnp.where(i > 0, prev_blk[7:8, :], 0.0)
    next_row = jnp.where(i < pl.num_programs(1) - 1, next_blk[0:1, :], 0.0)
    zp = jnp.where(rid == 0, prev_row, pltpu.roll(x, 1, 0))
    zn = jnp.where(rid == ts - 1, next_row, pltpu.roll(x, ts - 1, 0))
    return zp, zn


def _shortconv_kernel(p_ref, pp_ref, pn_ref, w_ref, b_ref, v_ref, x1_ref, x2_ref):
    p = p_ref[0]
    zp, zn = _neighbours(p, pp_ref[0], pn_ref[0])
    q = w_ref[0:1, :] * zp + w_ref[1:2, :] * p + w_ref[2:3, :] * zn + b_ref[...]
    d = v_ref.shape[2]
    v_ref[0] = q[:, :d]
    x1_ref[0] = q[:, d:2 * d]
    x2_ref[0] = q[:, 2 * d:]


def _shortconv(p, conv_w, conv_b):
    bsz, seq_len, width = p.shape
    d = width // 3
    ts = ROW_TILE
    out = pl.BlockSpec((1, ts, d), lambda b, i: (b, i, 0))
    return pl.pallas_call(
        _shortconv_kernel,
        grid=(bsz, seq_len // ts),
        in_specs=_halo_specs(ts, width, seq_len) + [pl.BlockSpec((3, width), lambda b, i: (0, 0)),
                                                     pl.BlockSpec((1, width), lambda b, i: (0, 0))],
        out_specs=[out, out, out],
        out_shape=[jax.ShapeDtypeStruct((bsz, seq_len, d), F32)] * 3,
        compiler_params=pltpu.CompilerParams(dimension_semantics=("parallel", "parallel")),
        name="hyena_shortconv",
    )(p, p, p, conv_w, conv_b.reshape(1, width))


def _head_sum(x, bd):
    hi, lo = _split(x)
    return _dg(hi, bd, _NN) + _dg(lo, bd, _NN)


def _softplus(z):
    return jnp.maximum(z, 0.0) + jnp.log(1.0 + jnp.exp(-jnp.abs(z)))


def _rwkv_prep_kernel(*refs, has_res):
    (p_ref, pp_ref, pn_ref, q_ref, qp_ref, qn_ref) = refs[:6]
    refs = refs[6:]
    if has_res:
        vf_ref, refs = refs[0], refs[1:]
    (mu_ref, mul_ref, w2_ref, w0_ref, a2_ref, a0_ref, g2_ref, kk_ref, v0_ref, v1_ref, v2_ref, bd_ref,
     r_out, k_out, v_out, kn_out, lw_out, a_out, g_out) = refs
    d = r_out.shape[2]
    p = p_ref[0]
    zp, zn = _neighbours(p, pp_ref[0], pn_ref[0])
    p = p + mu_ref[0:1, :] * (zp - p) + mu_ref[1:2, :] * (zn - p)
    q = q_ref[0]
    zp, zn = _neighbours(q, qp_ref[0], qn_ref[0])
    q = q + mul_ref[0:1, :] * (zp - q) + mul_ref[1:2, :] * (zn - q)
    r, k, v = p[:, :d], p[:, d:2 * d], p[:, 2 * d:]
    if has_res:
        mix = _dg(_dg(v.astype(BF16), v1_ref[...], _NN).astype(BF16), v2_ref[...], _NN)
        v = v + (vf_ref[0] - v) * jax.nn.sigmoid(v0_ref[...] + mix)
    wd, ad, gd = q[:, :LANES], q[:, LANES:2 * LANES], q[:, 2 * LANES:]
    wpre = w0_ref[...] + _dg(jnp.tanh(wd).astype(BF16), w2_ref[...], _NN)
    lw = -jnp.exp(-_softplus(-wpre) - 0.5)
    a = jax.nn.sigmoid(a0_ref[...] + _dg(ad.astype(BF16), a2_ref[...], _NN))
    g = _dg(jax.nn.sigmoid(gd).astype(BF16), g2_ref[...], _NN)
    kq = k * kk_ref[...]
    norm = jnp.sqrt(_head_sum(kq * kq, bd_ref[...]))
    r_out[0] = r
    k_out[0] = k
    v_out[0] = v
    kn_out[0] = kq / jnp.maximum(norm, 1e-12)
    lw_out[0, 0] = lw[:, :d]
    lw_out[0, 1] = lw[:, d:]
    a_out[0, 0] = a[:, :d]
    a_out[0, 1] = a[:, d:]
    g_out[0] = g


def _block_diag2(m):
    z = jnp.zeros_like(m[0])
    return jnp.concatenate([jnp.concatenate([m[0], z], axis=1), jnp.concatenate([z, m[1]], axis=1)], axis=0)


def _head_ones(d):
    i = np.arange(d) // RW_HEAD
    return jnp.asarray(i[:, None] == i[None, :], BF16)


def _rwkv_prep(p_rkv, p_lora, v_first, v_res, mu_rkv, mu_lora, w0, w2, a0, a2, g2, k_k):
    bsz, seq_len, w3 = p_rkv.shape
    d = w3 // 3
    wl = p_lora.shape[2]
    ts = ROW_TILE
    assert 2 * w2.shape[1] == LANES and 2 * a2.shape[1] == LANES
    has_res = v_res is not None
    const = lambda shape: pl.BlockSpec(shape, lambda b, i: (0,) * len(shape))
    tile = pl.BlockSpec((1, ts, d), lambda b, i: (b, i, 0))
    pair = pl.BlockSpec((1, 2, ts, d), lambda b, i: (b, 0, i, 0))
    if has_res:
        v0, v1, v2 = v_res
        rank = _round_up(v1.shape[1], LANES)
        v0, v1, v2 = v0.reshape(1, d), _pad_to(v1, rank, 1).astype(BF16), _pad_to(v2, rank, 0).astype(BF16)
    else:
        v0, v1, v2 = jnp.zeros((1, d), F32), jnp.zeros((d, LANES), BF16), jnp.zeros((LANES, d), BF16)
    params = [mu_rkv, mu_lora, _block_diag2(w2).astype(BF16), w0.reshape(1, 2 * d),
              _block_diag2(a2).astype(BF16), a0.reshape(1, 2 * d),
              _pad_to(g2, wl - 2 * LANES, 0).astype(BF16), k_k.reshape(1, d), v0, v1, v2, _head_ones(d)]
    args = [p_rkv, p_rkv, p_rkv, p_lora, p_lora, p_lora] + ([v_first] if has_res else []) + params
    tok = jax.ShapeDtypeStruct((bsz, seq_len, d), F32)
    two = jax.ShapeDtypeStruct((bsz, 2, seq_len, d), F32)
    return pl.pallas_call(
        functools.partial(_rwkv_prep_kernel, has_res=has_res),
        grid=(bsz, seq_len // ts),
        in_specs=(_halo_specs(ts, w3, seq_len) + _halo_specs(ts, wl, seq_len) + ([tile] if has_res else [])
                  + [const(x.shape) for x in params]),
        out_specs=[tile, tile, tile, tile, pair, pair, tile],
        out_shape=[tok, tok, tok, tok, two, two, tok],
        compiler_params=pltpu.CompilerParams(dimension_semantics=("parallel", "parallel"),
                                             vmem_limit_bytes=VMEM_LIMIT),
        name="rwkv_prep",
    )(*args)


def _mix_kernel(yf_ref, yb_ref, r_ref, k_ref, v_ref, a_ref, g_ref, yh_ref, pg_ref, h_ref,
                ka_ref, rk_ref, lg_ref, lb_ref, bd_ref, why_ref, wrw_ref, wo_ref, n1g_ref, n1b_ref,
                wr_ref, br_ref, h_out, aff_out, *, alpha, n_exp):
    d_model = h_ref.shape[2]
    bd = bd_ref[...]
    inv = 1.0 / RW_HEAD
    y = yf_ref[0] + yb_ref[0]
    mu = _head_sum(y, bd) * inv
    dy = y - mu
    var = _head_sum(dy * dy, bd) * inv
    yn = dy * lax.rsqrt(var + GN_EPS) * lg_ref[...] + lb_ref[...]
    k = k_ref[0]
    ka = ka_ref[...]
    k_sum = k * (1.0 + (a_ref[0, 0] - 1.0) * ka) + k * (1.0 + (a_ref[0, 1] - 1.0) * ka)
    bonus = _head_sum(r_ref[0] * k_sum * rk_ref[...], bd) * v_ref[0]
    y_rw = (yn + bonus) * g_ref[0]
    o_hy = _dg(yh_ref[0].astype(BF16), why_ref[...], _NN)
    o_rw = _dg(y_rw.astype(BF16), wrw_ref[...], _NN)
    gates = jax.nn.sigmoid(pg_ref[0])
    merged = gates[:, :d_model] * o_hy + gates[:, d_model:] * o_rw
    mo = _dg(merged.astype(BF16), wo_ref[...], _NN)
    h1 = _layer_norm(alpha * h_ref[0] + mo, n1g_ref[...], n1b_ref[...])
    h_out[0] = h1
    logits = _dg(h1.astype(BF16), wr_ref[...], _NN) + br_ref[...]
    col = lax.broadcasted_iota(jnp.int32, logits.shape, 1)
    logits = jnp.where(col < n_exp, logits, -1e30)
    ex = jnp.exp(logits - jnp.max(logits, axis=1, keepdims=True))
    aff = ex / jnp.sum(ex, axis=1, keepdims=True)
    aff_out[0] = jnp.transpose(aff)[:n_exp, :]


def _mix(y_f, y_b, r, k, v, a, g, y_hy, p_gate, h, k_a, r_k, lnx_g, lnx_b, w_o_hy, w_o_rw, w_out, ln_g, ln_b,
         w_router, b_router, alpha):
    bsz, seq_len, d = r.shape
    d_model = h.shape[2]
    n_exp = w_router.shape[1]
    ts = MIX_TILE
    tile = lambda w: pl.BlockSpec((1, ts, w), lambda b, i: (b, i, 0))
    const = lambda shape: pl.BlockSpec(shape, lambda b, i: (0,) * len(shape))
    vec = lambda x: x.reshape(1, -1)
    params = [vec(k_a), vec(r_k), vec(lnx_g), vec(lnx_b), _head_ones(d), w_o_hy.astype(BF16),
              w_o_rw.astype(BF16), w_out.astype(BF16), vec(ln_g), vec(ln_b),
              _pad_to(w_router, LANES, 1).astype(BF16), _pad_to(vec(b_router), LANES, 1)]
    return pl.pallas_call(
        functools.partial(_mix_kernel, alpha=alpha, n_exp=n_exp),
        grid=(bsz, seq_len // ts),
        in_specs=[tile(d), tile(d),
                  tile(d), tile(d), tile(d), pl.BlockSpec((1, 2, ts, d), lambda b, i: (b, 0, i, 0)),
                  tile(d), tile(d), tile(2 * d_model), tile(d_model)] + [const(x.shape) for x in params],
        out_specs=[tile(d_model), pl.BlockSpec((1, n_exp, ts), lambda b, i: (b, 0, i))],
        out_shape=[jax.ShapeDtypeStruct((bsz, seq_len, d_model), F32),
                   jax.ShapeDtypeStruct((bsz, n_exp, seq_len), F32)],
        compiler_params=pltpu.CompilerParams(dimension_semantics=("parallel", "parallel"),
                                             vmem_limit_bytes=VMEM_LIMIT),
        name="mix_out",
    )(y_f, y_b, r, k, v, a, g, y_hy, p_gate, h, *params)


LN_EPS = 1e-5
GN_EPS = 64e-5
HY_ORDER = 2
HY_TARGET = 1e-2
HY_SHORT_DECAY_PCT = 0.3
HY_LONG_DECAY_PCT = 1.5
N_EXPERTS = 16
EC_CAPACITY = 2


def _layer_norm(x, g, b):
    mu = jnp.mean(x, -1, keepdims=True)
    var = jnp.mean(jnp.square(x - mu), -1, keepdims=True)
    return (x - mu) * lax.rsqrt(var + LN_EPS) * g + b


def _hyena_filters(seq_len, w1, b1, w2, b2, w3, freq, d_hy):
    emb = w1.shape[0]
    bands = (emb - 1) // 2
    t = jnp.linspace(0.0, 1.0, seq_len, dtype=F32)[:, None]
    w = 2.0 * math.pi * jnp.arange(seq_len, dtype=F32)[:, None] / seq_len
    f = jnp.linspace(1e-4, bands - 1, bands, dtype=F32)[None, :]
    z = jnp.concatenate([t, jnp.cos(f * w), -jnp.sin(f * w)], axis=-1)
    a = jnp.sin(freq * (z @ w1 + b1))
    for i in range(w2.shape[0]):
        a = jnp.sin(freq * (a @ w2[i] + b2[i]))
    filt = (a @ w3).reshape(seq_len, HY_ORDER, 2, d_hy)
    max_decay = math.log(HY_TARGET) / HY_SHORT_DECAY_PCT
    min_decay = math.log(HY_TARGET) / HY_LONG_DECAY_PCT
    deltas = jnp.linspace(min_decay, max_decay, d_hy, dtype=F32)
    filt = filt * jnp.exp(-t * jnp.abs(deltas))[:, None, None, :]
    filt = filt * lax.rsqrt(jnp.sum(filt * filt, axis=(0, 2), keepdims=True) + 1e-12)
    return filt


def _hyena_branch(p, conv_w, conv_b, filt, bias):
    seq_len = p.shape[1]
    v, x1, x2 = _shortconv(p, conv_w, conv_b)
    z = _long_conv_gated(v, x1, *_filter_spectrum(filt[:, 0], seq_len), bias[0])
    return _long_conv_gated(z, x2, *_filter_spectrum(filt[:, 1], seq_len), bias[1])


def kernel(x, ln0_g, ln0_b, w_in, hy_conv_w, hy_conv_b, hy_ffn_w1, hy_ffn_b1, hy_ffn_w2, hy_ffn_b2,
           hy_ffn_w3, hy_freq, hy_bias, rw_mu, rw_w0, rw_w2, rw_a0, rw_a2, rw_v0, rw_v1, rw_v2, rw_g2,
           rw_k_k, rw_k_a, rw_r_k, rw_lnx_g, rw_lnx_b, w_o_hy, w_o_rw, w_out, ln1_g, ln1_b,
           w_router, b_router, w_gate, w_up, w_down, ln2_g, ln2_b):
    bsz, seq_len, d_model = x.shape
    depth = w_in.shape[0]
    d_hy = hy_bias.shape[-1]
    d_rw = rw_w0.shape[-1]
    p_hy = 3 * d_hy
    p_rw = rw_mu.shape[-1]
    lora = p_rw - 3 * d_rw
    lora_pad = _round_up(lora, LANES)
    widths = (p_hy, 3 * d_rw, lora_pad, 2 * d_model)
    alpha = (2 * depth) ** 0.25
    cap = EC_CAPACITY * seq_len // N_EXPERTS
    tokens = bsz * seq_len
    h = _ln_rows(x.reshape(tokens, d_model), ln0_g, ln0_b).reshape(bsz, seq_len, d_model)
    v_first = None
    for l in range(depth):
        c0, c1, c2 = p_hy, p_hy + 3 * d_rw, p_hy + p_rw
        w = jnp.concatenate([w_in[l][:, :c1], _pad_to(w_in[l][:, c1:c2], lora_pad, 1), w_in[l][:, c2:]], axis=1)
        pa, pb, pl_, pg = [t.reshape(bsz, seq_len, -1) for t in _proj(h.reshape(tokens, d_model), w, widths)]
        filt = _hyena_filters(seq_len, hy_ffn_w1[l], hy_ffn_b1[l], hy_ffn_w2[l], hy_ffn_b2[l],
                              hy_ffn_w3[l], hy_freq[l], d_hy)
        y_hy = _hyena_branch(pa, hy_conv_w[l], hy_conv_b[l], filt, hy_bias[l])
        v_res = None if l == 0 else (rw_v0[l - 1], rw_v1[l - 1], rw_v2[l - 1])
        mu = rw_mu[l]
        r, k, v, kk, lw, a, g = _rwkv_prep(pb, pl_, v_first, v_res, mu[:, :3 * d_rw],
                                           _pad_to(mu[:, 3 * d_rw:], lora_pad, 1), rw_w0[l], rw_w2[l],
                                           rw_a0[l], rw_a2[l], rw_g2[l], rw_k_k[l])
        if l == 0:
            v_first = v
        y_f, y_b = wkv7(r, k, v, kk, lw, a, rw_k_a[l])
        h, aff_t = _mix(y_f, y_b, r, k, v, a, g, y_hy, pg, h, rw_k_a[l], rw_r_k[l], rw_lnx_g[l], rw_lnx_b[l],
                        w_o_hy[l], w_o_rw[l], w_out[l], ln1_g[l], ln1_b[l], w_router[l], b_router[l], alpha)
        idx, gate = _route(aff_t, cap)
        h = _moe_block(h, idx, gate, w_gate[l].astype(BF16), w_up[l].astype(BF16), w_down[l].astype(BF16),
                       ln2_g[l], ln2_b[l], alpha)
    return h
```

```python
import functools
import math

import jax
import jax.numpy as jnp
import numpy as np
from jax import lax
from jax.experimental import pallas as pl
from jax.experimental.pallas import tpu as pltpu

F32 = jnp.float32
BF16 = jnp.bfloat16

RW_HEAD = 64
LANES = 128
WKV_CHUNK = 64
WKV_CHUNKS_PER_STEP = 4
VMEM_LIMIT = 56 * 1024 * 1024


def _split(x):
    hi = x.astype(BF16)
    lo = (x - hi.astype(F32)).astype(BF16)
    return hi, lo


def _dg(a, b, dims):
    return lax.dot_general(a, b, (dims, ((), ())), preferred_element_type=F32)


def _mm(a, b, dims=((1,), (0,)), passes=1):
    if passes == 1:
        return _dg(a.astype(BF16), b.astype(BF16), dims)
    ah, al = _split(a)
    bh, bl = _split(b)
    return _dg(ah, bh, dims) + (_dg(ah, bl, dims) + _dg(al, bh, dims))


_NN = ((1,), (0,))
_NT = ((1,), (1,))
_TN = ((0,), (0,))


def _wkv_operands(r, k, v, kk, lw, a, ka, reverse):
    C = WKV_CHUNK
    row = lax.broadcasted_iota(jnp.int32, (C, C), 0)
    col = lax.broadcasted_iota(jnp.int32, (C, C), 1)
    incl = (col >= row) if reverse else (col <= row)

    lw_hi, lw_lo = _split(lw)
    tri = incl.astype(BF16)
    cin = _dg(tri, lw_hi, _NN) + _dg(tri, lw_lo, _NN)
    tot = jnp.sum(lw, axis=0, keepdims=True)
    half = 0.5 * tot
    e0 = jnp.exp(half)
    e1 = jnp.exp(cin - half)
    e2 = jnp.exp(half - cin)
    ew = jnp.exp(-lw)
    kdir = k * (1.0 + (a - 1.0) * ka)
    rt = r * e1
    bt = kk * (e1 * ew)
    kt = kdir * e2
    at = -(kk * a) * e2
    r0 = rt * e0
    b0 = bt * e0
    kh = kt * e0
    ah = at * e0
    return dict(bt=bt, rt=rt, kt=kt, at=at, b0=b0, r0=r0, kh=kh, ah=ah, e0sq=e0 * e0, v=v)


def _wkv_kernel(rf_ref, kf_ref, vf_ref, nf_ref, rb_ref, kb_ref, vb_ref, nb_ref, lwf_ref, lwb_ref,
                af_ref, ab_ref, ka_ref, yf_ref, yb_ref, h_ref):
    C = WKV_CHUNK

    @pl.when(pl.program_id(1) == 0)
    def _():
        h_ref[...] = jnp.zeros_like(h_ref)

    ka = ka_ref[...]
    n_pairs = ka.shape[1] // LANES
    outs = [yf_ref, yb_ref]

    row2 = lax.broadcasted_iota(jnp.int32, (C, 2 * C), 0)
    col2 = lax.broadcasted_iota(jnp.int32, (C, 2 * C), 1) % C
    earlier2 = [col2 < row2, col2 > row2]
    incl2 = [col2 <= row2, col2 >= row2]
    lane = lax.broadcasted_iota(jnp.int32, (1, LANES), 1)
    first = lane < RW_HEAD
    ri = lax.broadcasted_iota(jnp.int32, (LANES, LANES), 0)
    ci = lax.broadcasted_iota(jnp.int32, (LANES, LANES), 1)
    eye = ri == ci
    same_head = (ri // RW_HEAD) == (ci // RW_HEAD)

    def stack2(x):
        return jnp.concatenate([jnp.where(first, x, 0.0), jnp.where(first, 0.0, x)], axis=0)

    chains = [(d, p) for d in range(2) for p in range(n_pairs)]
    n = range(len(chains))
    hp = [h_ref[q] for q in n]
    n_sub = rf_ref.shape[1] // C
    for s in range(n_sub):
        fs = slice(s * C, (s + 1) * C)
        bs = slice((n_sub - 1 - s) * C, (n_sub - s) * C)
        rows = [fs, bs]
        ops = [_wkv_operands(rf_ref[0, fs, :], kf_ref[0, fs, :], vf_ref[0, fs, :], nf_ref[0, fs, :],
                             lwf_ref[0, 0, fs, :], af_ref[0, 0, fs, :], ka, False),
               _wkv_operands(rb_ref[0, bs, :], kb_ref[0, bs, :], vb_ref[0, bs, :], nb_ref[0, bs, :],
                             lwb_ref[0, 0, bs, :], ab_ref[0, 0, bs, :], ka, True)]

        def op(name, q):
            d, p = chains[q]
            return ops[d][name][:, p * LANES:(p + 1) * LANES]

        g = [_mm(jnp.concatenate([op("bt", q), op("rt", q)], axis=0),
                 jnp.concatenate([stack2(op("kt", q)), stack2(op("at", q))], axis=0), _NT) for q in n]
        bh = [_mm(jnp.concatenate([op("b0", q), op("r0", q)], axis=0), hp[q]) for q in n]
        v2 = [stack2(op("v", q)) for q in n]
        x = [bh[q][:C] + _mm(jnp.where(earlier2[chains[q][0]], g[q][:C, :2 * C], 0.0), v2[q]) for q in n]
        pw = [jnp.where(earlier2[chains[q][0]], g[q][:C, 2 * C:], 0.0) for q in n]
        n_steps = C.bit_length() - 1
        for i in range(n_steps):
            x = [x[q] + _mm(pw[q], stack2(x[q])) for q in n]
            if i + 1 < n_steps:
                pw = [_mm(pw[q], stack2(pw[q])) for q in n]
        for q in n:
            d, p = chains[q]
            a_r = jnp.concatenate([jnp.where(incl2[d], g[q][C:, :2 * C], 0.0),
                                   jnp.where(incl2[d], g[q][C:, 2 * C:], 0.0)], axis=1)
            outs[d][0, rows[d], p * LANES:(p + 1) * LANES] = bh[q][C:] + _mm(
                a_r, jnp.concatenate([v2[q], stack2(x[q])], axis=0))
        new = []
        for q in n:
            dg = jnp.where(eye, jnp.broadcast_to(op("e0sq", q), (LANES, LANES)), 0.0)
            hn = _mm(jnp.concatenate([op("kh", q), op("ah", q), dg], axis=0),
                     jnp.concatenate([op("v", q), x[q], hp[q]], axis=0), _TN)
            new.append(jnp.where(same_head, hn, 0.0))
        hp = new
    for q in n:
        h_ref[q] = hp[q]


def wkv7(r, k, v, kk, lw, a, k_a):
    B, S, D = r.shape
    C = WKV_CHUNK * WKV_CHUNKS_PER_STEP
    nc = S // C
    assert S % C == 0 and D % LANES == 0
    fwd = pl.BlockSpec((1, C, D), lambda b, c: (b, c, 0))
    bwd = pl.BlockSpec((1, C, D), lambda b, c: (b, nc - 1 - c, 0))
    fwd2 = pl.BlockSpec((1, 1, C, D), lambda b, c: (b, 0, c, 0))
    bwd2 = pl.BlockSpec((1, 1, C, D), lambda b, c: (b, 1, nc - 1 - c, 0))
    out = jax.ShapeDtypeStruct((B, S, D), F32)
    return pl.pallas_call(
        _wkv_kernel,
        grid=(B, nc),
        in_specs=[fwd] * 4 + [bwd] * 4 + [fwd2, bwd2, fwd2, bwd2, pl.BlockSpec((1, D), lambda b, c: (0, 0))],
        out_specs=[fwd, bwd],
        out_shape=[out, out],
        scratch_shapes=[pltpu.VMEM((2 * D // LANES, LANES, LANES), F32)],
        compiler_params=pltpu.CompilerParams(dimension_semantics=("parallel", "arbitrary")),
        name="wkv7_chunked",
    )(r, k, v, kk, r, k, v, kk, lw, lw, a, a, k_a.reshape(1, D))


FFT_N2 = 64
FFT_K1_BLOCK = 8


def _round_up(n, m):
    return (n + m - 1) // m * m


def _twice(m, axis):
    hi = jnp.asarray(m, F32).astype(BF16)
    return jnp.concatenate([hi, hi], axis=axis)


def _hilo(x):
    hi, lo = _split(x)
    return jnp.concatenate([hi, lo], axis=0)


@functools.lru_cache(maxsize=None)
def _fft_tables(seq_len):
    n = 2 * seq_len
    n2 = FFT_N2
    n1 = n // n2
    h1 = n1 // 2
    k1n = h1 + 1
    k1p = _round_up(k1n, FFT_K1_BLOCK)
    k1 = np.arange(k1n)[:, None]
    th1 = 2 * np.pi * k1 * np.arange(h1)[None, :] / n1
    f1 = np.zeros((2 * k1p, h1))
    f1[:k1n] = np.cos(th1)
    f1[k1p:k1p + k1n] = -np.sin(th1)
    m = np.arange(n2)
    th2 = 2 * np.pi * (m[None, None, :] * m[None, :, None] / n2 + m[None, None, :] * np.arange(k1n)[:, None, None] / n)
    gr, gi = np.cos(th2), -np.sin(th2)
    g = np.zeros((k1p, 2 * n2, 2 * n2))
    g[:k1n] = np.block([[gr, -gi], [gi, gr]])
    gt = np.transpose(g, (0, 2, 1))
    coef = np.full((k1n,), 2.0)
    coef[0] = 1.0
    coef[-1] = 1.0
    th3 = th1.T
    m3 = np.zeros((h1, 2 * k1p))
    m3[:, :k1n] = coef * np.cos(th3) / n
    m3[:, k1p:k1p + k1n] = -coef * np.sin(th3) / n
    th1f = 2 * np.pi * k1 * np.arange(n1)[None, :] / n1
    f1_full = np.zeros((2 * k1p, n1))
    f1_full[:k1n] = np.cos(th1f)
    f1_full[k1p:k1p + k1n] = -np.sin(th1f)
    return dict(n1=n1, h1=h1, k1n=k1n, k1p=k1p, f1=f1, g=g, gt=gt, m3=m3, f1_full=f1_full)


FFT_N2_BLOCK = 8


def _rows_to_lanes(ref):
    return jnp.concatenate([ref[0, :, i, :] for i in range(FFT_N2_BLOCK)], axis=1)


def _lanes_to_rows(ref, val):
    ch = ref.shape[3]
    for i in range(FFT_N2_BLOCK):
        ref[0, :, i, :] = val[:, i * ch:(i + 1) * ch]


def _fft_s1_kernel(f_ref, x_ref, o_ref):
    _lanes_to_rows(o_ref, jnp.dot(f_ref[...], _hilo(_rows_to_lanes(x_ref)), preferred_element_type=F32))


def _fft_mid_kernel(g_ref, gt_ref, kr_ref, ki_ref, a_ref, o_ref):
    n2 = FFT_N2
    for j in range(FFT_K1_BLOCK):
        xin = jnp.concatenate([a_ref[0, 0, j], a_ref[0, 1, j]], axis=0)
        z = jnp.dot(g_ref[j], _hilo(xin), preferred_element_type=F32)
        zr, zi = z[:n2], z[n2:]
        kr, ki = kr_ref[j], ki_ref[j]
        y = jnp.concatenate([zr * kr - zi * ki, zr * ki + zi * kr], axis=0)
        b = jnp.dot(gt_ref[j], _hilo(y), preferred_element_type=F32)
        o_ref[0, 0, j] = b[:n2]
        o_ref[0, 1, j] = b[n2:]


def _fft_fwd_kernel(g_ref, a_ref, re_ref, im_ref):
    n2 = FFT_N2
    for j in range(FFT_K1_BLOCK):
        xin = jnp.concatenate([a_ref[0, 0, j], a_ref[0, 1, j]], axis=0)
        z = jnp.dot(g_ref[j], _hilo(xin), preferred_element_type=F32)
        re_ref[j] = z[:n2]
        im_ref[j] = z[n2:]


def _fft_s3_kernel(m_ref, b_ref, u_ref, x_ref, bias_ref, o_ref):
    y = jnp.dot(m_ref[...], _hilo(_rows_to_lanes(b_ref)), preferred_element_type=F32)
    _lanes_to_rows(o_ref, _rows_to_lanes(x_ref) * (y + _rows_to_lanes(u_ref) * bias_ref[...]))


def _long_conv_gated(u, gate, kf_re, kf_im, bias):
    bsz, seq_len, ch = u.shape
    t = _fft_tables(seq_len)
    n2, h1, k1p = FFT_N2, t["h1"], t["k1p"]
    lanes = n2 * ch
    nb = FFT_N2_BLOCK
    lt = nb * ch
    assert seq_len == h1 * n2 and ch % LANES == 0
    params = pltpu.CompilerParams(dimension_semantics=("parallel", "parallel"),
                                  vmem_limit_bytes=VMEM_LIMIT)
    spec4 = pl.BlockSpec((1, 2 * k1p, nb, ch), lambda b, j: (b, 0, j, 0))
    row = pl.BlockSpec((1, h1, nb, ch), lambda b, j: (b, 0, j, 0))
    f1 = _twice(t["f1"], 1)
    a = pl.pallas_call(
        _fft_s1_kernel,
        grid=(bsz, n2 // nb),
        in_specs=[pl.BlockSpec(f1.shape, lambda b, j: (0, 0)), row],
        out_specs=spec4,
        out_shape=jax.ShapeDtypeStruct((bsz, 2 * k1p, n2, ch), F32),
        compiler_params=params, name="hyena_dft1",
    )(f1, u.reshape(bsz, h1, n2, ch))
    kb = FFT_K1_BLOCK
    g = _twice(t["g"], 2)
    gt = _twice(t["gt"], 2)
    mat_spec = pl.BlockSpec((kb,) + g.shape[1:], lambda b, j: (j, 0, 0))
    kf_spec = pl.BlockSpec((kb, n2, ch), lambda b, j: (j, 0, 0))
    blk = pl.BlockSpec((1, 2, kb, n2, ch), lambda b, j: (b, 0, j, 0, 0))
    bm = pl.pallas_call(
        _fft_mid_kernel,
        grid=(bsz, k1p // kb),
        in_specs=[mat_spec, mat_spec, kf_spec, kf_spec, blk],
        out_specs=blk,
        out_shape=jax.ShapeDtypeStruct((bsz, 2, k1p, n2, ch), F32),
        compiler_params=params, name="hyena_dft2",
    )(g, gt, kf_re, kf_im, a.reshape(bsz, 2, k1p, n2, ch))
    m3 = _twice(t["m3"], 1)
    out = pl.pallas_call(
        _fft_s3_kernel,
        grid=(bsz, n2 // nb),
        in_specs=[pl.BlockSpec(m3.shape, lambda b, j: (0, 0)), spec4,
                  row, row, pl.BlockSpec((1, lt), lambda b, j: (0, j))],
        out_specs=row,
        out_shape=jax.ShapeDtypeStruct((bsz, h1, n2, ch), F32),
        compiler_params=params, name="hyena_dft3",
    )(m3, bm.reshape(bsz, 2 * k1p, n2, ch), u.reshape(bsz, h1, n2, ch), gate.reshape(bsz, h1, n2, ch),
      jnp.tile(bias, n2).reshape(1, lanes))
    return out.reshape(bsz, seq_len, ch)


def _filter_spectrum(filt_o, seq_len):
    t = _fft_tables(seq_len)
    n1, n2, k1p = t["n1"], FFT_N2, t["k1p"]
    nb, kb = FFT_N2_BLOCK, FFT_K1_BLOCK
    ch = filt_o.shape[-1]
    k_full = jnp.concatenate([filt_o[:, 0], filt_o[::-1, 1]], axis=0)
    f1 = _twice(t["f1_full"], 1)
    params = pltpu.CompilerParams(dimension_semantics=("parallel", "parallel"),
                                  vmem_limit_bytes=VMEM_LIMIT)
    a = pl.pallas_call(
        _fft_s1_kernel,
        grid=(1, n2 // nb),
        in_specs=[pl.BlockSpec(f1.shape, lambda b, j: (0, 0)),
                  pl.BlockSpec((1, n1, nb, ch), lambda b, j: (b, 0, j, 0))],
        out_specs=pl.BlockSpec((1, 2 * k1p, nb, ch), lambda b, j: (b, 0, j, 0)),
        out_shape=jax.ShapeDtypeStruct((1, 2 * k1p, n2, ch), F32),
        compiler_params=params, name="filter_dft1",
    )(f1, k_full.reshape(1, n1, n2, ch))
    g = _twice(t["g"], 2)
    kf_spec = pl.BlockSpec((kb, n2, ch), lambda b, j: (j, 0, 0))
    out = jax.ShapeDtypeStruct((k1p, n2, ch), F32)
    return pl.pallas_call(
        _fft_fwd_kernel,
        grid=(1, k1p // kb),
        in_specs=[pl.BlockSpec((kb,) + g.shape[1:], lambda b, j: (j, 0, 0)),
                  pl.BlockSpec((1, 2, kb, n2, ch), lambda b, j: (b, 0, j, 0, 0))],
        out_specs=[kf_spec, kf_spec],
        out_shape=[out, out],
        compiler_params=params, name="filter_dft2",
    )(g, a.reshape(1, 2, k1p, n2, ch))


ROUTE_COLS = 16
ROUTE_TOKEN_CHUNK = 512


def _route_kernel(aff_ref, o_ref, key_ref, val_ref, *, cap):
    a = aff_ref[0]
    n_exp, seq_len = a.shape
    bits = pltpu.bitcast(a, jnp.int32)

    def bisect(i, t):
        cand = t | jnp.left_shift(1, 30 - i)
        cnt = jnp.sum((bits >= cand).astype(jnp.int32), axis=1, keepdims=True)
        return jnp.where(cnt >= cap, cand, t)

    thr = lax.fori_loop(0, 31, bisect, jnp.zeros((n_exp, 1), jnp.int32))
    gt = bits > thr
    eq = bits == thr

    ri = lax.broadcasted_iota(jnp.int32, (LANES, LANES), 0)
    ci = lax.broadcasted_iota(jnp.int32, (LANES, LANES), 1)
    upper = (ri < ci).astype(BF16)

    def prefix(mask):
        m = mask.astype(F32)
        run = jnp.zeros((n_exp, 1), F32)
        parts = []
        for j in range(seq_len // LANES):
            tile = m[:, j * LANES:(j + 1) * LANES]
            parts.append(run + jnp.dot(tile.astype(BF16), upper, preferred_element_type=F32))
            run = run + jnp.sum(tile, axis=1, keepdims=True)
        return jnp.concatenate(parts, axis=1), run

    eq_before, _ = prefix(eq)
    n_gt = jnp.sum(gt.astype(F32), axis=1, keepdims=True)
    sel = jnp.logical_or(gt, jnp.logical_and(eq, eq_before < cap - n_gt))
    pos, _ = prefix(sel)
    key_ref[...] = jnp.where(sel, pos, -1.0)
    val_ref[...] = a

    tok = lax.broadcasted_iota(jnp.int32, (1, seq_len), 1)
    t_hi = (tok // 64).astype(F32)
    t_lo = (tok % 64).astype(F32)
    slot = lax.broadcasted_iota(jnp.int32, (cap, ROUTE_TOKEN_CHUNK), 0).astype(F32)
    zeros = jnp.zeros((ROUTE_COLS - 5, seq_len), F32)

    def compact(e, carry):
        key = key_ref[pl.ds(e, 1), :]
        g = val_ref[pl.ds(e, 1), :]
        g_hi = g.astype(BF16).astype(F32)
        g_mid = (g - g_hi).astype(BF16).astype(F32)
        g_lo = g - g_hi - g_mid
        rows = jnp.concatenate([t_hi, t_lo, g_hi, g_mid, g_lo, zeros], axis=0).astype(BF16)
        acc = jnp.zeros((cap, ROUTE_COLS), F32)
        for j in range(seq_len // ROUTE_TOKEN_CHUNK):
            sl = slice(j * ROUTE_TOKEN_CHUNK, (j + 1) * ROUTE_TOKEN_CHUNK)
            onehot = jnp.where(key[:, sl] == slot, 1.0, 0.0).astype(BF16)
            acc = acc + _dg(onehot, rows[:, sl], _NT)
        o_ref[0, pl.ds(e, 1)] = acc[None]
        return carry

    lax.fori_loop(0, n_exp, compact, 0)


def _route(aff_t, cap):
    bsz, n_exp, seq_len = aff_t.shape
    assert seq_len % ROUTE_TOKEN_CHUNK == 0 and seq_len <= 64 * 256
    out = pl.pallas_call(
        functools.partial(_route_kernel, cap=cap),
        grid=(bsz,),
        in_specs=[pl.BlockSpec((1, n_exp, seq_len), lambda b: (b, 0, 0))],
        out_specs=pl.BlockSpec((1, n_exp, cap, ROUTE_COLS), lambda b: (b, 0, 0, 0)),
        out_shape=jax.ShapeDtypeStruct((bsz, n_exp, cap, ROUTE_COLS), F32),
        scratch_shapes=[pltpu.VMEM((n_exp, seq_len), F32), pltpu.VMEM((n_exp, seq_len), F32)],
        compiler_params=pltpu.CompilerParams(dimension_semantics=("parallel",),
                                             vmem_limit_bytes=VMEM_LIMIT),
        name="ec_route",
    )(aff_t)
    idx = (out[..., 0] * 64.0 + out[..., 1]).astype(jnp.int32)
    gate = out[..., 2] + out[..., 3] + out[..., 4]
    return idx, gate


MOE_FF_TILE = 1024
MOE_GROUP = 16
LN_ROWS = 256


def _gather_rows(h_ref, idx_ref, xb_ref, base):
    rows = [h_ref[0, pl.ds(idx_ref[0, 0, base + j], 1), :] for j in range(MOE_GROUP)]
    xb_ref[pl.ds(base, MOE_GROUP), :] = jnp.concatenate(rows, axis=0).astype(BF16)


def _scatter_rows(o_ref, idx_ref, gate_ref, y_ref, base, enabled):
    y8 = y_ref[pl.ds(base, 8), :]
    toks = [idx_ref[0, 0, base + j] for j in range(8)]
    rows = [o_ref[0, pl.ds(toks[j], 1), :] + jnp.where(enabled, gate_ref[0, 0, base + j], 0.0) * y8[j:j + 1, :]
            for j in range(8)]
    for j in range(8):
        o_ref[0, pl.ds(toks[j], 1), :] = rows[j]


def _moe_kernel(idx_ref, gate_ref, idxn_ref, idxp_ref, gatep_ref, h_ref, wg_ref, wu_ref, wd_ref,
                lg_ref, lb_ref, o_ref, xb_ref, y_ref, *, alpha, n_f):
    e = pl.program_id(1)
    f = pl.program_id(2)
    last_e = pl.num_programs(1) - 1
    last_f = n_f - 1
    cap = xb_ref.shape[1]
    seq_len = h_ref.shape[1]
    share = cap // n_f
    slot = e % 2
    other = 1 - slot
    first_step = jnp.logical_and(e == 0, f == 0)

    @pl.when(first_step)
    def _():
        o_ref[...] = jnp.zeros_like(o_ref)
        y_ref[...] = jnp.zeros_like(y_ref)

        def gather(i, carry):
            _gather_rows(h_ref, idx_ref, xb_ref.at[0], pl.multiple_of(i * MOE_GROUP, MOE_GROUP))
            return carry

        lax.fori_loop(0, cap // MOE_GROUP, gather, 0)

    xb = xb_ref[slot]
    a = jnp.dot(xb, wg_ref[0], preferred_element_type=F32)
    u = jnp.dot(xb, wu_ref[0], preferred_element_type=F32)
    act = (a * jax.nn.sigmoid(a) * u).astype(BF16)
    part = jnp.dot(act, wd_ref[0], preferred_element_type=F32)

    for i in range(share // MOE_GROUP):
        _gather_rows(h_ref, idxn_ref, xb_ref.at[other], pl.multiple_of(f * share + i * MOE_GROUP, MOE_GROUP))
    for i in range(share // 8):
        _scatter_rows(o_ref, idxp_ref, gatep_ref, y_ref.at[other], pl.multiple_of(f * share + i * 8, 8), e > 0)

    @pl.when(f == 0)
    def _():
        y_ref[slot] = part

    @pl.when(f > 0)
    def _():
        y_ref[slot] += part

    @pl.when(jnp.logical_and(e == last_e, f == last_f))
    def _():
        def scatter(i, carry):
            _scatter_rows(o_ref, idx_ref, gate_ref, y_ref.at[slot], pl.multiple_of(i * 8, 8), True)
            return carry

        lax.fori_loop(0, cap // 8, scatter, 0)

    @pl.when(jnp.logical_and(e == last_e, f == last_f))
    def _():
        def norm(i, carry):
            rows = pl.ds(pl.multiple_of(i * LN_ROWS, LN_ROWS), LN_ROWS)
            z = alpha * h_ref[0, rows, :] + o_ref[0, rows, :]
            o_ref[0, rows, :] = _layer_norm(z, lg_ref[...], lb_ref[...])
            return carry

        lax.fori_loop(0, seq_len // LN_ROWS, norm, 0)


def _moe_block(h, idx, gate, wg, wu, wd, layer, ln_g, ln_b, alpha):
    bsz, seq_len, d = h.shape
    _, n_exp, _, ff = wg.shape
    cap = idx.shape[-1]
    ft = min(MOE_FF_TILE, ff)
    assert ff % ft == 0 and seq_len % LN_ROWS == 0
    n_f = ff // ft
    assert cap % (n_f * MOE_GROUP) == 0

    def smem(shift):
        def index(b, e, f):
            return (b * n_exp + jnp.clip(e + shift, 0, n_exp - 1), 0, 0)
        return pl.BlockSpec((1, 1, cap), index, memory_space=pltpu.SMEM)

    whole = pl.BlockSpec((1, seq_len, d), lambda b, e, f: (b, 0, 0), pipeline_mode=pl.Buffered(1))
    vec = pl.BlockSpec((1, d), lambda b, e, f: (0, 0))
    idx3 = idx.reshape(bsz * n_exp, 1, cap)
    gate3 = gate.reshape(bsz * n_exp, 1, cap)
    return pl.pallas_call(
        functools.partial(_moe_kernel, alpha=alpha, n_f=n_f),
        grid=(bsz, n_exp, n_f),
        in_specs=[smem(0), smem(0), smem(1), smem(-1), smem(-1), whole,
                  pl.BlockSpec((None, 1, d, ft), lambda b, e, f: (layer, e, 0, f)),
                  pl.BlockSpec((None, 1, d, ft), lambda b, e, f: (layer, e, 0, f)),
                  pl.BlockSpec((None, 1, ft, d), lambda b, e, f: (layer, e, f, 0)),
                  vec, vec],
        out_specs=whole,
        out_shape=jax.ShapeDtypeStruct((bsz, seq_len, d), F32),
        scratch_shapes=[pltpu.VMEM((2, cap, d), BF16), pltpu.VMEM((2, cap, d), F32)],
        compiler_params=pltpu.CompilerParams(
            dimension_semantics=("arbitrary", "arbitrary", "arbitrary"), vmem_limit_bytes=VMEM_LIMIT),
        name="ec_moe",
    )(idx3, gate3, idx3, idx3, gate3, h, wg, wu, wd, ln_g.reshape(1, d), ln_b.reshape(1, d))


ROW_TILE = 512
MIX_TILE = 512


def _proj_kernel(x_ref, w_ref, *o_refs):
    xb = x_ref[...].astype(BF16)
    off = 0
    for o_ref in o_refs:
        n = o_ref.shape[1]
        o_ref[...] = jnp.dot(xb, w_ref[:, off:off + n], preferred_element_type=F32)
        off += n


def _proj(x, w, widths):
    M, K = x.shape
    tm = ROW_TILE
    assert M % tm == 0 and sum(widths) == w.shape[1] and all(n % LANES == 0 for n in widths)
    return pl.pallas_call(
        _proj_kernel,
        grid=(M // tm,),
        in_specs=[pl.BlockSpec((tm, K), lambda i: (i, 0)),
                  pl.BlockSpec(w.shape, lambda i: (0, 0), pipeline_mode=pl.Buffered(1))],
        out_specs=[pl.BlockSpec((tm, n), lambda i: (i, 0)) for n in widths],
        out_shape=[jax.ShapeDtypeStruct((M, n), F32) for n in widths],
        compiler_params=pltpu.CompilerParams(dimension_semantics=("parallel",),
                                             vmem_limit_bytes=VMEM_LIMIT),
        name="proj",
    )(x, w.astype(BF16))


def _pad_to(w, n, axis):
    pad = [(0, 0)] * w.ndim
    pad[axis] = (0, n - w.shape[axis])
    return jnp.pad(w, pad)


def _ln_kernel(x_ref, g_ref, b_ref, o_ref):
    o_ref[...] = _layer_norm(x_ref[...], g_ref[...], b_ref[...])


def _ln_rows(x, g, b):
    M, D = x.shape
    row = pl.BlockSpec((ROW_TILE, D), lambda i: (i, 0))
    vec = pl.BlockSpec((1, D), lambda i: (0, 0))
    return pl.pallas_call(
        _ln_kernel, grid=(M // ROW_TILE,), in_specs=[row, vec, vec], out_specs=row,
        out_shape=jax.ShapeDtypeStruct((M, D), F32),
        compiler_params=pltpu.CompilerParams(dimension_semantics=("parallel",)),
        name="ln_in",
    )(x, g.reshape(1, D), b.reshape(1, D))


def _halo_specs(ts, width, seq_len):
    per = ts // 8
    last = seq_len // 8 - 1
    cur = pl.BlockSpec((1, ts, width), lambda b, i: (b, i, 0))
    prev = pl.BlockSpec((1, 8, width), lambda b, i: (b, jnp.maximum(i * per - 1, 0), 0))
    nxt = pl.BlockSpec((1, 8, width), lambda b, i: (b, jnp.minimum((i + 1) * per, last), 0))
    return [cur, prev, nxt]


def _neighbours(x, prev_blk, next_blk):
    ts = x.shape[0]
    i = pl.program_id(1)
    rid = lax.broadcasted_iota(jnp.int32, x.shape, 0)
    prev_row = jnp.where(i > 0, prev_blk[7:8, :], 0.0)
    next_row = jnp.where(i < pl.num_programs(1) - 1, next_blk[0:1, :], 0.0)
    zp = jnp.where(rid == 0, prev_row, pltpu.roll(x, 1, 0))
    zn = jnp.where(rid == ts - 1, next_row, pltpu.roll(x, ts - 1, 0))
    return zp, zn


def _shortconv_kernel(p_ref, pp_ref, pn_ref, w_ref, b_ref, v_ref, x1_ref, x2_ref):
    p = p_ref[0]
    zp, zn = _neighbours(p, pp_ref[0], pn_ref[0])
    q = w_ref[0:1, :] * zp + w_ref[1:2, :] * p + w_ref[2:3, :] * zn + b_ref[...]
    d = v_ref.shape[2]
    v_ref[0] = q[:, :d]
    x1_ref[0] = q[:, d:2 * d]
    x2_ref[0] = q[:, 2 * d:]


def _shortconv(p, conv_w, conv_b):
    bsz, seq_len, width = p.shape
    d = width // 3
    ts = ROW_TILE
    out = pl.BlockSpec((1, ts, d), lambda b, i: (b, i, 0))
    return pl.pallas_call(
        _shortconv_kernel,
        grid=(bsz, seq_len // ts),
        in_specs=_halo_specs(ts, width, seq_len) + [pl.BlockSpec((3, width), lambda b, i: (0, 0)),
                                                     pl.BlockSpec((1, width), lambda b, i: (0, 0))],
        out_specs=[out, out, out],
        out_shape=[jax.ShapeDtypeStruct((bsz, seq_len, d), F32)] * 3,
        compiler_params=pltpu.CompilerParams(dimension_semantics=("parallel", "parallel")),
        name="hyena_shortconv",
    )(p, p, p, conv_w, conv_b.reshape(1, width))


def _head_sum(x, bd):
    hi, lo = _split(x)
    return _dg(hi, bd, _NN) + _dg(lo, bd, _NN)


def _rwkv_prep_kernel(*refs, has_res):
    (p_ref, pp_ref, pn_ref, q_ref, qp_ref, qn_ref) = refs[:6]
    refs = refs[6:]
    if has_res:
        vf_ref, refs = refs[0], refs[1:]
    (mu_ref, mul_ref, w2_ref, w0_ref, a2_ref, a0_ref, g2_ref, kk_ref, v0_ref, v1_ref, v2_ref, bd_ref,
     r_out, k_out, v_out, kn_out, lw_out, a_out, g_out) = refs
    d = r_out.shape[2]
    p = p_ref[0]
    zp, zn = _neighbours(p, pp_ref[0], pn_ref[0])
    p = p + mu_ref[0:1, :] * (zp - p) + mu_ref[1:2, :] * (zn - p)
    q = q_ref[0]
    zp, zn = _neighbours(q, qp_ref[0], qn_ref[0])
    q = q + mul_ref[0:1, :] * (zp - q) + mul_ref[1:2, :] * (zn - q)
    r, k, v = p[:, :d], p[:, d:2 * d], p[:, 2 * d:]
    if has_res:
        mix = _dg(_dg(v.astype(BF16), v1_ref[...], _NN).astype(BF16), v2_ref[...], _NN)
        v = v + (vf_ref[0] - v) * jax.nn.sigmoid(v0_ref[...] + mix)
    wd, ad, gd = q[:, :LANES], q[:, LANES:2 * LANES], q[:, 2 * LANES:]
    wpre = w0_ref[...] + _dg(jnp.tanh(wd).astype(BF16), w2_ref[...], _NN)
    lw = -math.exp(-0.5) * jax.nn.sigmoid(wpre)
    a = jax.nn.sigmoid(a0_ref[...] + _dg(ad.astype(BF16), a2_ref[...], _NN))
    g = _dg(jax.nn.sigmoid(gd).astype(BF16), g2_ref[...], _NN)
    kq = k * kk_ref[...]
    norm = jnp.sqrt(_head_sum(kq * kq, bd_ref[...]))
    r_out[0] = r
    k_out[0] = k
    v_out[0] = v
    kn_out[0] = kq / jnp.maximum(norm, 1e-12)
    lw_out[0, 0] = lw[:, :d]
    lw_out[0, 1] = lw[:, d:]
    a_out[0, 0] = a[:, :d]
    a_out[0, 1] = a[:, d:]
    g_out[0] = g


def _block_diag2(m):
    z = jnp.zeros_like(m[0])
    return jnp.concatenate([jnp.concatenate([m[0], z], axis=1), jnp.concatenate([z, m[1]], axis=1)], axis=0)


def _head_ones(d):
    i = np.arange(d) // RW_HEAD
    return jnp.asarray(i[:, None] == i[None, :], BF16)


def _rwkv_prep(p_rkv, p_lora, v_first, v_res, mu_rkv, mu_lora, w0, w2, a0, a2, g2, k_k):
    bsz, seq_len, w3 = p_rkv.shape
    d = w3 // 3
    wl = p_lora.shape[2]
    ts = ROW_TILE
    assert 2 * w2.shape[1] == LANES and 2 * a2.shape[1] == LANES
    has_res = v_res is not None
    const = lambda shape: pl.BlockSpec(shape, lambda b, i: (0,) * len(shape))
    tile = pl.BlockSpec((1, ts, d), lambda b, i: (b, i, 0))
    pair = pl.BlockSpec((1, 2, ts, d), lambda b, i: (b, 0, i, 0))
    if has_res:
        v0, v1, v2 = v_res
        rank = _round_up(v1.shape[1], LANES)
        v0, v1, v2 = v0.reshape(1, d), _pad_to(v1, rank, 1).astype(BF16), _pad_to(v2, rank, 0).astype(BF16)
    else:
        v0, v1, v2 = jnp.zeros((1, d), F32), jnp.zeros((d, LANES), BF16), jnp.zeros((LANES, d), BF16)
    params = [mu_rkv, mu_lora, _block_diag2(w2).astype(BF16), w0.reshape(1, 2 * d),
              _block_diag2(a2).astype(BF16), a0.reshape(1, 2 * d),
              _pad_to(g2, wl - 2 * LANES, 0).astype(BF16), k_k.reshape(1, d), v0, v1, v2, _head_ones(d)]
    args = [p_rkv, p_rkv, p_rkv, p_lora, p_lora, p_lora] + ([v_first] if has_res else []) + params
    tok = jax.ShapeDtypeStruct((bsz, seq_len, d), F32)
    two = jax.ShapeDtypeStruct((bsz, 2, seq_len, d), F32)
    return pl.pallas_call(
        functools.partial(_rwkv_prep_kernel, has_res=has_res),
        grid=(bsz, seq_len // ts),
        in_specs=(_halo_specs(ts, w3, seq_len) + _halo_specs(ts, wl, seq_len) + ([tile] if has_res else [])
                  + [const(x.shape) for x in params]),
        out_specs=[tile, tile, tile, tile, pair, pair, tile],
        out_shape=[tok, tok, tok, tok, two, two, tok],
        compiler_params=pltpu.CompilerParams(dimension_semantics=("parallel", "parallel"),
                                             vmem_limit_bytes=VMEM_LIMIT),
        name="rwkv_prep",
    )(*args)


def _mix_kernel(yf_ref, yb_ref, r_ref, k_ref, v_ref, a_ref, g_ref, yh_ref, pg_ref, h_ref,
                ka_ref, rk_ref, lg_ref, lb_ref, bd_ref, why_ref, wrw_ref, wo_ref, n1g_ref, n1b_ref,
                wr_ref, br_ref, h_out, aff_out, *, alpha, n_exp):
    d_model = h_ref.shape[2]
    bd = bd_ref[...]
    inv = 1.0 / RW_HEAD
    y = yf_ref[0] + yb_ref[0]
    mu = _head_sum(y, bd) * inv
    dy = y - mu
    var = _head_sum(dy * dy, bd) * inv
    yn = dy * lax.rsqrt(var + GN_EPS) * lg_ref[...] + lb_ref[...]
    k = k_ref[0]
    ka = ka_ref[...]
    k_sum = k * (1.0 + (a_ref[0, 0] - 1.0) * ka) + k * (1.0 + (a_ref[0, 1] - 1.0) * ka)
    bonus = _head_sum(r_ref[0] * k_sum * rk_ref[...], bd) * v_ref[0]
    y_rw = (yn + bonus) * g_ref[0]
    o_hy = _dg(yh_ref[0].astype(BF16), why_ref[...], _NN)
    o_rw = _dg(y_rw.astype(BF16), wrw_ref[...], _NN)
    gates = jax.nn.sigmoid(pg_ref[0])
    merged = gates[:, :d_model] * o_hy + gates[:, d_model:] * o_rw
    mo = _dg(merged.astype(BF16), wo_ref[...], _NN)
    h1 = _layer_norm(alpha * h_ref[0] + mo, n1g_ref[...], n1b_ref[...])
    h_out[0] = h1
    logits = _dg(h1.astype(BF16), wr_ref[...], _NN) + br_ref[...]
    col = lax.broadcasted_iota(jnp.int32, logits.shape, 1)
    logits = jnp.where(col < n_exp, logits, -1e30)
    ex = jnp.exp(logits - jnp.max(logits, axis=1, keepdims=True))
    aff = ex / jnp.sum(ex, axis=1, keepdims=True)
    aff_out[0] = jnp.transpose(aff)[:n_exp, :]


def _mix(y_f, y_b, r, k, v, a, g, y_hy, p_gate, h, k_a, r_k, lnx_g, lnx_b, w_o_hy, w_o_rw, w_out, ln_g, ln_b,
         w_router, b_router, alpha):
    bsz, seq_len, d = r.shape
    d_model = h.shape[2]
    n_exp = w_router.shape[1]
    ts = MIX_TILE
    tile = lambda w: pl.BlockSpec((1, ts, w), lambda b, i: (b, i, 0))
    const = lambda shape: pl.BlockSpec(shape, lambda b, i: (0,) * len(shape))
    vec = lambda x: x.reshape(1, -1)
    params = [vec(k_a), vec(r_k), vec(lnx_g), vec(lnx_b), _head_ones(d), w_o_hy.astype(BF16),
              w_o_rw.astype(BF16), w_out.astype(BF16), vec(ln_g), vec(ln_b),
              _pad_to(w_router, LANES, 1).astype(BF16), _pad_to(vec(b_router), LANES, 1)]
    return pl.pallas_call(
        functools.partial(_mix_kernel, alpha=alpha, n_exp=n_exp),
        grid=(bsz, seq_len // ts),
        in_specs=[tile(d), tile(d),
                  tile(d), tile(d), tile(d), pl.BlockSpec((1, 2, ts, d), lambda b, i: (b, 0, i, 0)),
                  tile(d), tile(d), tile(2 * d_model), tile(d_model)] + [const(x.shape) for x in params],
        out_specs=[tile(d_model), pl.BlockSpec((1, n_exp, ts), lambda b, i: (b, 0, i))],
        out_shape=[jax.ShapeDtypeStruct((bsz, seq_len, d_model), F32),
                   jax.ShapeDtypeStruct((bsz, n_exp, seq_len), F32)],
        compiler_params=pltpu.CompilerParams(dimension_semantics=("parallel", "parallel"),
                                             vmem_limit_bytes=VMEM_LIMIT),
        name="mix_out",
    )(y_f, y_b, r, k, v, a, g, y_hy, p_gate, h, *params)


LN_EPS = 1e-5
GN_EPS = 64e-5
HY_ORDER = 2
HY_TARGET = 1e-2
HY_SHORT_DECAY_PCT = 0.3
HY_LONG_DECAY_PCT = 1.5
N_EXPERTS = 16
EC_CAPACITY = 2


def _layer_norm(x, g, b):
    mu = jnp.mean(x, -1, keepdims=True)
    var = jnp.mean(jnp.square(x - mu), -1, keepdims=True)
    return (x - mu) * lax.rsqrt(var + LN_EPS) * g + b


def _hyena_filters(seq_len, w1, b1, w2, b2, w3, freq, d_hy):
    emb = w1.shape[0]
    bands = (emb - 1) // 2
    t = jnp.linspace(0.0, 1.0, seq_len, dtype=F32)[:, None]
    w = 2.0 * math.pi * jnp.arange(seq_len, dtype=F32)[:, None] / seq_len
    f = jnp.linspace(1e-4, bands - 1, bands, dtype=F32)[None, :]
    z = jnp.concatenate([t, jnp.cos(f * w), -jnp.sin(f * w)], axis=-1)
    a = jnp.sin(freq * (z @ w1 + b1))
    for i in range(w2.shape[0]):
        a = jnp.sin(freq * (a @ w2[i] + b2[i]))
    filt = (a @ w3).reshape(seq_len, HY_ORDER, 2, d_hy)
    max_decay = math.log(HY_TARGET) / HY_SHORT_DECAY_PCT
    min_decay = math.log(HY_TARGET) / HY_LONG_DECAY_PCT
    deltas = jnp.linspace(min_decay, max_decay, d_hy, dtype=F32)
    filt = filt * jnp.exp(-t * jnp.abs(deltas))[:, None, None, :]
    filt = filt * lax.rsqrt(jnp.sum(filt * filt, axis=(0, 2), keepdims=True) + 1e-12)
    return filt


def _hyena_branch(p, conv_w, conv_b, filt, bias):
    seq_len = p.shape[1]
    v, x1, x2 = _shortconv(p, conv_w, conv_b)
    z = _long_conv_gated(v, x1, *_filter_spectrum(filt[:, 0], seq_len), bias[0])
    return _long_conv_gated(z, x2, *_filter_spectrum(filt[:, 1], seq_len), bias[1])


def kernel(x, ln0_g, ln0_b, w_in, hy_conv_w, hy_conv_b, hy_ffn_w1, hy_ffn_b1, hy_ffn_w2, hy_ffn_b2,
           hy_ffn_w3, hy_freq, hy_bias, rw_mu, rw_w0, rw_w2, rw_a0, rw_a2, rw_v0, rw_v1, rw_v2, rw_g2,
           rw_k_k, rw_k_a, rw_r_k, rw_lnx_g, rw_lnx_b, w_o_hy, w_o_rw, w_out, ln1_g, ln1_b,
           w_router, b_router, w_gate, w_up, w_down, ln2_g, ln2_b):
    bsz, seq_len, d_model = x.shape
    depth = w_in.shape[0]
    d_hy = hy_bias.shape[-1]
    d_rw = rw_w0.shape[-1]
    p_hy = 3 * d_hy
    p_rw = rw_mu.shape[-1]
    lora = p_rw - 3 * d_rw
    lora_pad = _round_up(lora, LANES)
    widths = (p_hy, 3 * d_rw, lora_pad, 2 * d_model)
    alpha = (2 * depth) ** 0.25
    cap = EC_CAPACITY * seq_len // N_EXPERTS
    tokens = bsz * seq_len
    h = _ln_rows(x.reshape(tokens, d_model), ln0_g, ln0_b).reshape(bsz, seq_len, d_model)
    v_first = None
    wg_all, wu_all, wd_all = w_gate.astype(BF16), w_up.astype(BF16), w_down.astype(BF16)
    for l in range(depth):
        c0, c1, c2 = p_hy, p_hy + 3 * d_rw, p_hy + p_rw
        w = jnp.concatenate([w_in[l][:, :c1], _pad_to(w_in[l][:, c1:c2], lora_pad, 1), w_in[l][:, c2:]], axis=1)
        pa, pb, pl_, pg = [t.reshape(bsz, seq_len, -1) for t in _proj(h.reshape(tokens, d_model), w, widths)]
        filt = _hyena_filters(seq_len, hy_ffn_w1[l], hy_ffn_b1[l], hy_ffn_w2[l], hy_ffn_b2[l],
                              hy_ffn_w3[l], hy_freq[l], d_hy)
        y_hy = _hyena_branch(pa, hy_conv_w[l], hy_conv_b[l], filt, hy_bias[l])
        v_res = None if l == 0 else (rw_v0[l - 1], rw_v1[l - 1], rw_v2[l - 1])
        mu = rw_mu[l]
        r, k, v, kk, lw, a, g = _rwkv_prep(pb, pl_, v_first, v_res, mu[:, :3 * d_rw],
                                           _pad_to(mu[:, 3 * d_rw:], lora_pad, 1), rw_w0[l], rw_w2[l],
                                           rw_a0[l], rw_a2[l], rw_g2[l], rw_k_k[l])
        if l == 0:
            v_first = v
        y_f, y_b = wkv7(r, k, v, kk, lw, a, rw_k_a[l])
        h, aff_t = _mix(y_f, y_b, r, k, v, a, g, y_hy, pg, h, rw_k_a[l], rw_r_k[l], rw_lnx_g[l], rw_lnx_b[l],
                        w_o_hy[l], w_o_rw[l], w_out[l], ln1_g[l], ln1_b[l], w_router[l], b_router[l], alpha)
        idx, gate = _route(aff_t, cap)
        h = _moe_block(h, idx, gate, wg_all, wu_all, wd_all, l, ln2_g[l], ln2_b[l], alpha)
    return h
```

```python
import functools
import math

import jax
import jax.numpy as jnp
import numpy as np
from jax import lax
from jax.experimental import pallas as pl
from jax.experimental.pallas import tpu as pltpu

F32 = jnp.float32
BF16 = jnp.bfloat16

RW_HEAD = 64
LANES = 128
MXU_WIDTH = 256
WKV_CHUNK = 64
WKV_CHUNKS_PER_STEP = 4
VMEM_LIMIT = 56 * 1024 * 1024


def _split(x):
    hi = x.astype(BF16)
    lo = (x - hi.astype(F32)).astype(BF16)
    return hi, lo


def _dg(a, b, dims):
    return lax.dot_general(a, b, (dims, ((), ())), preferred_element_type=F32)


def _mm(a, b, dims=((1,), (0,))):
    return _dg(a.astype(BF16), b.astype(BF16), dims)


_NN = ((1,), (0,))
_NT = ((1,), (1,))
_TN = ((0,), (0,))


def _wkv_operands(r, k, v, kk, lw, a, ka, reverse):
    C = WKV_CHUNK
    row = lax.broadcasted_iota(jnp.int32, (C, C), 0)
    col = lax.broadcasted_iota(jnp.int32, (C, C), 1)
    incl = (col >= row) if reverse else (col <= row)

    lw_hi, lw_lo = _split(lw)
    tri = incl.astype(BF16)
    cin = _dg(tri, lw_hi, _NN) + _dg(tri, lw_lo, _NN)
    tot = jnp.sum(lw, axis=0, keepdims=True)
    half = 0.5 * tot
    e0 = jnp.exp(half)
    e1 = jnp.exp(cin - half)
    e2 = jnp.exp(half - cin)
    ew = jnp.exp(-lw)
    kdir = k * (1.0 + (a - 1.0) * ka)
    rt = r * e1
    bt = kk * (e1 * ew)
    kt = kdir * e2
    at = -(kk * a) * e2
    r0 = rt * e0
    b0 = bt * e0
    kh = kt * e0
    ah = at * e0
    return dict(bt=bt, rt=rt, kt=kt, at=at, b0=b0, r0=r0, kh=kh, ah=ah, e0sq=e0 * e0, v=v)


def _wkv_kernel(rf_ref, kf_ref, vf_ref, nf_ref, rb_ref, kb_ref, vb_ref, nb_ref, lwf_ref, lwb_ref,
                af_ref, ab_ref, ka_ref, yf_ref, yb_ref, h_ref):
    C = WKV_CHUNK

    @pl.when(pl.program_id(1) == 0)
    def _():
        h_ref[...] = jnp.zeros_like(h_ref)

    ka = ka_ref[...]
    n_pairs = ka.shape[1] // LANES
    outs = [yf_ref, yb_ref]

    row2 = lax.broadcasted_iota(jnp.int32, (C, 2 * C), 0)
    col2 = lax.broadcasted_iota(jnp.int32, (C, 2 * C), 1) % C
    earlier2 = [col2 < row2, col2 > row2]
    incl2 = [col2 <= row2, col2 >= row2]
    lane = lax.broadcasted_iota(jnp.int32, (1, LANES), 1)
    first = lane < RW_HEAD
    ri = lax.broadcasted_iota(jnp.int32, (LANES, LANES), 0)
    ci = lax.broadcasted_iota(jnp.int32, (LANES, LANES), 1)
    eye = ri == ci
    same_head = (ri // RW_HEAD) == (ci // RW_HEAD)

    def stack2(x):
        return jnp.concatenate([jnp.where(first, x, 0.0), jnp.where(first, 0.0, x)], axis=0)

    chains = [(d, p) for d in range(2) for p in range(n_pairs)]
    n = range(len(chains))
    hp = [h_ref[q] for q in n]
    n_sub = rf_ref.shape[1] // C
    for s in range(n_sub):
        fs = slice(s * C, (s + 1) * C)
        bs = slice((n_sub - 1 - s) * C, (n_sub - s) * C)
        rows = [fs, bs]
        ops = [_wkv_operands(rf_ref[0, fs, :], kf_ref[0, fs, :], vf_ref[0, fs, :], nf_ref[0, fs, :],
                             lwf_ref[0, 0, fs, :], af_ref[0, 0, fs, :], ka, False),
               _wkv_operands(rb_ref[0, bs, :], kb_ref[0, bs, :], vb_ref[0, bs, :], nb_ref[0, bs, :],
                             lwb_ref[0, 0, bs, :], ab_ref[0, 0, bs, :], ka, True)]

        def op(name, q):
            d, p = chains[q]
            return ops[d][name][:, p * LANES:(p + 1) * LANES]

        g = [_mm(jnp.concatenate([op("bt", q), op("rt", q)], axis=0),
                 jnp.concatenate([stack2(op("kt", q)), stack2(op("at", q))], axis=0), _NT) for q in n]
        bh = [_mm(jnp.concatenate([op("b0", q), op("r0", q)], axis=0), hp[q]) for q in n]
        v2 = [stack2(op("v", q)) for q in n]
        x = [bh[q][:C] + _mm(jnp.where(earlier2[chains[q][0]], g[q][:C, :2 * C], 0.0), v2[q]) for q in n]
        pw = [jnp.where(earlier2[chains[q][0]], g[q][:C, 2 * C:], 0.0) for q in n]
        n_steps = C.bit_length() - 1
        for i in range(n_steps):
            x = [x[q] + _mm(pw[q], stack2(x[q])) for q in n]
            if i + 1 < n_steps:
                pw = [_mm(pw[q], stack2(pw[q])) for q in n]
        for q in n:
            d, p = chains[q]
            a_r = jnp.concatenate([jnp.where(incl2[d], g[q][C:, :2 * C], 0.0),
                                   jnp.where(incl2[d], g[q][C:, 2 * C:], 0.0)], axis=1)
            outs[d][0, rows[d], p * LANES:(p + 1) * LANES] = bh[q][C:] + _mm(
                a_r, jnp.concatenate([v2[q], stack2(x[q])], axis=0))
        new = []
        for q in n:
            dg = jnp.where(eye, jnp.broadcast_to(op("e0sq", q), (LANES, LANES)), 0.0)
            hn = _mm(jnp.concatenate([op("kh", q), op("ah", q), dg], axis=0),
                     jnp.concatenate([op("v", q), x[q], hp[q]], axis=0), _TN)
            new.append(jnp.where(same_head, hn, 0.0))
        hp = new
    for q in n:
        h_ref[q] = hp[q]


def wkv7(r, k, v, kk, lw, a, k_a):
    B, S, D = r.shape
    C = WKV_CHUNK * WKV_CHUNKS_PER_STEP
    nc = S // C
    assert S % C == 0 and D % LANES == 0
    fwd = pl.BlockSpec((1, C, D), lambda b, c: (b, c, 0))
    bwd = pl.BlockSpec((1, C, D), lambda b, c: (b, nc - 1 - c, 0))
    fwd2 = pl.BlockSpec((1, 1, C, D), lambda b, c: (b, 0, c, 0))
    bwd2 = pl.BlockSpec((1, 1, C, D), lambda b, c: (b, 1, nc - 1 - c, 0))
    out = jax.ShapeDtypeStruct((B, S, D), F32)
    return pl.pallas_call(
        _wkv_kernel,
        grid=(B, nc),
        in_specs=[fwd] * 4 + [bwd] * 4 + [fwd2, bwd2, fwd2, bwd2, pl.BlockSpec((1, D), lambda b, c: (0, 0))],
        out_specs=[fwd, bwd],
        out_shape=[out, out],
        scratch_shapes=[pltpu.VMEM((2 * D // LANES, LANES, LANES), F32)],
        compiler_params=pltpu.CompilerParams(dimension_semantics=("parallel", "arbitrary")),
        name="wkv7_chunked",
    )(r, k, v, kk, r, k, v, kk, lw, lw, a, a, k_a.reshape(1, D))


FFT_N2 = 64
FFT_K1_BLOCK = 8


def _round_up(n, m):
    return (n + m - 1) // m * m


def _twice(m, axis):
    hi = jnp.asarray(m, F32).astype(BF16)
    return jnp.concatenate([hi, hi], axis=axis)


def _hilo(x):
    hi, lo = _split(x)
    return jnp.concatenate([hi, lo], axis=0)


@functools.lru_cache(maxsize=None)
def _fft_tables(seq_len):
    n = 2 * seq_len
    n2 = FFT_N2
    n1 = n // n2
    h1 = n1 // 2
    k1n = h1 + 1
    k1p = _round_up(k1n, FFT_K1_BLOCK)
    k1 = np.arange(k1n)[:, None]
    th1 = 2 * np.pi * k1 * np.arange(h1)[None, :] / n1
    f1 = np.zeros((2 * k1p, h1))
    f1[:k1n] = np.cos(th1)
    f1[k1p:k1p + k1n] = -np.sin(th1)
    m = np.arange(n2)
    th2 = 2 * np.pi * (m[None, None, :] * m[None, :, None] / n2 + m[None, None, :] * np.arange(k1n)[:, None, None] / n)
    gr, gi = np.cos(th2), -np.sin(th2)
    g = np.zeros((k1p, 2 * n2, 2 * n2))
    g[:k1n] = np.block([[gr, -gi], [gi, gr]])
    gt = np.transpose(g, (0, 2, 1))
    coef = np.full((k1n,), 2.0)
    coef[0] = 1.0
    coef[-1] = 1.0
    th3 = th1.T
    m3 = np.zeros((h1, 2 * k1p))
    m3[:, :k1n] = coef * np.cos(th3) / n
    m3[:, k1p:k1p + k1n] = -coef * np.sin(th3) / n
    th1f = 2 * np.pi * k1 * np.arange(n1)[None, :] / n1
    f1_full = np.zeros((2 * k1p, n1))
    f1_full[:k1n] = np.cos(th1f)
    f1_full[k1p:k1p + k1n] = -np.sin(th1f)
    return dict(n1=n1, h1=h1, k1n=k1n, k1p=k1p, f1=f1, g=g, gt=gt, m3=m3, f1_full=f1_full)


FFT_N2_BLOCK = 8


def _rows_to_lanes(ref):
    return jnp.concatenate([ref[0, :, i, :] for i in range(FFT_N2_BLOCK)], axis=1)


def _lanes_to_rows(ref, val):
    ch = ref.shape[3]
    for i in range(FFT_N2_BLOCK):
        ref[0, :, i, :] = val[:, i * ch:(i + 1) * ch]


def _fft_s1_kernel(f_ref, x_ref, o_ref):
    _lanes_to_rows(o_ref, jnp.dot(f_ref[...], _hilo(_rows_to_lanes(x_ref)), preferred_element_type=F32))


def _fft_mid_kernel(g_ref, gt_ref, kr_ref, ki_ref, a_ref, o_ref):
    n2 = FFT_N2
    for j in range(FFT_K1_BLOCK):
        xin = jnp.concatenate([a_ref[0, 0, j], a_ref[0, 1, j]], axis=0)
        z = jnp.dot(g_ref[j], _hilo(xin), preferred_element_type=F32)
        zr, zi = z[:n2], z[n2:]
        kr, ki = kr_ref[j], ki_ref[j]
        y = jnp.concatenate([zr * kr - zi * ki, zr * ki + zi * kr], axis=0)
        b = jnp.dot(gt_ref[j], _hilo(y), preferred_element_type=F32)
        o_ref[0, 0, j] = b[:n2]
        o_ref[0, 1, j] = b[n2:]


def _fft_fwd_kernel(g_ref, a_ref, s_ref, re_ref, im_ref):
    n2 = FFT_N2
    for j in range(FFT_K1_BLOCK):
        xin = jnp.concatenate([a_ref[0, 0, j], a_ref[0, 1, j]], axis=0)
        z = jnp.dot(g_ref[j], _hilo(xin), preferred_element_type=F32) * s_ref[...]
        re_ref[j] = z[:n2]
        im_ref[j] = z[n2:]


def _fft_s3_kernel(m_ref, b_ref, u_ref, x_ref, bias_ref, o_ref):
    y = jnp.dot(m_ref[...], _hilo(_rows_to_lanes(b_ref)), preferred_element_type=F32)
    _lanes_to_rows(o_ref, _rows_to_lanes(x_ref) * (y + _rows_to_lanes(u_ref) * bias_ref[...]))


def _long_conv_gated(u, gate, kf_re, kf_im, bias):
    bsz, seq_len, ch = u.shape
    t = _fft_tables(seq_len)
    n2, h1, k1p = FFT_N2, t["h1"], t["k1p"]
    lanes = n2 * ch
    nb = FFT_N2_BLOCK
    lt = nb * ch
    assert seq_len == h1 * n2 and ch % LANES == 0
    params = pltpu.CompilerParams(dimension_semantics=("parallel", "parallel"),
                                  vmem_limit_bytes=VMEM_LIMIT)
    spec4 = pl.BlockSpec((1, 2 * k1p, nb, ch), lambda b, j: (b, 0, j, 0))
    row = pl.BlockSpec((1, h1, nb, ch), lambda b, j: (b, 0, j, 0))
    f1 = _twice(t["f1"], 1)
    a = pl.pallas_call(
        _fft_s1_kernel,
        grid=(bsz, n2 // nb),
        in_specs=[pl.BlockSpec(f1.shape, lambda b, j: (0, 0)), row],
        out_specs=spec4,
        out_shape=jax.ShapeDtypeStruct((bsz, 2 * k1p, n2, ch), F32),
        compiler_params=params, name="hyena_dft1",
    )(f1, u.reshape(bsz, h1, n2, ch))
    kb = FFT_K1_BLOCK
    g = _twice(t["g"], 2)
    gt = _twice(t["gt"], 2)
    mat_spec = pl.BlockSpec((kb,) + g.shape[1:], lambda b, j: (j, 0, 0))
    kf_spec = pl.BlockSpec((kb, n2, ch), lambda b, j: (j, 0, 0))
    blk = pl.BlockSpec((1, 2, kb, n2, ch), lambda b, j: (b, 0, j, 0, 0))
    bm = pl.pallas_call(
        _fft_mid_kernel,
        grid=(bsz, k1p // kb),
        in_specs=[mat_spec, mat_spec, kf_spec, kf_spec, blk],
        out_specs=blk,
        out_shape=jax.ShapeDtypeStruct((bsz, 2, k1p, n2, ch), F32),
        compiler_params=params, name="hyena_dft2",
    )(g, gt, kf_re, kf_im, a.reshape(bsz, 2, k1p, n2, ch))
    m3 = _twice(t["m3"], 1)
    out = pl.pallas_call(
        _fft_s3_kernel,
        grid=(bsz, n2 // nb),
        in_specs=[pl.BlockSpec(m3.shape, lambda b, j: (0, 0)), spec4,
                  row, row, pl.BlockSpec((1, lt), lambda b, j: (0, j))],
        out_specs=row,
        out_shape=jax.ShapeDtypeStruct((bsz, h1, n2, ch), F32),
        compiler_params=params, name="hyena_dft3",
    )(m3, bm.reshape(bsz, 2 * k1p, n2, ch), u.reshape(bsz, h1, n2, ch), gate.reshape(bsz, h1, n2, ch),
      jnp.tile(bias, n2).reshape(1, lanes))
    return out.reshape(bsz, seq_len, ch)


def _filter_spectrum(filt_o, scale, seq_len):
    t = _fft_tables(seq_len)
    n1, n2, k1p = t["n1"], FFT_N2, t["k1p"]
    nb, kb = FFT_N2_BLOCK, FFT_K1_BLOCK
    ch = filt_o.shape[-1]
    k_full = jnp.concatenate([filt_o[:, 0], filt_o[::-1, 1]], axis=0)
    f1 = _twice(t["f1_full"], 1)
    params = pltpu.CompilerParams(dimension_semantics=("parallel", "parallel"),
                                  vmem_limit_bytes=VMEM_LIMIT)
    a = pl.pallas_call(
        _fft_s1_kernel,
        grid=(1, n2 // nb),
        in_specs=[pl.BlockSpec(f1.shape, lambda b, j: (0, 0)),
                  pl.BlockSpec((1, n1, nb, ch), lambda b, j: (b, 0, j, 0))],
        out_specs=pl.BlockSpec((1, 2 * k1p, nb, ch), lambda b, j: (b, 0, j, 0)),
        out_shape=jax.ShapeDtypeStruct((1, 2 * k1p, n2, ch), F32),
        compiler_params=params, name="filter_dft1",
    )(f1, k_full.reshape(1, n1, n2, ch))
    g = _twice(t["g"], 2)
    kf_spec = pl.BlockSpec((kb, n2, ch), lambda b, j: (j, 0, 0))
    out = jax.ShapeDtypeStruct((k1p, n2, ch), F32)
    return pl.pallas_call(
        _fft_fwd_kernel,
        grid=(1, k1p // kb),
        in_specs=[pl.BlockSpec((kb,) + g.shape[1:], lambda b, j: (j, 0, 0)),
                  pl.BlockSpec((1, 2, kb, n2, ch), lambda b, j: (b, 0, j, 0, 0)),
                  pl.BlockSpec((1, ch), lambda b, j: (0, 0))],
        out_specs=[kf_spec, kf_spec],
        out_shape=[out, out],
        compiler_params=params, name="filter_dft2",
    )(g, a.reshape(1, 2, k1p, n2, ch), scale.reshape(1, ch))


ROUTE_COLS = 16
ROUTE_TOKEN_CHUNK = 512


def _route_kernel(aff_ref, o_ref, key_ref, val_ref, *, cap):
    a = aff_ref[0]
    n_exp, seq_len = a.shape
    bits = pltpu.bitcast(a, jnp.int32)

    def bisect(i, t):
        cand = t | jnp.left_shift(1, 30 - i)
        cnt = jnp.sum((bits >= cand).astype(jnp.int32), axis=1, keepdims=True)
        return jnp.where(cnt >= cap, cand, t)

    thr = lax.fori_loop(0, 31, bisect, jnp.zeros((n_exp, 1), jnp.int32))
    gt = bits > thr
    eq = bits == thr

    ri = lax.broadcasted_iota(jnp.int32, (LANES, LANES), 0)
    ci = lax.broadcasted_iota(jnp.int32, (LANES, LANES), 1)
    upper = (ri < ci).astype(BF16)

    def prefix(mask):
        m = mask.astype(F32)
        run = jnp.zeros((n_exp, 1), F32)
        parts = []
        for j in range(seq_len // LANES):
            tile = m[:, j * LANES:(j + 1) * LANES]
            parts.append(run + jnp.dot(tile.astype(BF16), upper, preferred_element_type=F32))
            run = run + jnp.sum(tile, axis=1, keepdims=True)
        return jnp.concatenate(parts, axis=1), run

    eq_before, _ = prefix(eq)
    n_gt = jnp.sum(gt.astype(F32), axis=1, keepdims=True)
    sel = jnp.logical_or(gt, jnp.logical_and(eq, eq_before < cap - n_gt))
    pos, _ = prefix(sel)
    key_ref[...] = jnp.where(sel, pos, -1.0)
    val_ref[...] = a

    tok = lax.broadcasted_iota(jnp.int32, (1, seq_len), 1)
    t_hi = (tok // 64).astype(F32)
    t_lo = (tok % 64).astype(F32)
    slot = lax.broadcasted_iota(jnp.int32, (cap, ROUTE_TOKEN_CHUNK), 0).astype(F32)
    zeros = jnp.zeros((ROUTE_COLS - 5, seq_len), F32)

    def compact(e, carry):
        key = key_ref[pl.ds(e, 1), :]
        g = val_ref[pl.ds(e, 1), :]
        g_hi = g.astype(BF16).astype(F32)
        g_mid = (g - g_hi).astype(BF16).astype(F32)
        g_lo = g - g_hi - g_mid
        rows = jnp.concatenate([t_hi, t_lo, g_hi, g_mid, g_lo, zeros], axis=0).astype(BF16)
        acc = jnp.zeros((cap, ROUTE_COLS), F32)
        for j in range(seq_len // ROUTE_TOKEN_CHUNK):
            sl = slice(j * ROUTE_TOKEN_CHUNK, (j + 1) * ROUTE_TOKEN_CHUNK)
            onehot = jnp.where(key[:, sl] == slot, 1.0, 0.0).astype(BF16)
            acc = acc + _dg(onehot, rows[:, sl], _NT)
        o_ref[0, pl.ds(e, 1)] = acc[None]
        return carry

    lax.fori_loop(0, n_exp, compact, 0)


def _route(aff_t, cap):
    bsz, n_exp, seq_len = aff_t.shape
    assert seq_len % ROUTE_TOKEN_CHUNK == 0 and seq_len <= 64 * 256
    out = pl.pallas_call(
        functools.partial(_route_kernel, cap=cap),
        grid=(bsz,),
        in_specs=[pl.BlockSpec((1, n_exp, seq_len), lambda b: (b, 0, 0))],
        out_specs=pl.BlockSpec((1, n_exp, cap, ROUTE_COLS), lambda b: (b, 0, 0, 0)),
        out_shape=jax.ShapeDtypeStruct((bsz, n_exp, cap, ROUTE_COLS), F32),
        scratch_shapes=[pltpu.VMEM((n_exp, seq_len), F32), pltpu.VMEM((n_exp, seq_len), F32)],
        compiler_params=pltpu.CompilerParams(dimension_semantics=("parallel",),
                                             vmem_limit_bytes=VMEM_LIMIT),
        name="ec_route",
    )(aff_t)
    idx = (out[..., 0] * 64.0 + out[..., 1]).astype(jnp.int32)
    gate = out[..., 2] + out[..., 3] + out[..., 4]
    return idx, gate


MOE_FF_TILE = 1024
MOE_GROUP = 16
LN_ROWS = 256


def _gather_rows(h_ref, idx_ref, xb_ref, base):
    rows = [h_ref[0, pl.ds(idx_ref[0, 0, base + j], 1), :] for j in range(MOE_GROUP)]
    xb_ref[pl.ds(base, MOE_GROUP), :] = jnp.concatenate(rows, axis=0).astype(BF16)


def _scatter_rows(o_ref, idx_ref, gate_ref, y_ref, base, enabled):
    y8 = y_ref[pl.ds(base, 8), :]
    toks = [idx_ref[0, 0, base + j] for j in range(8)]
    rows = [o_ref[0, pl.ds(toks[j], 1), :] + jnp.where(enabled, gate_ref[0, 0, base + j], 0.0) * y8[j:j + 1, :]
            for j in range(8)]
    for j in range(8):
        o_ref[0, pl.ds(toks[j], 1), :] = rows[j]


def _moe_kernel(idx_ref, gate_ref, idxn_ref, idxp_ref, gatep_ref, h_ref, wg_ref, wu_ref, wd_ref,
                lg_ref, lb_ref, o_ref, xb_ref, y_ref, *, alpha, n_f):
    e = pl.program_id(1)
    f = pl.program_id(2)
    last_e = pl.num_programs(1) - 1
    last_f = n_f - 1
    cap = xb_ref.shape[1]
    seq_len = h_ref.shape[1]
    share = cap // n_f
    slot = e % 2
    other = 1 - slot
    first_step = jnp.logical_and(e == 0, f == 0)

    @pl.when(first_step)
    def _():
        o_ref[...] = jnp.zeros_like(o_ref)
        y_ref[...] = jnp.zeros_like(y_ref)

        def gather(i, carry):
            _gather_rows(h_ref, idx_ref, xb_ref.at[0], pl.multiple_of(i * MOE_GROUP, MOE_GROUP))
            return carry

        lax.fori_loop(0, cap // MOE_GROUP, gather, 0)

    xb = xb_ref[slot]
    a = jnp.dot(xb, wg_ref[0], preferred_element_type=F32)
    u = jnp.dot(xb, wu_ref[0], preferred_element_type=F32)
    act = (a * jax.nn.sigmoid(a) * u).astype(BF16)
    part = jnp.dot(act, wd_ref[0], preferred_element_type=F32)

    for i in range(share // MOE_GROUP):
        _gather_rows(h_ref, idxn_ref, xb_ref.at[other], pl.multiple_of(f * share + i * MOE_GROUP, MOE_GROUP))
    for i in range(share // 8):
        _scatter_rows(o_ref, idxp_ref, gatep_ref, y_ref.at[other], pl.multiple_of(f * share + i * 8, 8), e > 0)

    @pl.when(f == 0)
    def _():
        y_ref[slot] = part

    @pl.when(f > 0)
    def _():
        y_ref[slot] += part

    @pl.when(jnp.logical_and(e == last_e, f == last_f))
    def _():
        def scatter(i, carry):
            _scatter_rows(o_ref, idx_ref, gate_ref, y_ref.at[slot], pl.multiple_of(i * 8, 8), True)
            return carry

        lax.fori_loop(0, cap // 8, scatter, 0)

    @pl.when(jnp.logical_and(e == last_e, f == last_f))
    def _():
        def norm(i, carry):
            rows = pl.ds(pl.multiple_of(i * LN_ROWS, LN_ROWS), LN_ROWS)
            z = alpha * h_ref[0, rows, :] + o_ref[0, rows, :]
            o_ref[0, rows, :] = _layer_norm(z, lg_ref[...], lb_ref[...])
            return carry

        lax.fori_loop(0, seq_len // LN_ROWS, norm, 0)


def _moe_block(h, idx, gate, wg, wu, wd, layer, ln_g, ln_b, alpha):
    bsz, seq_len, d = h.shape
    _, n_exp, _, ff = wg.shape
    cap = idx.shape[-1]
    ft = min(MOE_FF_TILE, ff)
    assert ff % ft == 0 and seq_len % LN_ROWS == 0
    n_f = ff // ft
    assert cap % (n_f * MOE_GROUP) == 0

    def smem(shift):
        def index(b, e, f):
            return (b * n_exp + jnp.clip(e + shift, 0, n_exp - 1), 0, 0)
        return pl.BlockSpec((1, 1, cap), index, memory_space=pltpu.SMEM)

    whole = pl.BlockSpec((1, seq_len, d), lambda b, e, f: (b, 0, 0), pipeline_mode=pl.Buffered(1))
    vec = pl.BlockSpec((1, d), lambda b, e, f: (0, 0))
    idx3 = idx.reshape(bsz * n_exp, 1, cap)
    gate3 = gate.reshape(bsz * n_exp, 1, cap)
    return pl.pallas_call(
        functools.partial(_moe_kernel, alpha=alpha, n_f=n_f),
        grid=(bsz, n_exp, n_f),
        in_specs=[smem(0), smem(0), smem(1), smem(-1), smem(-1), whole,
                  pl.BlockSpec((None, 1, d, ft), lambda b, e, f: (layer, e, 0, f)),
                  pl.BlockSpec((None, 1, d, ft), lambda b, e, f: (layer, e, 0, f)),
                  pl.BlockSpec((None, 1, ft, d), lambda b, e, f: (layer, e, f, 0)),
                  vec, vec],
        out_specs=whole,
        out_shape=jax.ShapeDtypeStruct((bsz, seq_len, d), F32),
        scratch_shapes=[pltpu.VMEM((2, cap, d), BF16), pltpu.VMEM((2, cap, d), F32)],
        compiler_params=pltpu.CompilerParams(
            dimension_semantics=("arbitrary", "arbitrary", "arbitrary"), vmem_limit_bytes=VMEM_LIMIT),
        name="ec_moe",
    )(idx3, gate3, idx3, idx3, gate3, h, wg, wu, wd, ln_g.reshape(1, d), ln_b.reshape(1, d))


ROW_TILE = 512
MIX_TILE = 512


def _proj_kernel(x_ref, w_ref, *o_refs):
    xb = x_ref[...].astype(BF16)
    off = 0
    for o_ref in o_refs:
        n = o_ref.shape[1]
        o_ref[...] = jnp.dot(xb, w_ref[:, off:off + n], preferred_element_type=F32)
        off += n


def _proj(x, w, widths):
    M, K = x.shape
    tm = ROW_TILE
    assert M % tm == 0 and sum(widths) == w.shape[1] and all(n % LANES == 0 for n in widths)
    return pl.pallas_call(
        _proj_kernel,
        grid=(M // tm,),
        in_specs=[pl.BlockSpec((tm, K), lambda i: (i, 0)),
                  pl.BlockSpec(w.shape, lambda i: (0, 0), pipeline_mode=pl.Buffered(1))],
        out_specs=[pl.BlockSpec((tm, n), lambda i: (i, 0)) for n in widths],
        out_shape=[jax.ShapeDtypeStruct((M, n), F32) for n in widths],
        compiler_params=pltpu.CompilerParams(dimension_semantics=("parallel",),
                                             vmem_limit_bytes=VMEM_LIMIT),
        name="proj",
    )(x, w.astype(BF16))


def _pad_to(w, n, axis):
    pad = [(0, 0)] * w.ndim
    pad[axis] = (0, n - w.shape[axis])
    return jnp.pad(w, pad)


def _ln_kernel(x_ref, g_ref, b_ref, o_ref):
    o_ref[...] = _layer_norm(x_ref[...], g_ref[...], b_ref[...])


def _ln_rows(x, g, b):
    M, D = x.shape
    row = pl.BlockSpec((ROW_TILE, D), lambda i: (i, 0))
    vec = pl.BlockSpec((1, D), lambda i: (0, 0))
    return pl.pallas_call(
        _ln_kernel, grid=(M // ROW_TILE,), in_specs=[row, vec, vec], out_specs=row,
        out_shape=jax.ShapeDtypeStruct((M, D), F32),
        compiler_params=pltpu.CompilerParams(dimension_semantics=("parallel",)),
        name="ln_in",
    )(x, g.reshape(1, D), b.reshape(1, D))


def _halo_specs(ts, width, seq_len):
    per = ts // 8
    last = seq_len // 8 - 1
    cur = pl.BlockSpec((1, ts, width), lambda b, i: (b, i, 0))
    prev = pl.BlockSpec((1, 8, width), lambda b, i: (b, jnp.maximum(i * per - 1, 0), 0))
    nxt = pl.BlockSpec((1, 8, width), lambda b, i: (b, jnp.minimum((i + 1) * per, last), 0))
    return [cur, prev, nxt]


def _neighbours(x, prev_blk, next_blk):
    ts = x.shape[0]
    i = pl.program_id(1)
    rid = lax.broadcasted_iota(jnp.int32, x.shape, 0)
    prev_row = jnp.where(i > 0, prev_blk[7:8, :], 0.0)
    next_row = jnp.where(i < pl.num_programs(1) - 1, next_blk[0:1, :], 0.0)
    zp = jnp.where(rid == 0, prev_row, pltpu.roll(x, 1, 0))
    zn = jnp.where(rid == ts - 1, next_row, pltpu.roll(x, ts - 1, 0))
    return zp, zn


def _shortconv_kernel(p_ref, pp_ref, pn_ref, w_ref, b_ref, v_ref, x1_ref, x2_ref):
    p = p_ref[0]
    zp, zn = _neighbours(p, pp_ref[0], pn_ref[0])
    q = w_ref[0:1, :] * zp + w_ref[1:2, :] * p + w_ref[2:3, :] * zn + b_ref[...]
    d = v_ref.shape[2]
    v_ref[0] = q[:, :d]
    x1_ref[0] = q[:, d:2 * d]
    x2_ref[0] = q[:, 2 * d:]


def _shortconv(p, conv_w, conv_b):
    bsz, seq_len, width = p.shape
    d = width // 3
    ts = ROW_TILE
    out = pl.BlockSpec((1, ts, d), lambda b, i: (b, i, 0))
    return pl.pallas_call(
        _shortconv_kernel,
        grid=(bsz, seq_len // ts),
        in_specs=_halo_specs(ts, width, seq_len) + [pl.BlockSpec((3, width), lambda b, i: (0, 0)),
                                                     pl.BlockSpec((1, width), lambda b, i: (0, 0))],
        out_specs=[out, out, out],
        out_shape=[jax.ShapeDtypeStruct((bsz, seq_len, d), F32)] * 3,
        compiler_params=pltpu.CompilerParams(dimension_semantics=("parallel", "parallel")),
        name="hyena_shortconv",
    )(p, p, p, conv_w, conv_b.reshape(1, width))


def _head_sum(x, bd):
    hi, lo = _split(x)
    w = MXU_WIDTH
    parts = [_dg(hi[:, j:j + w], bd[j:j + w, j:j + w], _NN) + _dg(lo[:, j:j + w], bd[j:j + w, j:j + w], _NN)
             for j in range(0, x.shape[1], w)]
    return jnp.concatenate(parts, axis=1)


def _rwkv_prep_kernel(*refs, has_res):
    (p_ref, pp_ref, pn_ref, q_ref, qp_ref, qn_ref) = refs[:6]
    refs = refs[6:]
    if has_res:
        vf_ref, refs = refs[0], refs[1:]
    (mu_ref, mul_ref, w2_ref, w0_ref, a2_ref, a0_ref, g2_ref, kk_ref, v0_ref, v1_ref, v2_ref, bd_ref,
     r_out, k_out, v_out, kn_out, lw_out, a_out, g_out) = refs
    d = r_out.shape[2]
    p = p_ref[0]
    zp, zn = _neighbours(p, pp_ref[0], pn_ref[0])
    p = p + mu_ref[0:1, :] * (zp - p) + mu_ref[1:2, :] * (zn - p)
    q = q_ref[0]
    zp, zn = _neighbours(q, qp_ref[0], qn_ref[0])
    q = q + mul_ref[0:1, :] * (zp - q) + mul_ref[1:2, :] * (zn - q)
    r, k, v = p[:, :d], p[:, d:2 * d], p[:, 2 * d:]
    if has_res:
        mix = _dg(_dg(v.astype(BF16), v1_ref[...], _NN).astype(BF16), v2_ref[...], _NN)
        v = v + (vf_ref[0] - v) * jax.nn.sigmoid(v0_ref[...] + mix)
    wd, ad, gd = q[:, :LANES], q[:, LANES:2 * LANES], q[:, 2 * LANES:]
    wpre = w0_ref[...] + _dg(jnp.tanh(wd).astype(BF16), w2_ref[...], _NN)
    lw = -math.exp(-0.5) * jax.nn.sigmoid(wpre)
    a = jax.nn.sigmoid(a0_ref[...] + _dg(ad.astype(BF16), a2_ref[...], _NN))
    g = _dg(jax.nn.sigmoid(gd).astype(BF16), g2_ref[...], _NN)
    kq = k * kk_ref[...]
    norm = jnp.sqrt(_head_sum(kq * kq, bd_ref[...]))
    r_out[0] = r
    k_out[0] = k
    v_out[0] = v
    kn_out[0] = kq / jnp.maximum(norm, 1e-12)
    lw_out[0, 0] = lw[:, :d]
    lw_out[0, 1] = lw[:, d:]
    a_out[0, 0] = a[:, :d]
    a_out[0, 1] = a[:, d:]
    g_out[0] = g


def _block_diag2(m):
    z = jnp.zeros_like(m[0])
    return jnp.concatenate([jnp.concatenate([m[0], z], axis=1), jnp.concatenate([z, m[1]], axis=1)], axis=0)


def _head_ones(d):
    i = np.arange(d) // RW_HEAD
    return jnp.asarray(i[:, None] == i[None, :], BF16)


def _rwkv_prep(p_rkv, p_lora, v_first, v_res, mu_rkv, mu_lora, w0, w2, a0, a2, g2, k_k):
    bsz, seq_len, w3 = p_rkv.shape
    d = w3 // 3
    wl = p_lora.shape[2]
    ts = ROW_TILE
    assert 2 * w2.shape[1] == LANES and 2 * a2.shape[1] == LANES
    has_res = v_res is not None
    const = lambda shape: pl.BlockSpec(shape, lambda b, i: (0,) * len(shape))
    tile = pl.BlockSpec((1, ts, d), lambda b, i: (b, i, 0))
    pair = pl.BlockSpec((1, 2, ts, d), lambda b, i: (b, 0, i, 0))
    if has_res:
        v0, v1, v2 = v_res
        rank = _round_up(v1.shape[1], LANES)
        v0, v1, v2 = v0.reshape(1, d), _pad_to(v1, rank, 1).astype(BF16), _pad_to(v2, rank, 0).astype(BF16)
    else:
        v0, v1, v2 = jnp.zeros((1, d), F32), jnp.zeros((d, LANES), BF16), jnp.zeros((LANES, d), BF16)
    params = [mu_rkv, mu_lora, _block_diag2(w2).astype(BF16), w0.reshape(1, 2 * d),
              _block_diag2(a2).astype(BF16), a0.reshape(1, 2 * d),
              _pad_to(g2, wl - 2 * LANES, 0).astype(BF16), k_k.reshape(1, d), v0, v1, v2, _head_ones(d)]
    args = [p_rkv, p_rkv, p_rkv, p_lora, p_lora, p_lora] + ([v_first] if has_res else []) + params
    tok = jax.ShapeDtypeStruct((bsz, seq_len, d), F32)
    two = jax.ShapeDtypeStruct((bsz, 2, seq_len, d), F32)
    return pl.pallas_call(
        functools.partial(_rwkv_prep_kernel, has_res=has_res),
        grid=(bsz, seq_len // ts),
        in_specs=(_halo_specs(ts, w3, seq_len) + _halo_specs(ts, wl, seq_len) + ([tile] if has_res else [])
                  + [const(x.shape) for x in params]),
        out_specs=[tile, tile, tile, tile, pair, pair, tile],
        out_shape=[tok, tok, tok, tok, two, two, tok],
        compiler_params=pltpu.CompilerParams(dimension_semantics=("parallel", "parallel"),
                                             vmem_limit_bytes=VMEM_LIMIT),
        name="rwkv_prep",
    )(*args)


def _mix_kernel(yf_ref, yb_ref, r_ref, k_ref, v_ref, a_ref, g_ref, yh_ref, pg_ref, h_ref,
                ka_ref, rk_ref, lg_ref, lb_ref, bd_ref, why_ref, wrw_ref, wo_ref, n1g_ref, n1b_ref,
                wr_ref, br_ref, h_out, aff_out, *, alpha, n_exp):
    d_model = h_ref.shape[2]
    bd = bd_ref[...]
    inv = 1.0 / RW_HEAD
    y = yf_ref[0] + yb_ref[0]
    mu = _head_sum(y, bd) * inv
    dy = y - mu
    var = _head_sum(dy * dy, bd) * inv
    yn = dy * lax.rsqrt(var + GN_EPS) * lg_ref[...] + lb_ref[...]
    k = k_ref[0]
    ka = ka_ref[...]
    k_sum = k * (1.0 + (a_ref[0, 0] - 1.0) * ka) + k * (1.0 + (a_ref[0, 1] - 1.0) * ka)
    bonus = _head_sum(r_ref[0] * k_sum * rk_ref[...], bd) * v_ref[0]
    y_rw = (yn + bonus) * g_ref[0]
    o_hy = _dg(yh_ref[0].astype(BF16), why_ref[...], _NN)
    o_rw = _dg(y_rw.astype(BF16), wrw_ref[...], _NN)
    gates = jax.nn.sigmoid(pg_ref[0])
    merged = gates[:, :d_model] * o_hy + gates[:, d_model:] * o_rw
    mo = _dg(merged.astype(BF16), wo_ref[...], _NN)
    h1 = _layer_norm(alpha * h_ref[0] + mo, n1g_ref[...], n1b_ref[...])
    h_out[0] = h1
    logits = _dg(h1.astype(BF16), wr_ref[...], _NN) + br_ref[...]
    col = lax.broadcasted_iota(jnp.int32, logits.shape, 1)
    logits = jnp.where(col < n_exp, logits, -1e30)
    ex = jnp.exp(logits - jnp.max(logits, axis=1, keepdims=True))
    aff = ex / jnp.sum(ex, axis=1, keepdims=True)
    aff_out[0] = jnp.transpose(aff)[:n_exp, :]


def _mix(y_f, y_b, r, k, v, a, g, y_hy, p_gate, h, k_a, r_k, lnx_g, lnx_b, w_o_hy, w_o_rw, w_out, ln_g, ln_b,
         w_router, b_router, alpha):
    bsz, seq_len, d = r.shape
    d_model = h.shape[2]
    n_exp = w_router.shape[1]
    ts = MIX_TILE
    tile = lambda w: pl.BlockSpec((1, ts, w), lambda b, i: (b, i, 0))
    const = lambda shape: pl.BlockSpec(shape, lambda b, i: (0,) * len(shape))
    vec = lambda x: x.reshape(1, -1)
    params = [vec(k_a), vec(r_k), vec(lnx_g), vec(lnx_b), _head_ones(d), w_o_hy.astype(BF16),
              w_o_rw.astype(BF16), w_out.astype(BF16), vec(ln_g), vec(ln_b),
              _pad_to(w_router, LANES, 1).astype(BF16), _pad_to(vec(b_router), LANES, 1)]
    return pl.pallas_call(
        functools.partial(_mix_kernel, alpha=alpha, n_exp=n_exp),
        grid=(bsz, seq_len // ts),
        in_specs=[tile(d), tile(d),
                  tile(d), tile(d), tile(d), pl.BlockSpec((1, 2, ts, d), lambda b, i: (b, 0, i, 0)),
                  tile(d), tile(d), tile(2 * d_model), tile(d_model)] + [const(x.shape) for x in params],
        out_specs=[tile(d_model), pl.BlockSpec((1, n_exp, ts), lambda b, i: (b, 0, i))],
        out_shape=[jax.ShapeDtypeStruct((bsz, seq_len, d_model), F32),
                   jax.ShapeDtypeStruct((bsz, n_exp, seq_len), F32)],
        compiler_params=pltpu.CompilerParams(dimension_semantics=("parallel", "parallel"),
                                             vmem_limit_bytes=VMEM_LIMIT),
        name="mix_out",
    )(y_f, y_b, r, k, v, a, g, y_hy, p_gate, h, *params)


LN_EPS = 1e-5
GN_EPS = 64e-5
HY_ORDER = 2
HY_TARGET = 1e-2
HY_SHORT_DECAY_PCT = 0.3
HY_LONG_DECAY_PCT = 1.5
N_EXPERTS = 16
EC_CAPACITY = 2


def _layer_norm(x, g, b):
    mu = jnp.mean(x, -1, keepdims=True)
    var = jnp.mean(jnp.square(x - mu), -1, keepdims=True)
    return (x - mu) * lax.rsqrt(var + LN_EPS) * g + b


def _filter_mlp_kernel(z_ref, t_ref, w1_ref, b1_ref, w2_ref, b2_ref, w3_ref, fr_ref, dl_ref, o_ref, ss_ref):
    fr = fr_ref[...]
    a = jnp.sin(fr * (_dg(z_ref[...].astype(BF16), w1_ref[...], _NN) + b1_ref[...]))
    for i in range(w2_ref.shape[0]):
        a = jnp.sin(fr * (_dg(a.astype(BF16), w2_ref[i], _NN) + b2_ref[i]))
    filt = _dg(a.astype(BF16), w3_ref[...], _NN) * jnp.exp(-t_ref[...] * dl_ref[...])
    o_ref[...] = filt
    ss_ref[0] = jnp.broadcast_to(jnp.sum(filt * filt, axis=0, keepdims=True), ss_ref.shape[1:])


def _hyena_filters(seq_len, w1, b1, w2, b2, w3, freq, d_hy):
    emb, fh = w1.shape
    bands = (emb - 1) // 2
    width = w3.shape[1]
    t = jnp.linspace(0.0, 1.0, seq_len, dtype=F32)[:, None]
    w = 2.0 * math.pi * jnp.arange(seq_len, dtype=F32)[:, None] / seq_len
    f = jnp.linspace(1e-4, bands - 1, bands, dtype=F32)[None, :]
    z = _pad_to(jnp.concatenate([t, jnp.cos(f * w), -jnp.sin(f * w)], axis=-1), LANES, 1)
    max_decay = math.log(HY_TARGET) / HY_SHORT_DECAY_PCT
    min_decay = math.log(HY_TARGET) / HY_LONG_DECAY_PCT
    deltas = jnp.abs(jnp.linspace(min_decay, max_decay, d_hy, dtype=F32))
    tl = ROW_TILE
    n_tiles = seq_len // tl
    full = lambda x: pl.BlockSpec(x.shape, lambda i: (0,) * x.ndim)
    params = [_pad_to(w1, LANES, 0).astype(BF16), b1.reshape(1, fh), w2.astype(BF16),
              b2.reshape(b2.shape[0], 1, fh), w3.astype(BF16), freq.reshape(1, fh),
              jnp.tile(deltas, width // d_hy).reshape(1, width)]
    filt, ss = pl.pallas_call(
        _filter_mlp_kernel,
        grid=(n_tiles,),
        in_specs=[pl.BlockSpec((tl, LANES), lambda i: (i, 0)), pl.BlockSpec((tl, 1), lambda i: (i, 0))]
        + [full(x) for x in params],
        out_specs=[pl.BlockSpec((tl, width), lambda i: (i, 0)), pl.BlockSpec((1, 8, width), lambda i: (i, 0, 0))],
        out_shape=[jax.ShapeDtypeStruct((seq_len, width), F32), jax.ShapeDtypeStruct((n_tiles, 8, width), F32)],
        compiler_params=pltpu.CompilerParams(dimension_semantics=("parallel",)),
        name="filter_mlp",
    )(z, t, *params)
    energy = jnp.sum(ss[:, 0, :], axis=0).reshape(HY_ORDER, 2, d_hy).sum(axis=1)
    return filt.reshape(seq_len, HY_ORDER, 2, d_hy), lax.rsqrt(energy + 1e-12)


def _hyena_branch(p, conv_w, conv_b, filt, scale, bias):
    seq_len = p.shape[1]
    v, x1, x2 = _shortconv(p, conv_w, conv_b)
    z = _long_conv_gated(v, x1, *_filter_spectrum(filt[:, 0], scale[0], seq_len), bias[0])
    return _long_conv_gated(z, x2, *_filter_spectrum(filt[:, 1], scale[1], seq_len), bias[1])


def kernel(x, ln0_g, ln0_b, w_in, hy_conv_w, hy_conv_b, hy_ffn_w1, hy_ffn_b1, hy_ffn_w2, hy_ffn_b2,
           hy_ffn_w3, hy_freq, hy_bias, rw_mu, rw_w0, rw_w2, rw_a0, rw_a2, rw_v0, rw_v1, rw_v2, rw_g2,
           rw_k_k, rw_k_a, rw_r_k, rw_lnx_g, rw_lnx_b, w_o_hy, w_o_rw, w_out, ln1_g, ln1_b,
           w_router, b_router, w_gate, w_up, w_down, ln2_g, ln2_b):
    bsz, seq_len, d_model = x.shape
    depth = w_in.shape[0]
    d_hy = hy_bias.shape[-1]
    d_rw = rw_w0.shape[-1]
    p_hy = 3 * d_hy
    p_rw = rw_mu.shape[-1]
    lora = p_rw - 3 * d_rw
    lora_pad = _round_up(lora, LANES)
    widths = (p_hy, 3 * d_rw, lora_pad, 2 * d_model)
    alpha = (2 * depth) ** 0.25
    cap = EC_CAPACITY * seq_len // N_EXPERTS
    tokens = bsz * seq_len
    h = _ln_rows(x.reshape(tokens, d_model), ln0_g, ln0_b).reshape(bsz, seq_len, d_model)
    v_first = None
    wg_all, wu_all, wd_all = w_gate.astype(BF16), w_up.astype(BF16), w_down.astype(BF16)
    for l in range(depth):
        c0, c1, c2 = p_hy, p_hy + 3 * d_rw, p_hy + p_rw
        w = jnp.concatenate([w_in[l][:, :c1], _pad_to(w_in[l][:, c1:c2], lora_pad, 1), w_in[l][:, c2:]], axis=1)
        pa, pb, pl_, pg = [t.reshape(bsz, seq_len, -1) for t in _proj(h.reshape(tokens, d_model), w, widths)]
        filt, scale = _hyena_filters(seq_len, hy_ffn_w1[l], hy_ffn_b1[l], hy_ffn_w2[l], hy_ffn_b2[l],
                                     hy_ffn_w3[l], hy_freq[l], d_hy)
        y_hy = _hyena_branch(pa, hy_conv_w[l], hy_conv_b[l], filt, scale, hy_bias[l])
        v_res = None if l == 0 else (rw_v0[l - 1], rw_v1[l - 1], rw_v2[l - 1])
        mu = rw_mu[l]
        r, k, v, kk, lw, a, g = _rwkv_prep(pb, pl_, v_first, v_res, mu[:, :3 * d_rw],
                                           _pad_to(mu[:, 3 * d_rw:], lora_pad, 1), rw_w0[l], rw_w2[l],
                                           rw_a0[l], rw_a2[l], rw_g2[l], rw_k_k[l])
        if l == 0:
            v_first = v
        y_f, y_b = wkv7(r, k, v, kk, lw, a, rw_k_a[l])
        h, aff_t = _mix(y_f, y_b, r, k, v, a, g, y_hy, pg, h, rw_k_a[l], rw_r_k[l], rw_lnx_g[l], rw_lnx_b[l],
                        w_o_hy[l], w_o_rw[l], w_out[l], ln1_g[l], ln1_b[l], w_router[l], b_router[l], alpha)
        idx, gate = _route(aff_t, cap)
        h = _moe_block(h, idx, gate, wg_all, wu_all, wd_all, l, ln2_g[l], ln2_b[l], alpha)
    return h
```

```python
import functools
import math

import jax
import jax.numpy as jnp
import numpy as np
from jax import lax
from jax.experimental import pallas as pl
from jax.experimental.pallas import tpu as pltpu

F32 = jnp.float32
BF16 = jnp.bfloat16

RW_HEAD = 64
LANES = 128
MXU_WIDTH = 256
WKV_CHUNK = 64
WKV_CHUNKS_PER_STEP = 8
VMEM_LIMIT = 56 * 1024 * 1024


def _split(x):
    hi = x.astype(BF16)
    lo = (x - hi.astype(F32)).astype(BF16)
    return hi, lo


def _dg(a, b, dims):
    return lax.dot_general(a, b, (dims, ((), ())), preferred_element_type=F32)


def _mm(a, b, dims=((1,), (0,))):
    return _dg(a.astype(BF16), b.astype(BF16), dims)


_NN = ((1,), (0,))
_NT = ((1,), (1,))
_TN = ((0,), (0,))


def _wkv_operands(r, k, v, kk, lw, a, ka, reverse):
    C = WKV_CHUNK
    row = lax.broadcasted_iota(jnp.int32, (C, C), 0)
    col = lax.broadcasted_iota(jnp.int32, (C, C), 1)
    incl = (col >= row) if reverse else (col <= row)

    lw_hi, lw_lo = _split(lw)
    tri = incl.astype(BF16)
    cin = _dg(tri, lw_hi, _NN) + _dg(tri, lw_lo, _NN)
    tot = jnp.sum(lw, axis=0, keepdims=True)
    half = 0.5 * tot
    e0 = jnp.exp(half)
    e1 = jnp.exp(cin - half)
    e2 = jnp.exp(half - cin)
    ew = jnp.exp(-lw)
    kdir = k * (1.0 + (a - 1.0) * ka)
    rt = r * e1
    bt = kk * (e1 * ew)
    kt = kdir * e2
    at = -(kk * a) * e2
    r0 = rt * e0
    b0 = bt * e0
    kh = kt * e0
    ah = at * e0
    return dict(bt=bt, rt=rt, kt=kt, at=at, b0=b0, r0=r0, kh=kh, ah=ah, e0sq=e0 * e0, v=v)


def _wkv_kernel(rf_ref, kf_ref, vf_ref, nf_ref, rb_ref, kb_ref, vb_ref, nb_ref, lwf_ref, lwb_ref,
                af_ref, ab_ref, ka_ref, yf_ref, yb_ref, h_ref):
    C = WKV_CHUNK

    @pl.when(pl.program_id(1) == 0)
    def _():
        h_ref[...] = jnp.zeros_like(h_ref)

    ka = ka_ref[...]
    n_pairs = ka.shape[1] // LANES
    outs = [yf_ref, yb_ref]

    row2 = lax.broadcasted_iota(jnp.int32, (C, 2 * C), 0)
    col2 = lax.broadcasted_iota(jnp.int32, (C, 2 * C), 1) % C
    earlier2 = [col2 < row2, col2 > row2]
    incl2 = [col2 <= row2, col2 >= row2]
    lane = lax.broadcasted_iota(jnp.int32, (1, LANES), 1)
    first = lane < RW_HEAD
    ri = lax.broadcasted_iota(jnp.int32, (LANES, LANES), 0)
    ci = lax.broadcasted_iota(jnp.int32, (LANES, LANES), 1)
    eye = ri == ci
    same_head = (ri // RW_HEAD) == (ci // RW_HEAD)

    def stack2(x):
        return jnp.concatenate([jnp.where(first, x, 0.0), jnp.where(first, 0.0, x)], axis=0)

    chains = [(d, p) for d in range(2) for p in range(n_pairs)]
    n = range(len(chains))
    hp = [h_ref[q] for q in n]
    n_sub = rf_ref.shape[1] // C
    for s in range(n_sub):
        fs = slice(s * C, (s + 1) * C)
        bs = slice((n_sub - 1 - s) * C, (n_sub - s) * C)
        rows = [fs, bs]
        ops = [_wkv_operands(rf_ref[0, fs, :], kf_ref[0, fs, :], vf_ref[0, fs, :], nf_ref[0, fs, :],
                             lwf_ref[0, 0, fs, :], af_ref[0, 0, fs, :], ka, False),
               _wkv_operands(rb_ref[0, bs, :], kb_ref[0, bs, :], vb_ref[0, bs, :], nb_ref[0, bs, :],
                             lwb_ref[0, 0, bs, :], ab_ref[0, 0, bs, :], ka, True)]

        def op(name, q):
            d, p = chains[q]
            return ops[d][name][:, p * LANES:(p + 1) * LANES]

        g = [_mm(jnp.concatenate([op("bt", q), op("rt", q)], axis=0),
                 jnp.concatenate([stack2(op("kt", q)), stack2(op("at", q))], axis=0), _NT) for q in n]
        bh = [_mm(jnp.concatenate([op("b0", q), op("r0", q)], axis=0), hp[q]) for q in n]
        v2 = [stack2(op("v", q)) for q in n]
        x = [bh[q][:C] + _mm(jnp.where(earlier2[chains[q][0]], g[q][:C, :2 * C], 0.0), v2[q]) for q in n]
        pw = [jnp.where(earlier2[chains[q][0]], g[q][:C, 2 * C:], 0.0) for q in n]
        n_steps = C.bit_length() - 1
        for i in range(n_steps):
            x = [x[q] + _mm(pw[q], stack2(x[q])) for q in n]
            if i + 1 < n_steps:
                pw = [_mm(pw[q], stack2(pw[q])) for q in n]
        for q in n:
            d, p = chains[q]
            a_r = jnp.concatenate([jnp.where(incl2[d], g[q][C:, :2 * C], 0.0),
                                   jnp.where(incl2[d], g[q][C:, 2 * C:], 0.0)], axis=1)
            outs[d][0, rows[d], p * LANES:(p + 1) * LANES] = bh[q][C:] + _mm(
                a_r, jnp.concatenate([v2[q], stack2(x[q])], axis=0))
        new = []
        for q in n:
            dg = jnp.where(eye, jnp.broadcast_to(op("e0sq", q), (LANES, LANES)), 0.0)
            hn = _mm(jnp.concatenate([op("kh", q), op("ah", q), dg], axis=0),
                     jnp.concatenate([op("v", q), x[q], hp[q]], axis=0), _TN)
            new.append(jnp.where(same_head, hn, 0.0))
        hp = new
    for q in n:
        h_ref[q] = hp[q]


def wkv7(r, k, v, kk, lw, a, k_a):
    B, S, D = r.shape
    C = WKV_CHUNK * WKV_CHUNKS_PER_STEP
    nc = S // C
    assert S % C == 0 and D % LANES == 0
    fwd = pl.BlockSpec((1, C, D), lambda b, c: (b, c, 0))
    bwd = pl.BlockSpec((1, C, D), lambda b, c: (b, nc - 1 - c, 0))
    fwd2 = pl.BlockSpec((1, 1, C, D), lambda b, c: (b, 0, c, 0))
    bwd2 = pl.BlockSpec((1, 1, C, D), lambda b, c: (b, 1, nc - 1 - c, 0))
    out = jax.ShapeDtypeStruct((B, S, D), F32)
    return pl.pallas_call(
        _wkv_kernel,
        grid=(B, nc),
        in_specs=[fwd] * 4 + [bwd] * 4 + [fwd2, bwd2, fwd2, bwd2, pl.BlockSpec((1, D), lambda b, c: (0, 0))],
        out_specs=[fwd, bwd],
        out_shape=[out, out],
        scratch_shapes=[pltpu.VMEM((2 * D // LANES, LANES, LANES), F32)],
        compiler_params=pltpu.CompilerParams(dimension_semantics=("parallel", "arbitrary")),
        name="wkv7_chunked",
    )(r, k, v, kk, r, k, v, kk, lw, lw, a, a, k_a.reshape(1, D))


FFT_N2 = 64
FFT_K1_BLOCK = 8


def _round_up(n, m):
    return (n + m - 1) // m * m


def _twice(m, axis):
    hi = jnp.asarray(m, F32).astype(BF16)
    return jnp.concatenate([hi, hi], axis=axis)


def _hilo(x):
    hi, lo = _split(x)
    return jnp.concatenate([hi, lo], axis=0)


@functools.lru_cache(maxsize=None)
def _fft_tables(seq_len):
    n = 2 * seq_len
    n2 = FFT_N2
    n1 = n // n2
    h1 = n1 // 2
    k1n = h1 + 1
    k1p = _round_up(k1n, FFT_K1_BLOCK)
    k1 = np.arange(k1n)[:, None]
    th1 = 2 * np.pi * k1 * np.arange(h1)[None, :] / n1
    f1 = np.zeros((2 * k1p, h1))
    f1[:k1n] = np.cos(th1)
    f1[k1p:k1p + k1n] = -np.sin(th1)
    m = np.arange(n2)
    th2 = 2 * np.pi * (m[None, None, :] * m[None, :, None] / n2 + m[None, None, :] * np.arange(k1n)[:, None, None] / n)
    gr, gi = np.cos(th2), -np.sin(th2)
    g = np.zeros((k1p, 2 * n2, 2 * n2))
    g[:k1n] = np.block([[gr, -gi], [gi, gr]])
    gt = np.transpose(g, (0, 2, 1))
    coef = np.full((k1n,), 2.0)
    coef[0] = 1.0
    coef[-1] = 1.0
    th3 = th1.T
    m3 = np.zeros((h1, 2 * k1p))
    m3[:, :k1n] = coef * np.cos(th3) / n
    m3[:, k1p:k1p + k1n] = -coef * np.sin(th3) / n
    th1f = 2 * np.pi * k1 * np.arange(n1)[None, :] / n1
    f1_full = np.zeros((2 * k1p, n1))
    f1_full[:k1n] = np.cos(th1f)
    f1_full[k1p:k1p + k1n] = -np.sin(th1f)
    return dict(n1=n1, h1=h1, k1n=k1n, k1p=k1p, f1=f1, g=g, gt=gt, m3=m3, f1_full=f1_full)


FFT_N2_BLOCK = 8


def _rows_to_lanes(ref):
    return jnp.concatenate([ref[0, :, i, :] for i in range(FFT_N2_BLOCK)], axis=1)


def _lanes_to_rows(ref, val):
    ch = ref.shape[3]
    for i in range(FFT_N2_BLOCK):
        ref[0, :, i, :] = val[:, i * ch:(i + 1) * ch]


def _fft_s1_kernel(f_ref, x_ref, o_ref):
    _lanes_to_rows(o_ref, jnp.dot(f_ref[...], _hilo(_rows_to_lanes(x_ref)), preferred_element_type=F32))


def _fft_mid_kernel(g_ref, gt_ref, kr_ref, ki_ref, a_ref, o_ref):
    n2 = FFT_N2
    for j in range(FFT_K1_BLOCK):
        xin = jnp.concatenate([a_ref[0, 0, j], a_ref[0, 1, j]], axis=0)
        z = jnp.dot(g_ref[j], _hilo(xin), preferred_element_type=F32)
        zr, zi = z[:n2], z[n2:]
        kr, ki = kr_ref[j], ki_ref[j]
        y = jnp.concatenate([zr * kr - zi * ki, zr * ki + zi * kr], axis=0)
        b = jnp.dot(gt_ref[j], _hilo(y), preferred_element_type=F32)
        o_ref[0, 0, j] = b[:n2]
        o_ref[0, 1, j] = b[n2:]


def _fft_fwd_kernel(g_ref, a_ref, s_ref, re_ref, im_ref):
    n2 = FFT_N2
    for j in range(FFT_K1_BLOCK):
        xin = jnp.concatenate([a_ref[0, 0, j], a_ref[0, 1, j]], axis=0)
        z = jnp.dot(g_ref[j], _hilo(xin), preferred_element_type=F32) * s_ref[...]
        re_ref[j] = z[:n2]
        im_ref[j] = z[n2:]


def _fft_s3_kernel(m_ref, b_ref, u_ref, x_ref, bias_ref, o_ref):
    y = jnp.dot(m_ref[...], _hilo(_rows_to_lanes(b_ref)), preferred_element_type=F32)
    _lanes_to_rows(o_ref, _rows_to_lanes(x_ref) * (y + _rows_to_lanes(u_ref) * bias_ref[...]))


def _long_conv_gated(u, gate, kf_re, kf_im, bias):
    bsz, seq_len, ch = u.shape
    t = _fft_tables(seq_len)
    n2, h1, k1p = FFT_N2, t["h1"], t["k1p"]
    lanes = n2 * ch
    nb = FFT_N2_BLOCK
    lt = nb * ch
    assert seq_len == h1 * n2 and ch % LANES == 0
    params = pltpu.CompilerParams(dimension_semantics=("parallel", "parallel"),
                                  vmem_limit_bytes=VMEM_LIMIT)
    spec4 = pl.BlockSpec((1, 2 * k1p, nb, ch), lambda b, j: (b, 0, j, 0))
    row = pl.BlockSpec((1, h1, nb, ch), lambda b, j: (b, 0, j, 0))
    f1 = _twice(t["f1"], 1)
    a = pl.pallas_call(
        _fft_s1_kernel,
        grid=(bsz, n2 // nb),
        in_specs=[pl.BlockSpec(f1.shape, lambda b, j: (0, 0)), row],
        out_specs=spec4,
        out_shape=jax.ShapeDtypeStruct((bsz, 2 * k1p, n2, ch), F32),
        compiler_params=params, name="hyena_dft1",
    )(f1, u.reshape(bsz, h1, n2, ch))
    kb = FFT_K1_BLOCK
    g = _twice(t["g"], 2)
    gt = _twice(t["gt"], 2)
    mat_spec = pl.BlockSpec((kb,) + g.shape[1:], lambda j, b: (j, 0, 0))
    kf_spec = pl.BlockSpec((kb, n2, ch), lambda j, b: (j, 0, 0))
    blk = pl.BlockSpec((1, 2, kb, n2, ch), lambda j, b: (b, 0, j, 0, 0))
    bm = pl.pallas_call(
        _fft_mid_kernel,
        grid=(k1p // kb, bsz),
        in_specs=[mat_spec, mat_spec, kf_spec, kf_spec, blk],
        out_specs=blk,
        out_shape=jax.ShapeDtypeStruct((bsz, 2, k1p, n2, ch), F32),
        compiler_params=params, name="hyena_dft2",
    )(g, gt, kf_re, kf_im, a.reshape(bsz, 2, k1p, n2, ch))
    m3 = _twice(t["m3"], 1)
    out = pl.pallas_call(
        _fft_s3_kernel,
        grid=(bsz, n2 // nb),
        in_specs=[pl.BlockSpec(m3.shape, lambda b, j: (0, 0)), spec4,
                  row, row, pl.BlockSpec((1, lt), lambda b, j: (0, j))],
        out_specs=row,
        out_shape=jax.ShapeDtypeStruct((bsz, h1, n2, ch), F32),
        compiler_params=params, name="hyena_dft3",
    )(m3, bm.reshape(bsz, 2 * k1p, n2, ch), u.reshape(bsz, h1, n2, ch), gate.reshape(bsz, h1, n2, ch),
      jnp.tile(bias, n2).reshape(1, lanes))
    return out.reshape(bsz, seq_len, ch)


def _filter_spectrum(filt_o, scale, seq_len):
    t = _fft_tables(seq_len)
    n1, n2, k1p = t["n1"], FFT_N2, t["k1p"]
    nb, kb = FFT_N2_BLOCK, FFT_K1_BLOCK
    ch = filt_o.shape[-1]
    k_full = jnp.concatenate([filt_o[:, 0], filt_o[::-1, 1]], axis=0)
    f1 = _twice(t["f1_full"], 1)
    params = pltpu.CompilerParams(dimension_semantics=("parallel", "parallel"),
                                  vmem_limit_bytes=VMEM_LIMIT)
    a = pl.pallas_call(
        _fft_s1_kernel,
        grid=(1, n2 // nb),
        in_specs=[pl.BlockSpec(f1.shape, lambda b, j: (0, 0)),
                  pl.BlockSpec((1, n1, nb, ch), lambda b, j: (b, 0, j, 0))],
        out_specs=pl.BlockSpec((1, 2 * k1p, nb, ch), lambda b, j: (b, 0, j, 0)),
        out_shape=jax.ShapeDtypeStruct((1, 2 * k1p, n2, ch), F32),
        compiler_params=params, name="filter_dft1",
    )(f1, k_full.reshape(1, n1, n2, ch))
    g = _twice(t["g"], 2)
    kf_spec = pl.BlockSpec((kb, n2, ch), lambda b, j: (j, 0, 0))
    out = jax.ShapeDtypeStruct((k1p, n2, ch), F32)
    return pl.pallas_call(
        _fft_fwd_kernel,
        grid=(1, k1p // kb),
        in_specs=[pl.BlockSpec((kb,) + g.shape[1:], lambda b, j: (j, 0, 0)),
                  pl.BlockSpec((1, 2, kb, n2, ch), lambda b, j: (b, 0, j, 0, 0)),
                  pl.BlockSpec((1, ch), lambda b, j: (0, 0))],
        out_specs=[kf_spec, kf_spec],
        out_shape=[out, out],
        compiler_params=params, name="filter_dft2",
    )(g, a.reshape(1, 2, k1p, n2, ch), scale.reshape(1, ch))


ROUTE_COLS = 16
ROUTE_TOKEN_CHUNK = 512


def _route_kernel(aff_ref, o_ref, key_ref, val_ref, *, cap):
    a = aff_ref[0]
    n_exp, seq_len = a.shape
    bits = pltpu.bitcast(a, jnp.int32)

    def bisect(i, t):
        cand = t | jnp.left_shift(1, 30 - i)
        cnt = jnp.sum((bits >= cand).astype(jnp.int32), axis=1, keepdims=True)
        return jnp.where(cnt >= cap, cand, t)

    thr = lax.fori_loop(0, 31, bisect, jnp.zeros((n_exp, 1), jnp.int32))
    gt = bits > thr
    eq = bits == thr

    ri = lax.broadcasted_iota(jnp.int32, (LANES, LANES), 0)
    ci = lax.broadcasted_iota(jnp.int32, (LANES, LANES), 1)
    upper = (ri < ci).astype(BF16)

    def prefix(mask):
        m = mask.astype(F32)
        run = jnp.zeros((n_exp, 1), F32)
        parts = []
        for j in range(seq_len // LANES):
            tile = m[:, j * LANES:(j + 1) * LANES]
            parts.append(run + jnp.dot(tile.astype(BF16), upper, preferred_element_type=F32))
            run = run + jnp.sum(tile, axis=1, keepdims=True)
        return jnp.concatenate(parts, axis=1), run

    eq_before, _ = prefix(eq)
    n_gt = jnp.sum(gt.astype(F32), axis=1, keepdims=True)
    sel = jnp.logical_or(gt, jnp.logical_and(eq, eq_before < cap - n_gt))
    pos, _ = prefix(sel)
    key_ref[...] = jnp.where(sel, pos, -1.0)
    val_ref[...] = a

    tok = lax.broadcasted_iota(jnp.int32, (1, seq_len), 1)
    t_hi = (tok // 64).astype(F32)
    t_lo = (tok % 64).astype(F32)
    slot = lax.broadcasted_iota(jnp.int32, (cap, ROUTE_TOKEN_CHUNK), 0).astype(F32)
    zeros = jnp.zeros((ROUTE_COLS - 5, seq_len), F32)

    def compact(e, carry):
        key = key_ref[pl.ds(e, 1), :]
        g = val_ref[pl.ds(e, 1), :]
        g_hi = g.astype(BF16).astype(F32)
        g_mid = (g - g_hi).astype(BF16).astype(F32)
        g_lo = g - g_hi - g_mid
        rows = jnp.concatenate([t_hi, t_lo, g_hi, g_mid, g_lo, zeros], axis=0).astype(BF16)
        acc = jnp.zeros((cap, ROUTE_COLS), F32)
        for j in range(seq_len // ROUTE_TOKEN_CHUNK):
            sl = slice(j * ROUTE_TOKEN_CHUNK, (j + 1) * ROUTE_TOKEN_CHUNK)
            onehot = jnp.where(key[:, sl] == slot, 1.0, 0.0).astype(BF16)
            acc = acc + _dg(onehot, rows[:, sl], _NT)
        o_ref[0, pl.ds(e, 1)] = acc[None]
        return carry

    lax.fori_loop(0, n_exp, compact, 0)


def _route(aff_t, cap):
    bsz, n_exp, seq_len = aff_t.shape
    assert seq_len % ROUTE_TOKEN_CHUNK == 0 and seq_len <= 64 * 256
    out = pl.pallas_call(
        functools.partial(_route_kernel, cap=cap),
        grid=(bsz,),
        in_specs=[pl.BlockSpec((1, n_exp, seq_len), lambda b: (b, 0, 0))],
        out_specs=pl.BlockSpec((1, n_exp, cap, ROUTE_COLS), lambda b: (b, 0, 0, 0)),
        out_shape=jax.ShapeDtypeStruct((bsz, n_exp, cap, ROUTE_COLS), F32),
        scratch_shapes=[pltpu.VMEM((n_exp, seq_len), F32), pltpu.VMEM((n_exp, seq_len), F32)],
        compiler_params=pltpu.CompilerParams(dimension_semantics=("parallel",),
                                             vmem_limit_bytes=VMEM_LIMIT),
        name="ec_route",
    )(aff_t)
    idx = (out[..., 0] * 64.0 + out[..., 1]).astype(jnp.int32)
    gate = out[..., 2] + out[..., 3] + out[..., 4]
    return idx, gate


MOE_FF_TILE = 1024
MOE_GROUP = 16
LN_ROWS = 256


def _gather_rows(h_ref, idx_ref, xb_ref, base):
    rows = [h_ref[0, pl.ds(idx_ref[0, 0, base + j], 1), :] for j in range(MOE_GROUP)]
    xb_ref[pl.ds(base, MOE_GROUP), :] = jnp.concatenate(rows, axis=0).astype(BF16)


def _scatter_rows(o_ref, idx_ref, gate_ref, y_ref, base, enabled):
    y8 = y_ref[pl.ds(base, 8), :]
    toks = [idx_ref[0, 0, base + j] for j in range(8)]
    rows = [o_ref[0, pl.ds(toks[j], 1), :] + jnp.where(enabled, gate_ref[0, 0, base + j], 0.0) * y8[j:j + 1, :]
            for j in range(8)]
    for j in range(8):
        o_ref[0, pl.ds(toks[j], 1), :] = rows[j]


def _moe_kernel(idx_ref, gate_ref, idxn_ref, idxp_ref, gatep_ref, h_ref, wg_ref, wu_ref, wd_ref,
                lg_ref, lb_ref, o_ref, xb_ref, y_ref, *, alpha, n_f):
    e = pl.program_id(1)
    f = pl.program_id(2)
    last_e = pl.num_programs(1) - 1
    last_f = n_f - 1
    cap = xb_ref.shape[1]
    seq_len = h_ref.shape[1]
    share = cap // n_f
    slot = e % 2
    other = 1 - slot
    first_step = jnp.logical_and(e == 0, f == 0)

    @pl.when(first_step)
    def _():
        o_ref[...] = jnp.zeros_like(o_ref)
        y_ref[...] = jnp.zeros_like(y_ref)

        def gather(i, carry):
            _gather_rows(h_ref, idx_ref, xb_ref.at[0], pl.multiple_of(i * MOE_GROUP, MOE_GROUP))
            return carry

        lax.fori_loop(0, cap // MOE_GROUP, gather, 0)

    xb = xb_ref[slot]
    a = jnp.dot(xb, wg_ref[0], preferred_element_type=F32)
    u = jnp.dot(xb, wu_ref[0], preferred_element_type=F32)
    act = (a * jax.nn.sigmoid(a) * u).astype(BF16)
    part = jnp.dot(act, wd_ref[0], preferred_element_type=F32)

    for i in range(share // MOE_GROUP):
        _gather_rows(h_ref, idxn_ref, xb_ref.at[other], pl.multiple_of(f * share + i * MOE_GROUP, MOE_GROUP))
    for i in range(share // 8):
        _scatter_rows(o_ref, idxp_ref, gatep_ref, y_ref.at[other], pl.multiple_of(f * share + i * 8, 8), e > 0)

    @pl.when(f == 0)
    def _():
        y_ref[slot] = part

    @pl.when(f > 0)
    def _():
        y_ref[slot] += part

    @pl.when(jnp.logical_and(e == last_e, f == last_f))
    def _():
        def scatter(i, carry):
            _scatter_rows(o_ref, idx_ref, gate_ref, y_ref.at[slot], pl.multiple_of(i * 8, 8), True)
            return carry

        lax.fori_loop(0, cap // 8, scatter, 0)

    @pl.when(jnp.logical_and(e == last_e, f == last_f))
    def _():
        def norm(i, carry):
            rows = pl.ds(pl.multiple_of(i * LN_ROWS, LN_ROWS), LN_ROWS)
            z = alpha * h_ref[0, rows, :] + o_ref[0, rows, :]
            o_ref[0, rows, :] = _layer_norm(z, lg_ref[...], lb_ref[...])
            return carry

        lax.fori_loop(0, seq_len // LN_ROWS, norm, 0)


def _moe_block(h, idx, gate, wg, wu, wd, layer, ln_g, ln_b, alpha):
    bsz, seq_len, d = h.shape
    _, n_exp, _, ff = wg.shape
    cap = idx.shape[-1]
    ft = min(MOE_FF_TILE, ff)
    assert ff % ft == 0 and seq_len % LN_ROWS == 0
    n_f = ff // ft
    assert cap % (n_f * MOE_GROUP) == 0

    def smem(shift):
        def index(b, e, f):
            return (b * n_exp + jnp.clip(e + shift, 0, n_exp - 1), 0, 0)
        return pl.BlockSpec((1, 1, cap), index, memory_space=pltpu.SMEM)

    whole = pl.BlockSpec((1, seq_len, d), lambda b, e, f: (b, 0, 0), pipeline_mode=pl.Buffered(1))
    vec = pl.BlockSpec((1, d), lambda b, e, f: (0, 0))
    idx3 = idx.reshape(bsz * n_exp, 1, cap)
    gate3 = gate.reshape(bsz * n_exp, 1, cap)
    return pl.pallas_call(
        functools.partial(_moe_kernel, alpha=alpha, n_f=n_f),
        grid=(bsz, n_exp, n_f),
        in_specs=[smem(0), smem(0), smem(1), smem(-1), smem(-1), whole,
                  pl.BlockSpec((None, 1, d, ft), lambda b, e, f: (layer, e, 0, f)),
                  pl.BlockSpec((None, 1, d, ft), lambda b, e, f: (layer, e, 0, f)),
                  pl.BlockSpec((None, 1, ft, d), lambda b, e, f: (layer, e, f, 0)),
                  vec, vec],
        out_specs=whole,
        out_shape=jax.ShapeDtypeStruct((bsz, seq_len, d), F32),
        scratch_shapes=[pltpu.VMEM((2, cap, d), BF16), pltpu.VMEM((2, cap, d), F32)],
        compiler_params=pltpu.CompilerParams(
            dimension_semantics=("arbitrary", "arbitrary", "arbitrary"), vmem_limit_bytes=VMEM_LIMIT),
        name="ec_moe",
    )(idx3, gate3, idx3, idx3, gate3, h, wg, wu, wd, ln_g.reshape(1, d), ln_b.reshape(1, d))


ROW_TILE = 512
MIX_TILE = 512


def _proj_kernel(x_ref, w_ref, *o_refs):
    xb = x_ref[...].astype(BF16)
    off = 0
    for o_ref in o_refs:
        n = o_ref.shape[1]
        o_ref[...] = jnp.dot(xb, w_ref[:, off:off + n], preferred_element_type=F32)
        off += n


def _proj(x, w, widths):
    M, K = x.shape
    tm = ROW_TILE
    assert M % tm == 0 and sum(widths) == w.shape[1] and all(n % LANES == 0 for n in widths)
    return pl.pallas_call(
        _proj_kernel,
        grid=(M // tm,),
        in_specs=[pl.BlockSpec((tm, K), lambda i: (i, 0)),
                  pl.BlockSpec(w.shape, lambda i: (0, 0), pipeline_mode=pl.Buffered(1))],
        out_specs=[pl.BlockSpec((tm, n), lambda i: (i, 0)) for n in widths],
        out_shape=[jax.ShapeDtypeStruct((M, n), F32) for n in widths],
        compiler_params=pltpu.CompilerParams(dimension_semantics=("parallel",),
                                             vmem_limit_bytes=VMEM_LIMIT),
        name="proj",
    )(x, w.astype(BF16))


def _pad_to(w, n, axis):
    pad = [(0, 0)] * w.ndim
    pad[axis] = (0, n - w.shape[axis])
    return jnp.pad(w, pad)


def _ln_kernel(x_ref, g_ref, b_ref, o_ref):
    o_ref[...] = _layer_norm(x_ref[...], g_ref[...], b_ref[...])


def _ln_rows(x, g, b):
    M, D = x.shape
    row = pl.BlockSpec((ROW_TILE, D), lambda i: (i, 0))
    vec = pl.BlockSpec((1, D), lambda i: (0, 0))
    return pl.pallas_call(
        _ln_kernel, grid=(M // ROW_TILE,), in_specs=[row, vec, vec], out_specs=row,
        out_shape=jax.ShapeDtypeStruct((M, D), F32),
        compiler_params=pltpu.CompilerParams(dimension_semantics=("parallel",)),
        name="ln_in",
    )(x, g.reshape(1, D), b.reshape(1, D))


def _halo_specs(ts, width, seq_len):
    per = ts // 8
    last = seq_len // 8 - 1
    cur = pl.BlockSpec((1, ts, width), lambda b, i: (b, i, 0))
    prev = pl.BlockSpec((1, 8, width), lambda b, i: (b, jnp.maximum(i * per - 1, 0), 0))
    nxt = pl.BlockSpec((1, 8, width), lambda b, i: (b, jnp.minimum((i + 1) * per, last), 0))
    return [cur, prev, nxt]


def _neighbours(x, prev_blk, next_blk):
    ts = x.shape[0]
    i = pl.program_id(1)
    rid = lax.broadcasted_iota(jnp.int32, x.shape, 0)
    prev_row = jnp.where(i > 0, prev_blk[7:8, :], 0.0)
    next_row = jnp.where(i < pl.num_programs(1) - 1, next_blk[0:1, :], 0.0)
    zp = jnp.where(rid == 0, prev_row, pltpu.roll(x, 1, 0))
    zn = jnp.where(rid == ts - 1, next_row, pltpu.roll(x, ts - 1, 0))
    return zp, zn


def _shortconv_kernel(p_ref, pp_ref, pn_ref, w_ref, b_ref, v_ref, x1_ref, x2_ref):
    p = p_ref[0]
    zp, zn = _neighbours(p, pp_ref[0], pn_ref[0])
    q = w_ref[0:1, :] * zp + w_ref[1:2, :] * p + w_ref[2:3, :] * zn + b_ref[...]
    d = v_ref.shape[2]
    v_ref[0] = q[:, :d]
    x1_ref[0] = q[:, d:2 * d]
    x2_ref[0] = q[:, 2 * d:]


def _shortconv(p, conv_w, conv_b):
    bsz, seq_len, width = p.shape
    d = width // 3
    ts = ROW_TILE
    out = pl.BlockSpec((1, ts, d), lambda b, i: (b, i, 0))
    return pl.pallas_call(
        _shortconv_kernel,
        grid=(bsz, seq_len // ts),
        in_specs=_halo_specs(ts, width, seq_len) + [pl.BlockSpec((3, width), lambda b, i: (0, 0)),
                                                     pl.BlockSpec((1, width), lambda b, i: (0, 0))],
        out_specs=[out, out, out],
        out_shape=[jax.ShapeDtypeStruct((bsz, seq_len, d), F32)] * 3,
        compiler_params=pltpu.CompilerParams(dimension_semantics=("parallel", "parallel")),
        name="hyena_shortconv",
    )(p, p, p, conv_w, conv_b.reshape(1, width))


def _head_sum(x, bd):
    hi, lo = _split(x)
    w = MXU_WIDTH
    parts = [_dg(hi[:, j:j + w], bd[j:j + w, j:j + w], _NN) + _dg(lo[:, j:j + w], bd[j:j + w, j:j + w], _NN)
             for j in range(0, x.shape[1], w)]
    return jnp.concatenate(parts, axis=1)


def _rwkv_prep_kernel(*refs, has_res):
    (p_ref, pp_ref, pn_ref, q_ref, qp_ref, qn_ref) = refs[:6]
    refs = refs[6:]
    if has_res:
        vf_ref, refs = refs[0], refs[1:]
    (mu_ref, mul_ref, w2_ref, w0_ref, a2_ref, a0_ref, g2_ref, kk_ref, v0_ref, v1_ref, v2_ref, bd_ref,
     r_out, k_out, v_out, kn_out, lw_out, a_out, g_out) = refs
    d = r_out.shape[2]
    p = p_ref[0]
    zp, zn = _neighbours(p, pp_ref[0], pn_ref[0])
    p = p + mu_ref[0:1, :] * (zp - p) + mu_ref[1:2, :] * (zn - p)
    q = q_ref[0]
    zp, zn = _neighbours(q, qp_ref[0], qn_ref[0])
    q = q + mul_ref[0:1, :] * (zp - q) + mul_ref[1:2, :] * (zn - q)
    r, k, v = p[:, :d], p[:, d:2 * d], p[:, 2 * d:]
    if has_res:
        mix = _dg(_dg(v.astype(BF16), v1_ref[...], _NN).astype(BF16), v2_ref[...], _NN)
        v = v + (vf_ref[0] - v) * jax.nn.sigmoid(v0_ref[...] + mix)
    wd, ad, gd = q[:, :LANES], q[:, LANES:2 * LANES], q[:, 2 * LANES:]
    wpre = w0_ref[...] + _dg(jnp.tanh(wd).astype(BF16), w2_ref[...], _NN)
    lw = -math.exp(-0.5) * jax.nn.sigmoid(wpre)
    a = jax.nn.sigmoid(a0_ref[...] + _dg(ad.astype(BF16), a2_ref[...], _NN))
    g = _dg(jax.nn.sigmoid(gd).astype(BF16), g2_ref[...], _NN)
    kq = k * kk_ref[...]
    norm = jnp.sqrt(_head_sum(kq * kq, bd_ref[...]))
    r_out[0] = r
    k_out[0] = k
    v_out[0] = v
    kn_out[0] = kq / jnp.maximum(norm, 1e-12)
    lw_out[0, 0] = lw[:, :d]
    lw_out[0, 1] = lw[:, d:]
    a_out[0, 0] = a[:, :d]
    a_out[0, 1] = a[:, d:]
    g_out[0] = g


def _block_diag2(m):
    z = jnp.zeros_like(m[0])
    return jnp.concatenate([jnp.concatenate([m[0], z], axis=1), jnp.concatenate([z, m[1]], axis=1)], axis=0)


def _head_ones(d):
    i = np.arange(d) // RW_HEAD
    return jnp.asarray(i[:, None] == i[None, :], BF16)


def _rwkv_prep(p_rkv, p_lora, v_first, v_res, mu_rkv, mu_lora, w0, w2, a0, a2, g2, k_k):
    bsz, seq_len, w3 = p_rkv.shape
    d = w3 // 3
    wl = p_lora.shape[2]
    ts = ROW_TILE
    assert 2 * w2.shape[1] == LANES and 2 * a2.shape[1] == LANES
    has_res = v_res is not None
    const = lambda shape: pl.BlockSpec(shape, lambda b, i: (0,) * len(shape))
    tile = pl.BlockSpec((1, ts, d), lambda b, i: (b, i, 0))
    pair = pl.BlockSpec((1, 2, ts, d), lambda b, i: (b, 0, i, 0))
    if has_res:
        v0, v1, v2 = v_res
        rank = _round_up(v1.shape[1], LANES)
        v0, v1, v2 = v0.reshape(1, d), _pad_to(v1, rank, 1).astype(BF16), _pad_to(v2, rank, 0).astype(BF16)
    else:
        v0, v1, v2 = jnp.zeros((1, d), F32), jnp.zeros((d, LANES), BF16), jnp.zeros((LANES, d), BF16)
    params = [mu_rkv, mu_lora, _block_diag2(w2).astype(BF16), w0.reshape(1, 2 * d),
              _block_diag2(a2).astype(BF16), a0.reshape(1, 2 * d),
              _pad_to(g2, wl - 2 * LANES, 0).astype(BF16), k_k.reshape(1, d), v0, v1, v2, _head_ones(d)]
    args = [p_rkv, p_rkv, p_rkv, p_lora, p_lora, p_lora] + ([v_first] if has_res else []) + params
    tok = jax.ShapeDtypeStruct((bsz, seq_len, d), F32)
    two = jax.ShapeDtypeStruct((bsz, 2, seq_len, d), F32)
    return pl.pallas_call(
        functools.partial(_rwkv_prep_kernel, has_res=has_res),
        grid=(bsz, seq_len // ts),
        in_specs=(_halo_specs(ts, w3, seq_len) + _halo_specs(ts, wl, seq_len) + ([tile] if has_res else [])
                  + [const(x.shape) for x in params]),
        out_specs=[tile, tile, tile, tile, pair, pair, tile],
        out_shape=[tok, tok, tok, tok, two, two, tok],
        compiler_params=pltpu.CompilerParams(dimension_semantics=("parallel", "parallel"),
                                             vmem_limit_bytes=VMEM_LIMIT),
        name="rwkv_prep",
    )(*args)


def _mix_kernel(yf_ref, yb_ref, r_ref, k_ref, v_ref, a_ref, g_ref, yh_ref, pg_ref, h_ref,
                ka_ref, rk_ref, lg_ref, lb_ref, bd_ref, why_ref, wrw_ref, wo_ref, n1g_ref, n1b_ref,
                wr_ref, br_ref, h_out, aff_out, *, alpha, n_exp):
    d_model = h_ref.shape[2]
    bd = bd_ref[...]
    inv = 1.0 / RW_HEAD
    y = yf_ref[0] + yb_ref[0]
    mu = _head_sum(y, bd) * inv
    dy = y - mu
    var = _head_sum(dy * dy, bd) * inv
    yn = dy * lax.rsqrt(var + GN_EPS) * lg_ref[...] + lb_ref[...]
    k = k_ref[0]
    ka = ka_ref[...]
    k_sum = k * (1.0 + (a_ref[0, 0] - 1.0) * ka) + k * (1.0 + (a_ref[0, 1] - 1.0) * ka)
    bonus = _head_sum(r_ref[0] * k_sum * rk_ref[...], bd) * v_ref[0]
    y_rw = (yn + bonus) * g_ref[0]
    o_hy = _dg(yh_ref[0].astype(BF16), why_ref[...], _NN)
    o_rw = _dg(y_rw.astype(BF16), wrw_ref[...], _NN)
    gates = jax.nn.sigmoid(pg_ref[0])
    merged = gates[:, :d_model] * o_hy + gates[:, d_model:] * o_rw
    mo = _dg(merged.astype(BF16), wo_ref[...], _NN)
    h1 = _layer_norm(alpha * h_ref[0] + mo, n1g_ref[...], n1b_ref[...])
    h_out[0] = h1
    logits = _dg(h1.astype(BF16), wr_ref[...], _NN) + br_ref[...]
    col = lax.broadcasted_iota(jnp.int32, logits.shape, 1)
    logits = jnp.where(col < n_exp, logits, -1e30)
    ex = jnp.exp(logits - jnp.max(logits, axis=1, keepdims=True))
    aff = ex / jnp.sum(ex, axis=1, keepdims=True)
    aff_out[0] = jnp.transpose(aff)[:n_exp, :]


def _mix(y_f, y_b, r, k, v, a, g, y_hy, p_gate, h, k_a, r_k, lnx_g, lnx_b, w_o_hy, w_o_rw, w_out, ln_g, ln_b,
         w_router, b_router, alpha):
    bsz, seq_len, d = r.shape
    d_model = h.shape[2]
    n_exp = w_router.shape[1]
    ts = MIX_TILE
    tile = lambda w: pl.BlockSpec((1, ts, w), lambda b, i: (b, i, 0))
    const = lambda shape: pl.BlockSpec(shape, lambda b, i: (0,) * len(shape))
    vec = lambda x: x.reshape(1, -1)
    params = [vec(k_a), vec(r_k), vec(lnx_g), vec(lnx_b), _head_ones(d), w_o_hy.astype(BF16),
              w_o_rw.astype(BF16), w_out.astype(BF16), vec(ln_g), vec(ln_b),
              _pad_to(w_router, LANES, 1).astype(BF16), _pad_to(vec(b_router), LANES, 1)]
    return pl.pallas_call(
        functools.partial(_mix_kernel, alpha=alpha, n_exp=n_exp),
        grid=(bsz, seq_len // ts),
        in_specs=[tile(d), tile(d),
                  tile(d), tile(d), tile(d), pl.BlockSpec((1, 2, ts, d), lambda b, i: (b, 0, i, 0)),
                  tile(d), tile(d), tile(2 * d_model), tile(d_model)] + [const(x.shape) for x in params],
        out_specs=[tile(d_model), pl.BlockSpec((1, n_exp, ts), lambda b, i: (b, 0, i))],
        out_shape=[jax.ShapeDtypeStruct((bsz, seq_len, d_model), F32),
                   jax.ShapeDtypeStruct((bsz, n_exp, seq_len), F32)],
        compiler_params=pltpu.CompilerParams(dimension_semantics=("parallel", "parallel"),
                                             vmem_limit_bytes=VMEM_LIMIT),
        name="mix_out",
    )(y_f, y_b, r, k, v, a, g, y_hy, p_gate, h, *params)


LN_EPS = 1e-5
GN_EPS = 64e-5
HY_ORDER = 2
HY_TARGET = 1e-2
HY_SHORT_DECAY_PCT = 0.3
HY_LONG_DECAY_PCT = 1.5
N_EXPERTS = 16
EC_CAPACITY = 2


def _layer_norm(x, g, b):
    mu = jnp.mean(x, -1, keepdims=True)
    var = jnp.mean(jnp.square(x - mu), -1, keepdims=True)
    return (x - mu) * lax.rsqrt(var + LN_EPS) * g + b


def _filter_mlp_kernel(z_ref, t_ref, w1_ref, b1_ref, w2_ref, b2_ref, w3_ref, fr_ref, dl_ref, o_ref, ss_ref):
    fr = fr_ref[...]
    a = jnp.sin(fr * (_dg(z_ref[...].astype(BF16), w1_ref[...], _NN) + b1_ref[...]))
    for i in range(w2_ref.shape[0]):
        a = jnp.sin(fr * (_dg(a.astype(BF16), w2_ref[i], _NN) + b2_ref[i]))
    filt = _dg(a.astype(BF16), w3_ref[...], _NN) * jnp.exp(-t_ref[...] * dl_ref[...])
    o_ref[...] = filt
    ss_ref[0] = jnp.broadcast_to(jnp.sum(filt * filt, axis=0, keepdims=True), ss_ref.shape[1:])


def _hyena_filters(seq_len, w1, b1, w2, b2, w3, freq, d_hy):
    emb, fh = w1.shape
    bands = (emb - 1) // 2
    width = w3.shape[1]
    t = jnp.linspace(0.0, 1.0, seq_len, dtype=F32)[:, None]
    w = 2.0 * math.pi * jnp.arange(seq_len, dtype=F32)[:, None] / seq_len
    f = jnp.linspace(1e-4, bands - 1, bands, dtype=F32)[None, :]
    z = _pad_to(jnp.concatenate([t, jnp.cos(f * w), -jnp.sin(f * w)], axis=-1), LANES, 1)
    max_decay = math.log(HY_TARGET) / HY_SHORT_DECAY_PCT
    min_decay = math.log(HY_TARGET) / HY_LONG_DECAY_PCT
    deltas = jnp.abs(jnp.linspace(min_decay, max_decay, d_hy, dtype=F32))
    tl = ROW_TILE
    n_tiles = seq_len // tl
    full = lambda x: pl.BlockSpec(x.shape, lambda i: (0,) * x.ndim)
    params = [_pad_to(w1, LANES, 0).astype(BF16), b1.reshape(1, fh), w2.astype(BF16),
              b2.reshape(b2.shape[0], 1, fh), w3.astype(BF16), freq.reshape(1, fh),
              jnp.tile(deltas, width // d_hy).reshape(1, width)]
    filt, ss = pl.pallas_call(
        _filter_mlp_kernel,
        grid=(n_tiles,),
        in_specs=[pl.BlockSpec((tl, LANES), lambda i: (i, 0)), pl.BlockSpec((tl, 1), lambda i: (i, 0))]
        + [full(x) for x in params],
        out_specs=[pl.BlockSpec((tl, width), lambda i: (i, 0)), pl.BlockSpec((1, 8, width), lambda i: (i, 0, 0))],
        out_shape=[jax.ShapeDtypeStruct((seq_len, width), F32), jax.ShapeDtypeStruct((n_tiles, 8, width), F32)],
        compiler_params=pltpu.CompilerParams(dimension_semantics=("parallel",)),
        name="filter_mlp",
    )(z, t, *params)
    energy = jnp.sum(ss[:, 0, :], axis=0).reshape(HY_ORDER, 2, d_hy).sum(axis=1)
    return filt.reshape(seq_len, HY_ORDER, 2, d_hy), lax.rsqrt(energy + 1e-12)


def _hyena_branch(p, conv_w, conv_b, filt, scale, bias):
    seq_len = p.shape[1]
    v, x1, x2 = _shortconv(p, conv_w, conv_b)
    z = _long_conv_gated(v, x1, *_filter_spectrum(filt[:, 0], scale[0], seq_len), bias[0])
    return _long_conv_gated(z, x2, *_filter_spectrum(filt[:, 1], scale[1], seq_len), bias[1])


def kernel(x, ln0_g, ln0_b, w_in, hy_conv_w, hy_conv_b, hy_ffn_w1, hy_ffn_b1, hy_ffn_w2, hy_ffn_b2,
           hy_ffn_w3, hy_freq, hy_bias, rw_mu, rw_w0, rw_w2, rw_a0, rw_a2, rw_v0, rw_v1, rw_v2, rw_g2,
           rw_k_k, rw_k_a, rw_r_k, rw_lnx_g, rw_lnx_b, w_o_hy, w_o_rw, w_out, ln1_g, ln1_b,
           w_router, b_router, w_gate, w_up, w_down, ln2_g, ln2_b):
    bsz, seq_len, d_model = x.shape
    depth = w_in.shape[0]
    d_hy = hy_bias.shape[-1]
    d_rw = rw_w0.shape[-1]
    p_hy = 3 * d_hy
    p_rw = rw_mu.shape[-1]
    lora = p_rw - 3 * d_rw
    lora_pad = _round_up(lora, LANES)
    widths = (p_hy, 3 * d_rw, lora_pad, 2 * d_model)
    alpha = (2 * depth) ** 0.25
    cap = EC_CAPACITY * seq_len // N_EXPERTS
    tokens = bsz * seq_len
    h = _ln_rows(x.reshape(tokens, d_model), ln0_g, ln0_b).reshape(bsz, seq_len, d_model)
    v_first = None
    wg_all, wu_all, wd_all = w_gate.astype(BF16), w_up.astype(BF16), w_down.astype(BF16)
    for l in range(depth):
        c0, c1, c2 = p_hy, p_hy + 3 * d_rw, p_hy + p_rw
        w = jnp.concatenate([w_in[l][:, :c1], _pad_to(w_in[l][:, c1:c2], lora_pad, 1), w_in[l][:, c2:]], axis=1)
        pa, pb, pl_, pg = [t.reshape(bsz, seq_len, -1) for t in _proj(h.reshape(tokens, d_model), w, widths)]
        filt, scale = _hyena_filters(seq_len, hy_ffn_w1[l], hy_ffn_b1[l], hy_ffn_w2[l], hy_ffn_b2[l],
                                     hy_ffn_w3[l], hy_freq[l], d_hy)
        y_hy = _hyena_branch(pa, hy_conv_w[l], hy_conv_b[l], filt, scale, hy_bias[l])
        v_res = None if l == 0 else (rw_v0[l - 1], rw_v1[l - 1], rw_v2[l - 1])
        mu = rw_mu[l]
        r, k, v, kk, lw, a, g = _rwkv_prep(pb, pl_, v_first, v_res, mu[:, :3 * d_rw],
                                           _pad_to(mu[:, 3 * d_rw:], lora_pad, 1), rw_w0[l], rw_w2[l],
                                           rw_a0[l], rw_a2[l], rw_g2[l], rw_k_k[l])
        if l == 0:
            v_first = v
        y_f, y_b = wkv7(r, k, v, kk, lw, a, rw_k_a[l])
        h, aff_t = _mix(y_f, y_b, r, k, v, a, g, y_hy, pg, h, rw_k_a[l], rw_r_k[l], rw_lnx_g[l], rw_lnx_b[l],
                        w_o_hy[l], w_o_rw[l], w_out[l], ln1_g[l], ln1_b[l], w_router[l], b_router[l], alpha)
        idx, gate = _route(aff_t, cap)
        h = _moe_block(h, idx, gate, wg_all, wu_all, wd_all, l, ln2_g[l], ln2_b[l], alpha)
    return h
```

```python
import functools
import math

import jax
import jax.numpy as jnp
import numpy as np
from jax import lax
from jax.experimental import pallas as pl
from jax.experimental.pallas import tpu as pltpu

F32 = jnp.float32
BF16 = jnp.bfloat16

RW_HEAD = 64
LANES = 128
MXU_WIDTH = 256
WKV_CHUNK = 64
WKV_CHUNKS_PER_STEP = 8
VMEM_LIMIT = 56 * 1024 * 1024


def _split(x):
    hi = x.astype(BF16)
    lo = (x - hi.astype(F32)).astype(BF16)
    return hi, lo


def _dg(a, b, dims):
    return lax.dot_general(a, b, (dims, ((), ())), preferred_element_type=F32)


def _mm(a, b, dims=((1,), (0,))):
    return _dg(a.astype(BF16), b.astype(BF16), dims)


_NN = ((1,), (0,))
_NT = ((1,), (1,))
_TN = ((0,), (0,))


def _wkv_operands(r, k, v, kk, lw, a, ka, reverse):
    C = WKV_CHUNK
    row = lax.broadcasted_iota(jnp.int32, (C, C), 0)
    col = lax.broadcasted_iota(jnp.int32, (C, C), 1)
    incl = (col >= row) if reverse else (col <= row)

    lw_hi, lw_lo = _split(lw)
    tri = incl.astype(BF16)
    cin = _dg(tri, lw_hi, _NN) + _dg(tri, lw_lo, _NN)
    tot = jnp.sum(lw, axis=0, keepdims=True)
    half = 0.5 * tot
    e0 = jnp.exp(half)
    e1 = jnp.exp(cin - half)
    e2 = jnp.exp(half - cin)
    ew = jnp.exp(-lw)
    kdir = k * (1.0 + (a - 1.0) * ka)
    rt = r * e1
    bt = kk * (e1 * ew)
    kt = kdir * e2
    at = -(kk * a) * e2
    r0 = rt * e0
    b0 = bt * e0
    kh = kt * e0
    ah = at * e0
    return dict(bt=bt, rt=rt, kt=kt, at=at, b0=b0, r0=r0, kh=kh, ah=ah, e0sq=e0 * e0, v=v)


def _wkv_kernel(rf_ref, kf_ref, vf_ref, nf_ref, rb_ref, kb_ref, vb_ref, nb_ref, lwf_ref, lwb_ref,
                af_ref, ab_ref, ka_ref, yf_ref, yb_ref, h_ref):
    C = WKV_CHUNK

    @pl.when(pl.program_id(1) == 0)
    def _():
        h_ref[...] = jnp.zeros_like(h_ref)

    ka = ka_ref[...]
    n_pairs = ka.shape[1] // LANES
    outs = [yf_ref, yb_ref]

    row2 = lax.broadcasted_iota(jnp.int32, (C, 2 * C), 0)
    col2 = lax.broadcasted_iota(jnp.int32, (C, 2 * C), 1) % C
    earlier2 = [col2 < row2, col2 > row2]
    incl2 = [col2 <= row2, col2 >= row2]
    lane = lax.broadcasted_iota(jnp.int32, (1, LANES), 1)
    first = lane < RW_HEAD
    ri = lax.broadcasted_iota(jnp.int32, (LANES, LANES), 0)
    ci = lax.broadcasted_iota(jnp.int32, (LANES, LANES), 1)
    eye = ri == ci
    same_head = (ri // RW_HEAD) == (ci // RW_HEAD)

    def stack2(x):
        return jnp.concatenate([jnp.where(first, x, 0.0), jnp.where(first, 0.0, x)], axis=0)

    chains = [(d, p) for d in range(2) for p in range(n_pairs)]
    n = range(len(chains))
    hp = [h_ref[q] for q in n]
    n_sub = rf_ref.shape[1] // C
    for s in range(n_sub):
        fs = slice(s * C, (s + 1) * C)
        bs = slice((n_sub - 1 - s) * C, (n_sub - s) * C)
        rows = [fs, bs]
        ops = [_wkv_operands(rf_ref[0, fs, :], kf_ref[0, fs, :], vf_ref[0, fs, :], nf_ref[0, fs, :],
                             lwf_ref[0, 0, fs, :], af_ref[0, 0, fs, :], ka, False),
               _wkv_operands(rb_ref[0, bs, :], kb_ref[0, bs, :], vb_ref[0, bs, :], nb_ref[0, bs, :],
                             lwb_ref[0, 0, bs, :], ab_ref[0, 0, bs, :], ka, True)]

        def op(name, q):
            d, p = chains[q]
            return ops[d][name][:, p * LANES:(p + 1) * LANES]

        g = [_mm(jnp.concatenate([op("bt", q), op("rt", q)], axis=0),
                 jnp.concatenate([stack2(op("kt", q)), stack2(op("at", q))], axis=0), _NT) for q in n]
        bh = [_mm(jnp.concatenate([op("b0", q), op("r0", q)], axis=0), hp[q]) for q in n]
        v2 = [stack2(op("v", q)) for q in n]
        x = [bh[q][:C] + _mm(jnp.where(earlier2[chains[q][0]], g[q][:C, :2 * C], 0.0), v2[q]) for q in n]
        pw = [jnp.where(earlier2[chains[q][0]], g[q][:C, 2 * C:], 0.0) for q in n]
        n_steps = C.bit_length() - 1
        for i in range(n_steps):
            x = [x[q] + _mm(pw[q], stack2(x[q])) for q in n]
            if i + 1 < n_steps:
                pw = [_mm(pw[q], stack2(pw[q])) for q in n]
        for q in n:
            d, p = chains[q]
            a_r = jnp.concatenate([jnp.where(incl2[d], g[q][C:, :2 * C], 0.0),
                                   jnp.where(incl2[d], g[q][C:, 2 * C:], 0.0)], axis=1)
            outs[d][0, rows[d], p * LANES:(p + 1) * LANES] = bh[q][C:] + _mm(
                a_r, jnp.concatenate([v2[q], stack2(x[q])], axis=0))
        new = []
        for q in n:
            dg = jnp.where(eye, jnp.broadcast_to(op("e0sq", q), (LANES, LANES)), 0.0)
            hn = _mm(jnp.concatenate([op("kh", q), op("ah", q), dg], axis=0),
                     jnp.concatenate([op("v", q), x[q], hp[q]], axis=0), _TN)
            new.append(jnp.where(same_head, hn, 0.0))
        hp = new
    for q in n:
        h_ref[q] = hp[q]


def wkv7(r, k, v, kk, lw, a, k_a):
    B, S, D = r.shape
    C = WKV_CHUNK * WKV_CHUNKS_PER_STEP
    nc = S // C
    assert S % C == 0 and D % LANES == 0
    fwd = pl.BlockSpec((1, C, D), lambda b, c: (b, c, 0))
    bwd = pl.BlockSpec((1, C, D), lambda b, c: (b, nc - 1 - c, 0))
    fwd2 = pl.BlockSpec((1, 1, C, D), lambda b, c: (b, 0, c, 0))
    bwd2 = pl.BlockSpec((1, 1, C, D), lambda b, c: (b, 1, nc - 1 - c, 0))
    out = jax.ShapeDtypeStruct((B, S, D), F32)
    return pl.pallas_call(
        _wkv_kernel,
        grid=(B, nc),
        in_specs=[fwd] * 4 + [bwd] * 4 + [fwd2, bwd2, fwd2, bwd2, pl.BlockSpec((1, D), lambda b, c: (0, 0))],
        out_specs=[fwd, bwd],
        out_shape=[out, out],
        scratch_shapes=[pltpu.VMEM((2 * D // LANES, LANES, LANES), F32)],
        compiler_params=pltpu.CompilerParams(dimension_semantics=("parallel", "arbitrary")),
        name="wkv7_chunked",
    )(r, k, v, kk, r, k, v, kk, lw, lw, a, a, k_a.reshape(1, D))


FFT_N2 = 64
FFT_K1_BLOCK = 8


def _round_up(n, m):
    return (n + m - 1) // m * m


def _twice(m, axis):
    hi = jnp.asarray(m, F32).astype(BF16)
    return jnp.concatenate([hi, hi], axis=axis)


def _hilo(x):
    hi, lo = _split(x)
    return jnp.concatenate([hi, lo], axis=0)


@functools.lru_cache(maxsize=None)
def _fft_tables(seq_len):
    n = 2 * seq_len
    n2 = FFT_N2
    n1 = n // n2
    h1 = n1 // 2
    k1n = h1 + 1
    k1p = _round_up(k1n, FFT_K1_BLOCK)
    k1 = np.arange(k1n)[:, None]
    th1 = 2 * np.pi * k1 * np.arange(h1)[None, :] / n1
    f1 = np.zeros((2 * k1p, h1))
    f1[:k1n] = np.cos(th1)
    f1[k1p:k1p + k1n] = -np.sin(th1)
    m = np.arange(n2)
    th2 = 2 * np.pi * (m[None, None, :] * m[None, :, None] / n2 + m[None, None, :] * np.arange(k1n)[:, None, None] / n)
    gr, gi = np.cos(th2), -np.sin(th2)
    g = np.zeros((k1p, 2 * n2, 2 * n2))
    g[:k1n] = np.block([[gr, -gi], [gi, gr]])
    gt = np.transpose(g, (0, 2, 1))
    coef = np.full((k1n,), 2.0)
    coef[0] = 1.0
    coef[-1] = 1.0
    th3 = th1.T
    m3 = np.zeros((h1, 2 * k1p))
    m3[:, :k1n] = coef * np.cos(th3) / n
    m3[:, k1p:k1p + k1n] = -coef * np.sin(th3) / n
    th1f = 2 * np.pi * k1 * np.arange(n1)[None, :] / n1
    f1_full = np.zeros((2 * k1p, n1))
    f1_full[:k1n] = np.cos(th1f)
    f1_full[k1p:k1p + k1n] = -np.sin(th1f)
    return dict(n1=n1, h1=h1, k1n=k1n, k1p=k1p, f1=f1, g=g, gt=gt, m3=m3, f1_full=f1_full)


FFT_N2_BLOCK = 8


def _rows_to_lanes(ref):
    return jnp.concatenate([ref[0, :, i, :] for i in range(FFT_N2_BLOCK)], axis=1)


def _lanes_to_rows(ref, val):
    ch = ref.shape[3]
    for i in range(FFT_N2_BLOCK):
        ref[0, :, i, :] = val[:, i * ch:(i + 1) * ch]


def _fft_s1_kernel(f_ref, x_ref, o_ref):
    _lanes_to_rows(o_ref, jnp.dot(f_ref[...], _hilo(_rows_to_lanes(x_ref)), preferred_element_type=F32))


def _fft_mid_kernel(g_ref, gt_ref, kr_ref, ki_ref, a_ref, o_ref):
    n2 = FFT_N2
    for j in range(FFT_K1_BLOCK):
        xin = jnp.concatenate([a_ref[0, 0, j], a_ref[0, 1, j]], axis=0)
        z = jnp.dot(g_ref[j], _hilo(xin), preferred_element_type=F32)
        zr, zi = z[:n2], z[n2:]
        kr, ki = kr_ref[j], ki_ref[j]
        y = jnp.concatenate([zr * kr - zi * ki, zr * ki + zi * kr], axis=0)
        b = jnp.dot(gt_ref[j], _hilo(y), preferred_element_type=F32)
        o_ref[0, 0, j] = b[:n2]
        o_ref[0, 1, j] = b[n2:]


def _fft_fwd_kernel(g_ref, a_ref, s_ref, re_ref, im_ref):
    n2 = FFT_N2
    for j in range(FFT_K1_BLOCK):
        xin = jnp.concatenate([a_ref[0, 0, j], a_ref[0, 1, j]], axis=0)
        z = jnp.dot(g_ref[j], _hilo(xin), preferred_element_type=F32) * s_ref[...]
        re_ref[j] = z[:n2]
        im_ref[j] = z[n2:]


def _fft_s3_kernel(m_ref, b_ref, u_ref, x_ref, bias_ref, o_ref):
    y = jnp.dot(m_ref[...], _hilo(_rows_to_lanes(b_ref)), preferred_element_type=F32)
    _lanes_to_rows(o_ref, _rows_to_lanes(x_ref) * (y + _rows_to_lanes(u_ref) * bias_ref[...]))


def _long_conv_gated(u, gate, kf_re, kf_im, bias):
    bsz, seq_len, ch = u.shape
    t = _fft_tables(seq_len)
    n2, h1, k1p = FFT_N2, t["h1"], t["k1p"]
    lanes = n2 * ch
    nb = FFT_N2_BLOCK
    lt = nb * ch
    assert seq_len == h1 * n2 and ch % LANES == 0
    params = pltpu.CompilerParams(dimension_semantics=("parallel", "parallel"),
                                  vmem_limit_bytes=VMEM_LIMIT)
    spec4 = pl.BlockSpec((1, 2 * k1p, nb, ch), lambda b, j: (b, 0, j, 0))
    row = pl.BlockSpec((1, h1, nb, ch), lambda b, j: (b, 0, j, 0))
    f1 = _twice(t["f1"], 1)
    a = pl.pallas_call(
        _fft_s1_kernel,
        grid=(bsz, n2 // nb),
        in_specs=[pl.BlockSpec(f1.shape, lambda b, j: (0, 0)), row],
        out_specs=spec4,
        out_shape=jax.ShapeDtypeStruct((bsz, 2 * k1p, n2, ch), F32),
        compiler_params=params, name="hyena_dft1",
    )(f1, u.reshape(bsz, h1, n2, ch))
    kb = FFT_K1_BLOCK
    g = _twice(t["g"], 2)
    gt = _twice(t["gt"], 2)
    mat_spec = pl.BlockSpec((kb,) + g.shape[1:], lambda j, b: (j, 0, 0))
    kf_spec = pl.BlockSpec((kb, n2, ch), lambda j, b: (j, 0, 0))
    blk = pl.BlockSpec((1, 2, kb, n2, ch), lambda j, b: (b, 0, j, 0, 0))
    bm = pl.pallas_call(
        _fft_mid_kernel,
        grid=(k1p // kb, bsz),
        in_specs=[mat_spec, mat_spec, kf_spec, kf_spec, blk],
        out_specs=blk,
        out_shape=jax.ShapeDtypeStruct((bsz, 2, k1p, n2, ch), F32),
        compiler_params=params, name="hyena_dft2",
    )(g, gt, kf_re, kf_im, a.reshape(bsz, 2, k1p, n2, ch))
    m3 = _twice(t["m3"], 1)
    out = pl.pallas_call(
        _fft_s3_kernel,
        grid=(bsz, n2 // nb),
        in_specs=[pl.BlockSpec(m3.shape, lambda b, j: (0, 0)), spec4,
                  row, row, pl.BlockSpec((1, lt), lambda b, j: (0, j))],
        out_specs=row,
        out_shape=jax.ShapeDtypeStruct((bsz, h1, n2, ch), F32),
        compiler_params=params, name="hyena_dft3",
    )(m3, bm.reshape(bsz, 2 * k1p, n2, ch), u.reshape(bsz, h1, n2, ch), gate.reshape(bsz, h1, n2, ch),
      jnp.tile(bias, n2).reshape(1, lanes))
    return out.reshape(bsz, seq_len, ch)


def _fft_s1_pair_kernel(f_ref, xa_ref, xb_ref, o_ref):
    x = jnp.concatenate([_rows_to_lanes(xa_ref), _rows_to_lanes(xb_ref)], axis=0)
    _lanes_to_rows(o_ref, jnp.dot(f_ref[...], _hilo(x), preferred_element_type=F32))


def _filter_spectrum(pos, neg_rev, scale, seq_len):
    t = _fft_tables(seq_len)
    h1, n2, k1p = t["h1"], FFT_N2, t["k1p"]
    nb, kb = FFT_N2_BLOCK, FFT_K1_BLOCK
    ch = pos.shape[-1]
    f1 = _twice(t["f1_full"], 1)
    params = pltpu.CompilerParams(dimension_semantics=("parallel", "parallel"),
                                  vmem_limit_bytes=VMEM_LIMIT)
    half = pl.BlockSpec((1, h1, nb, ch), lambda b, j: (b, 0, j, 0))
    a = pl.pallas_call(
        _fft_s1_pair_kernel,
        grid=(1, n2 // nb),
        in_specs=[pl.BlockSpec(f1.shape, lambda b, j: (0, 0)), half, half],
        out_specs=pl.BlockSpec((1, 2 * k1p, nb, ch), lambda b, j: (b, 0, j, 0)),
        out_shape=jax.ShapeDtypeStruct((1, 2 * k1p, n2, ch), F32),
        compiler_params=params, name="filter_dft1",
    )(f1, pos.reshape(1, h1, n2, ch), neg_rev.reshape(1, h1, n2, ch))
    g = _twice(t["g"], 2)
    kf_spec = pl.BlockSpec((kb, n2, ch), lambda b, j: (j, 0, 0))
    out = jax.ShapeDtypeStruct((k1p, n2, ch), F32)
    return pl.pallas_call(
        _fft_fwd_kernel,
        grid=(1, k1p // kb),
        in_specs=[pl.BlockSpec((kb,) + g.shape[1:], lambda b, j: (j, 0, 0)),
                  pl.BlockSpec((1, 2, kb, n2, ch), lambda b, j: (b, 0, j, 0, 0)),
                  pl.BlockSpec((1, ch), lambda b, j: (0, 0))],
        out_specs=[kf_spec, kf_spec],
        out_shape=[out, out],
        compiler_params=params, name="filter_dft2",
    )(g, a.reshape(1, 2, k1p, n2, ch), scale.reshape(1, ch))


ROUTE_COLS = 16
ROUTE_TOKEN_CHUNK = 512


def _route_kernel(aff_ref, idx_ref, gate_ref, key_ref, val_ref, *, cap):
    a = aff_ref[0]
    n_exp, seq_len = a.shape
    bits = pltpu.bitcast(a, jnp.int32)

    def bisect(i, t):
        cand = t | jnp.left_shift(1, 30 - i)
        cnt = jnp.sum((bits >= cand).astype(jnp.int32), axis=1, keepdims=True)
        return jnp.where(cnt >= cap, cand, t)

    thr = lax.fori_loop(0, 31, bisect, jnp.zeros((n_exp, 1), jnp.int32))
    gt = bits > thr
    eq = bits == thr

    ri = lax.broadcasted_iota(jnp.int32, (LANES, LANES), 0)
    ci = lax.broadcasted_iota(jnp.int32, (LANES, LANES), 1)
    upper = (ri < ci).astype(BF16)

    def prefix(mask):
        m = mask.astype(F32)
        run = jnp.zeros((n_exp, 1), F32)
        parts = []
        for j in range(seq_len // LANES):
            tile = m[:, j * LANES:(j + 1) * LANES]
            parts.append(run + jnp.dot(tile.astype(BF16), upper, preferred_element_type=F32))
            run = run + jnp.sum(tile, axis=1, keepdims=True)
        return jnp.concatenate(parts, axis=1), run

    eq_before, _ = prefix(eq)
    n_gt = jnp.sum(gt.astype(F32), axis=1, keepdims=True)
    sel = jnp.logical_or(gt, jnp.logical_and(eq, eq_before < cap - n_gt))
    pos, _ = prefix(sel)
    key_ref[...] = jnp.where(sel, pos, -1.0)
    val_ref[...] = a

    tok = lax.broadcasted_iota(jnp.int32, (1, seq_len), 1)
    t_hi = (tok // 64).astype(F32)
    t_lo = (tok % 64).astype(F32)
    slot = lax.broadcasted_iota(jnp.int32, (cap, ROUTE_TOKEN_CHUNK), 0).astype(F32)
    zeros = jnp.zeros((ROUTE_COLS - 5, seq_len), F32)

    def compact(e, carry):
        key = key_ref[pl.ds(e, 1), :]
        g = val_ref[pl.ds(e, 1), :]
        g_hi = g.astype(BF16).astype(F32)
        g_mid = (g - g_hi).astype(BF16).astype(F32)
        g_lo = g - g_hi - g_mid
        rows = jnp.concatenate([t_hi, t_lo, g_hi, g_mid, g_lo, zeros], axis=0).astype(BF16)
        acc = jnp.zeros((cap, ROUTE_COLS), F32)
        for j in range(seq_len // ROUTE_TOKEN_CHUNK):
            sl = slice(j * ROUTE_TOKEN_CHUNK, (j + 1) * ROUTE_TOKEN_CHUNK)
            onehot = jnp.where(key[:, sl] == slot, 1.0, 0.0).astype(BF16)
            acc = acc + _dg(onehot, rows[:, sl], _NT)
        pad = jnp.zeros((cap, LANES - ROUTE_COLS), F32)
        cols = jnp.transpose(jnp.concatenate([acc, pad], axis=1))
        idx_ref[0, pl.ds(e, 1), :] = (cols[0:1] * 64.0 + cols[1:2]).astype(jnp.int32)
        gate_ref[0, pl.ds(e, 1), :] = cols[2:3] + cols[3:4] + cols[4:5]
        return carry

    lax.fori_loop(0, n_exp, compact, 0)


def _route(aff_t, cap):
    bsz, n_exp, seq_len = aff_t.shape
    assert seq_len % ROUTE_TOKEN_CHUNK == 0 and seq_len <= 64 * 256
    out = pl.BlockSpec((1, n_exp, cap), lambda b: (b, 0, 0))
    return pl.pallas_call(
        functools.partial(_route_kernel, cap=cap),
        grid=(bsz,),
        in_specs=[pl.BlockSpec((1, n_exp, seq_len), lambda b: (b, 0, 0))],
        out_specs=[out, out],
        out_shape=[jax.ShapeDtypeStruct((bsz, n_exp, cap), jnp.int32),
                   jax.ShapeDtypeStruct((bsz, n_exp, cap), F32)],
        scratch_shapes=[pltpu.VMEM((n_exp, seq_len), F32), pltpu.VMEM((n_exp, seq_len), F32)],
        compiler_params=pltpu.CompilerParams(dimension_semantics=("parallel",),
                                             vmem_limit_bytes=VMEM_LIMIT),
        name="ec_route",
    )(aff_t)


MOE_FF_TILE = 1024
MOE_GROUP = 16
LN_ROWS = 256


def _gather_rows(h_ref, idx_ref, xb_ref, base):
    rows = [h_ref[0, pl.ds(idx_ref[0, 0, base + j], 1), :] for j in range(MOE_GROUP)]
    xb_ref[pl.ds(base, MOE_GROUP), :] = jnp.concatenate(rows, axis=0).astype(BF16)


def _scatter_rows(o_ref, idx_ref, gate_ref, y_ref, base, enabled):
    y8 = y_ref[pl.ds(base, 8), :]
    toks = [idx_ref[0, 0, base + j] for j in range(8)]
    rows = [o_ref[0, pl.ds(toks[j], 1), :] + jnp.where(enabled, gate_ref[0, 0, base + j], 0.0) * y8[j:j + 1, :]
            for j in range(8)]
    for j in range(8):
        o_ref[0, pl.ds(toks[j], 1), :] = rows[j]


def _moe_kernel(idx_ref, gate_ref, idxn_ref, idxp_ref, gatep_ref, h_ref, wg_ref, wu_ref, wd_ref,
                lg_ref, lb_ref, o_ref, xb_ref, y_ref, *, alpha, n_f):
    e = pl.program_id(1)
    f = pl.program_id(2)
    last_e = pl.num_programs(1) - 1
    last_f = n_f - 1
    cap = xb_ref.shape[1]
    seq_len = h_ref.shape[1]
    share = cap // n_f
    slot = e % 2
    other = 1 - slot
    first_step = jnp.logical_and(e == 0, f == 0)

    @pl.when(first_step)
    def _():
        o_ref[...] = jnp.zeros_like(o_ref)
        y_ref[...] = jnp.zeros_like(y_ref)

        def gather(i, carry):
            _gather_rows(h_ref, idx_ref, xb_ref.at[0], pl.multiple_of(i * MOE_GROUP, MOE_GROUP))
            return carry

        lax.fori_loop(0, cap // MOE_GROUP, gather, 0)

    xb = xb_ref[slot]
    a = jnp.dot(xb, wg_ref[0], preferred_element_type=F32)
    u = jnp.dot(xb, wu_ref[0], preferred_element_type=F32)
    act = (a * jax.nn.sigmoid(a) * u).astype(BF16)
    part = jnp.dot(act, wd_ref[0], preferred_element_type=F32)

    for i in range(share // MOE_GROUP):
        _gather_rows(h_ref, idxn_ref, xb_ref.at[other], pl.multiple_of(f * share + i * MOE_GROUP, MOE_GROUP))
    for i in range(share // 8):
        _scatter_rows(o_ref, idxp_ref, gatep_ref, y_ref.at[other], pl.multiple_of(f * share + i * 8, 8), e > 0)

    @pl.when(f == 0)
    def _():
        y_ref[slot] = part

    @pl.when(f > 0)
    def _():
        y_ref[slot] += part

    @pl.when(jnp.logical_and(e == last_e, f == last_f))
    def _():
        def scatter(i, carry):
            _scatter_rows(o_ref, idx_ref, gate_ref, y_ref.at[slot], pl.multiple_of(i * 8, 8), True)
            return carry

        lax.fori_loop(0, cap // 8, scatter, 0)

    @pl.when(jnp.logical_and(e == last_e, f == last_f))
    def _():
        def norm(i, carry):
            rows = pl.ds(pl.multiple_of(i * LN_ROWS, LN_ROWS), LN_ROWS)
            z = alpha * h_ref[0, rows, :] + o_ref[0, rows, :]
            o_ref[0, rows, :] = _layer_norm(z, lg_ref[...], lb_ref[...])
            return carry

        lax.fori_loop(0, seq_len // LN_ROWS, norm, 0)


def _moe_block(h, idx, gate, wg, wu, wd, layer, ln_g, ln_b, alpha):
    bsz, seq_len, d = h.shape
    _, n_exp, _, ff = wg.shape
    cap = idx.shape[-1]
    ft = min(MOE_FF_TILE, ff)
    assert ff % ft == 0 and seq_len % LN_ROWS == 0
    n_f = ff // ft
    assert cap % (n_f * MOE_GROUP) == 0

    def smem(shift):
        def index(b, e, f):
            return (b * n_exp + jnp.clip(e + shift, 0, n_exp - 1), 0, 0)
        return pl.BlockSpec((1, 1, cap), index, memory_space=pltpu.SMEM)

    whole = pl.BlockSpec((1, seq_len, d), lambda b, e, f: (b, 0, 0), pipeline_mode=pl.Buffered(1))
    vec = pl.BlockSpec((1, d), lambda b, e, f: (0, 0))
    idx3 = idx.reshape(bsz * n_exp, 1, cap)
    gate3 = gate.reshape(bsz * n_exp, 1, cap)
    return pl.pallas_call(
        functools.partial(_moe_kernel, alpha=alpha, n_f=n_f),
        grid=(bsz, n_exp, n_f),
        in_specs=[smem(0), smem(0), smem(1), smem(-1), smem(-1), whole,
                  pl.BlockSpec((None, 1, d, ft), lambda b, e, f: (layer, e, 0, f)),
                  pl.BlockSpec((None, 1, d, ft), lambda b, e, f: (layer, e, 0, f)),
                  pl.BlockSpec((None, 1, ft, d), lambda b, e, f: (layer, e, f, 0)),
                  vec, vec],
        out_specs=whole,
        out_shape=jax.ShapeDtypeStruct((bsz, seq_len, d), F32),
        scratch_shapes=[pltpu.VMEM((2, cap, d), BF16), pltpu.VMEM((2, cap, d), F32)],
        compiler_params=pltpu.CompilerParams(
            dimension_semantics=("arbitrary", "arbitrary", "arbitrary"), vmem_limit_bytes=VMEM_LIMIT),
        name="ec_moe",
    )(idx3, gate3, idx3, idx3, gate3, h, wg, wu, wd, ln_g.reshape(1, d), ln_b.reshape(1, d))


ROW_TILE = 512
MIX_TILE = 512


def _proj_kernel(x_ref, w_ref, *o_refs):
    xb = x_ref[...].astype(BF16)
    off = 0
    for o_ref in o_refs:
        n = o_ref.shape[1]
        o_ref[...] = jnp.dot(xb, w_ref[:, off:off + n], preferred_element_type=F32)
        off += n


def _proj(x, w, widths):
    M, K = x.shape
    tm = ROW_TILE
    assert M % tm == 0 and sum(widths) == w.shape[1] and all(n % LANES == 0 for n in widths)
    return pl.pallas_call(
        _proj_kernel,
        grid=(M // tm,),
        in_specs=[pl.BlockSpec((tm, K), lambda i: (i, 0)),
                  pl.BlockSpec(w.shape, lambda i: (0, 0), pipeline_mode=pl.Buffered(1))],
        out_specs=[pl.BlockSpec((tm, n), lambda i: (i, 0)) for n in widths],
        out_shape=[jax.ShapeDtypeStruct((M, n), F32) for n in widths],
        compiler_params=pltpu.CompilerParams(dimension_semantics=("parallel",),
                                             vmem_limit_bytes=VMEM_LIMIT),
        name="proj",
    )(x, w.astype(BF16))


def _pad_to(w, n, axis):
    pad = [(0, 0)] * w.ndim
    pad[axis] = (0, n - w.shape[axis])
    return jnp.pad(w, pad)


def _ln_kernel(x_ref, g_ref, b_ref, o_ref):
    o_ref[...] = _layer_norm(x_ref[...], g_ref[...], b_ref[...])


def _ln_rows(x, g, b):
    M, D = x.shape
    row = pl.BlockSpec((ROW_TILE, D), lambda i: (i, 0))
    vec = pl.BlockSpec((1, D), lambda i: (0, 0))
    return pl.pallas_call(
        _ln_kernel, grid=(M // ROW_TILE,), in_specs=[row, vec, vec], out_specs=row,
        out_shape=jax.ShapeDtypeStruct((M, D), F32),
        compiler_params=pltpu.CompilerParams(dimension_semantics=("parallel",)),
        name="ln_in",
    )(x, g.reshape(1, D), b.reshape(1, D))


def _halo_specs(ts, width, seq_len):
    per = ts // 8
    last = seq_len // 8 - 1
    cur = pl.BlockSpec((1, ts, width), lambda b, i: (b, i, 0))
    prev = pl.BlockSpec((1, 8, width), lambda b, i: (b, jnp.maximum(i * per - 1, 0), 0))
    nxt = pl.BlockSpec((1, 8, width), lambda b, i: (b, jnp.minimum((i + 1) * per, last), 0))
    return [cur, prev, nxt]


def _neighbours(x, prev_blk, next_blk):
    ts = x.shape[0]
    i = pl.program_id(1)
    rid = lax.broadcasted_iota(jnp.int32, x.shape, 0)
    prev_row = jnp.where(i > 0, prev_blk[7:8, :], 0.0)
    next_row = jnp.where(i < pl.num_programs(1) - 1, next_blk[0:1, :], 0.0)
    zp = jnp.where(rid == 0, prev_row, pltpu.roll(x, 1, 0))
    zn = jnp.where(rid == ts - 1, next_row, pltpu.roll(x, ts - 1, 0))
    return zp, zn


def _shortconv_kernel(p_ref, pp_ref, pn_ref, w_ref, b_ref, v_ref, x1_ref, x2_ref):
    p = p_ref[0]
    zp, zn = _neighbours(p, pp_ref[0], pn_ref[0])
    q = w_ref[0:1, :] * zp + w_ref[1:2, :] * p + w_ref[2:3, :] * zn + b_ref[...]
    d = v_ref.shape[2]
    v_ref[0] = q[:, :d]
    x1_ref[0] = q[:, d:2 * d]
    x2_ref[0] = q[:, 2 * d:]


def _shortconv(p, conv_w, conv_b):
    bsz, seq_len, width = p.shape
    d = width // 3
    ts = ROW_TILE
    out = pl.BlockSpec((1, ts, d), lambda b, i: (b, i, 0))
    return pl.pallas_call(
        _shortconv_kernel,
        grid=(bsz, seq_len // ts),
        in_specs=_halo_specs(ts, width, seq_len) + [pl.BlockSpec((3, width), lambda b, i: (0, 0)),
                                                     pl.BlockSpec((1, width), lambda b, i: (0, 0))],
        out_specs=[out, out, out],
        out_shape=[jax.ShapeDtypeStruct((bsz, seq_len, d), F32)] * 3,
        compiler_params=pltpu.CompilerParams(dimension_semantics=("parallel", "parallel")),
        name="hyena_shortconv",
    )(p, p, p, conv_w, conv_b.reshape(1, width))


def _head_sum(x, bd):
    hi, lo = _split(x)
    w = MXU_WIDTH
    parts = [_dg(hi[:, j:j + w], bd[j:j + w, j:j + w], _NN) + _dg(lo[:, j:j + w], bd[j:j + w, j:j + w], _NN)
             for j in range(0, x.shape[1], w)]
    return jnp.concatenate(parts, axis=1)


def _rwkv_prep_kernel(*refs, has_res):
    (p_ref, pp_ref, pn_ref, q_ref, qp_ref, qn_ref) = refs[:6]
    refs = refs[6:]
    if has_res:
        vf_ref, refs = refs[0], refs[1:]
    (mu_ref, mul_ref, w2_ref, w0_ref, a2_ref, a0_ref, g2_ref, kk_ref, v0_ref, v1_ref, v2_ref, bd_ref,
     r_out, k_out, v_out, kn_out, lw_out, a_out, g_out) = refs
    d = r_out.shape[2]
    p = p_ref[0]
    zp, zn = _neighbours(p, pp_ref[0], pn_ref[0])
    p = p + mu_ref[0:1, :] * (zp - p) + mu_ref[1:2, :] * (zn - p)
    q = q_ref[0]
    zp, zn = _neighbours(q, qp_ref[0], qn_ref[0])
    q = q + mul_ref[0:1, :] * (zp - q) + mul_ref[1:2, :] * (zn - q)
    r, k, v = p[:, :d], p[:, d:2 * d], p[:, 2 * d:]
    if has_res:
        mix = _dg(_dg(v.astype(BF16), v1_ref[...], _NN).astype(BF16), v2_ref[...], _NN)
        v = v + (vf_ref[0] - v) * jax.nn.sigmoid(v0_ref[...] + mix)
    wd, ad, gd = q[:, :LANES], q[:, LANES:2 * LANES], q[:, 2 * LANES:]
    wpre = w0_ref[...] + _dg(jnp.tanh(wd).astype(BF16), w2_ref[...], _NN)
    lw = -math.exp(-0.5) * jax.nn.sigmoid(wpre)
    a = jax.nn.sigmoid(a0_ref[...] + _dg(ad.astype(BF16), a2_ref[...], _NN))
    g = _dg(jax.nn.sigmoid(gd).astype(BF16), g2_ref[...], _NN)
    kq = k * kk_ref[...]
    norm = jnp.sqrt(_head_sum(kq * kq, bd_ref[...]))
    r_out[0] = r
    k_out[0] = k
    v_out[0] = v
    kn_out[0] = kq / jnp.maximum(norm, 1e-12)
    lw_out[0, 0] = lw[:, :d]
    lw_out[0, 1] = lw[:, d:]
    a_out[0, 0] = a[:, :d]
    a_out[0, 1] = a[:, d:]
    g_out[0] = g


def _block_diag2(m):
    z = jnp.zeros_like(m[0])
    return jnp.concatenate([jnp.concatenate([m[0], z], axis=1), jnp.concatenate([z, m[1]], axis=1)], axis=0)


def _head_ones(d):
    i = np.arange(d) // RW_HEAD
    return jnp.asarray(i[:, None] == i[None, :], BF16)


def _rwkv_prep(p_rkv, p_lora, v_first, v_res, mu_rkv, mu_lora, w0, w2, a0, a2, g2, k_k):
    bsz, seq_len, w3 = p_rkv.shape
    d = w3 // 3
    wl = p_lora.shape[2]
    ts = ROW_TILE
    assert 2 * w2.shape[1] == LANES and 2 * a2.shape[1] == LANES
    has_res = v_res is not None
    const = lambda shape: pl.BlockSpec(shape, lambda b, i: (0,) * len(shape))
    tile = pl.BlockSpec((1, ts, d), lambda b, i: (b, i, 0))
    pair = pl.BlockSpec((1, 2, ts, d), lambda b, i: (b, 0, i, 0))
    if has_res:
        v0, v1, v2 = v_res
        rank = _round_up(v1.shape[1], LANES)
        v0, v1, v2 = v0.reshape(1, d), _pad_to(v1, rank, 1).astype(BF16), _pad_to(v2, rank, 0).astype(BF16)
    else:
        v0, v1, v2 = jnp.zeros((1, d), F32), jnp.zeros((d, LANES), BF16), jnp.zeros((LANES, d), BF16)
    params = [mu_rkv, mu_lora, _block_diag2(w2).astype(BF16), w0.reshape(1, 2 * d),
              _block_diag2(a2).astype(BF16), a0.reshape(1, 2 * d),
              _pad_to(g2, wl - 2 * LANES, 0).astype(BF16), k_k.reshape(1, d), v0, v1, v2, _head_ones(d)]
    args = [p_rkv, p_rkv, p_rkv, p_lora, p_lora, p_lora] + ([v_first] if has_res else []) + params
    tok = jax.ShapeDtypeStruct((bsz, seq_len, d), F32)
    two = jax.ShapeDtypeStruct((bsz, 2, seq_len, d), F32)
    return pl.pallas_call(
        functools.partial(_rwkv_prep_kernel, has_res=has_res),
        grid=(bsz, seq_len // ts),
        in_specs=(_halo_specs(ts, w3, seq_len) + _halo_specs(ts, wl, seq_len) + ([tile] if has_res else [])
                  + [const(x.shape) for x in params]),
        out_specs=[tile, tile, tile, tile, pair, pair, tile],
        out_shape=[tok, tok, tok, tok, two, two, tok],
        compiler_params=pltpu.CompilerParams(dimension_semantics=("parallel", "parallel"),
                                             vmem_limit_bytes=VMEM_LIMIT),
        name="rwkv_prep",
    )(*args)


def _mix_kernel(yf_ref, yb_ref, r_ref, k_ref, v_ref, a_ref, g_ref, yh_ref, pg_ref, h_ref,
                ka_ref, rk_ref, lg_ref, lb_ref, bd_ref, why_ref, wrw_ref, wo_ref, n1g_ref, n1b_ref,
                wr_ref, br_ref, h_out, aff_out, *, alpha, n_exp):
    d_model = h_ref.shape[2]
    bd = bd_ref[...]
    inv = 1.0 / RW_HEAD
    y = yf_ref[0] + yb_ref[0]
    mu = _head_sum(y, bd) * inv
    dy = y - mu
    var = _head_sum(dy * dy, bd) * inv
    yn = dy * lax.rsqrt(var + GN_EPS) * lg_ref[...] + lb_ref[...]
    k = k_ref[0]
    ka = ka_ref[...]
    k_sum = k * (1.0 + (a_ref[0, 0] - 1.0) * ka) + k * (1.0 + (a_ref[0, 1] - 1.0) * ka)
    bonus = _head_sum(r_ref[0] * k_sum * rk_ref[...], bd) * v_ref[0]
    y_rw = (yn + bonus) * g_ref[0]
    o_hy = _dg(yh_ref[0].astype(BF16), why_ref[...], _NN)
    o_rw = _dg(y_rw.astype(BF16), wrw_ref[...], _NN)
    gates = jax.nn.sigmoid(pg_ref[0])
    merged = gates[:, :d_model] * o_hy + gates[:, d_model:] * o_rw
    mo = _dg(merged.astype(BF16), wo_ref[...], _NN)
    h1 = _layer_norm(alpha * h_ref[0] + mo, n1g_ref[...], n1b_ref[...])
    h_out[0] = h1
    logits = _dg(h1.astype(BF16), wr_ref[...], _NN) + br_ref[...]
    col = lax.broadcasted_iota(jnp.int32, logits.shape, 1)
    logits = jnp.where(col < n_exp, logits, -1e30)
    ex = jnp.exp(logits - jnp.max(logits, axis=1, keepdims=True))
    aff = ex / jnp.sum(ex, axis=1, keepdims=True)
    aff_out[0] = jnp.transpose(aff)[:n_exp, :]


def _mix(y_f, y_b, r, k, v, a, g, y_hy, p_gate, h, k_a, r_k, lnx_g, lnx_b, w_o_hy, w_o_rw, w_out, ln_g, ln_b,
         w_router, b_router, alpha):
    bsz, seq_len, d = r.shape
    d_model = h.shape[2]
    n_exp = w_router.shape[1]
    ts = MIX_TILE
    tile = lambda w: pl.BlockSpec((1, ts, w), lambda b, i: (b, i, 0))
    const = lambda shape: pl.BlockSpec(shape, lambda b, i: (0,) * len(shape))
    vec = lambda x: x.reshape(1, -1)
    params = [vec(k_a), vec(r_k), vec(lnx_g), vec(lnx_b), _head_ones(d), w_o_hy.astype(BF16),
              w_o_rw.astype(BF16), w_out.astype(BF16), vec(ln_g), vec(ln_b),
              _pad_to(w_router, LANES, 1).astype(BF16), _pad_to(vec(b_router), LANES, 1)]
    return pl.pallas_call(
        functools.partial(_mix_kernel, alpha=alpha, n_exp=n_exp),
        grid=(bsz, seq_len // ts),
        in_specs=[tile(d), tile(d),
                  tile(d), tile(d), tile(d), pl.BlockSpec((1, 2, ts, d), lambda b, i: (b, 0, i, 0)),
                  tile(d), tile(d), tile(2 * d_model), tile(d_model)] + [const(x.shape) for x in params],
        out_specs=[tile(d_model), pl.BlockSpec((1, n_exp, ts), lambda b, i: (b, 0, i))],
        out_shape=[jax.ShapeDtypeStruct((bsz, seq_len, d_model), F32),
                   jax.ShapeDtypeStruct((bsz, n_exp, seq_len), F32)],
        compiler_params=pltpu.CompilerParams(dimension_semantics=("parallel", "parallel"),
                                             vmem_limit_bytes=VMEM_LIMIT),
        name="mix_out",
    )(y_f, y_b, r, k, v, a, g, y_hy, p_gate, h, *params)


LN_EPS = 1e-5
GN_EPS = 64e-5
HY_ORDER = 2
HY_TARGET = 1e-2
HY_SHORT_DECAY_PCT = 0.3
HY_LONG_DECAY_PCT = 1.5
N_EXPERTS = 16
EC_CAPACITY = 2


def _layer_norm(x, g, b):
    mu = jnp.mean(x, -1, keepdims=True)
    var = jnp.mean(jnp.square(x - mu), -1, keepdims=True)
    return (x - mu) * lax.rsqrt(var + LN_EPS) * g + b


def _filter_mlp_kernel(zf_ref, tf_ref, zr_ref, tr_ref, w1_ref, b1_ref, w2_ref, b2_ref, w3_ref, fr_ref, dl_ref,
                       pos_ref, neg_ref, ss_ref):
    fr = fr_ref[...]
    total = 0.0
    for d, (z_ref, t_ref, o_ref) in enumerate([(zf_ref, tf_ref, pos_ref), (zr_ref, tr_ref, neg_ref)]):
        a = jnp.sin(fr * (_dg(z_ref[...].astype(BF16), w1_ref[...], _NN) + b1_ref[...]))
        for i in range(w2_ref.shape[0]):
            a = jnp.sin(fr * (_dg(a.astype(BF16), w2_ref[i], _NN) + b2_ref[i]))
        filt = _dg(a.astype(BF16), w3_ref[d], _NN) * jnp.exp(-t_ref[...] * dl_ref[...])
        o_ref[...] = filt
        total = total + jnp.sum(filt * filt, axis=0, keepdims=True)
    ss_ref[0] = jnp.broadcast_to(total, ss_ref.shape[1:])


def _hyena_filters(seq_len, w1, b1, w2, b2, w3, freq, d_hy):
    emb, fh = w1.shape
    bands = (emb - 1) // 2
    width = w3.shape[1] // 2
    t = jnp.linspace(0.0, 1.0, seq_len, dtype=F32)[:, None]
    w = 2.0 * math.pi * jnp.arange(seq_len, dtype=F32)[:, None] / seq_len
    f = jnp.linspace(1e-4, bands - 1, bands, dtype=F32)[None, :]
    z = _pad_to(jnp.concatenate([t, jnp.cos(f * w), -jnp.sin(f * w)], axis=-1), LANES, 1)
    max_decay = math.log(HY_TARGET) / HY_SHORT_DECAY_PCT
    min_decay = math.log(HY_TARGET) / HY_LONG_DECAY_PCT
    deltas = jnp.abs(jnp.linspace(min_decay, max_decay, d_hy, dtype=F32))
    tl = ROW_TILE
    n_tiles = seq_len // tl
    full = lambda x: pl.BlockSpec(x.shape, lambda i: (0,) * x.ndim)
    w3d = jnp.transpose(w3.reshape(fh, HY_ORDER, 2, d_hy), (2, 0, 1, 3)).reshape(2, fh, width)
    params = [_pad_to(w1, LANES, 0).astype(BF16), b1.reshape(1, fh), w2.astype(BF16),
              b2.reshape(b2.shape[0], 1, fh), w3d.astype(BF16), freq.reshape(1, fh),
              jnp.tile(deltas, width // d_hy).reshape(1, width)]
    rows = lambda w: pl.BlockSpec((tl, w), lambda i: (i, 0))
    out = jax.ShapeDtypeStruct((seq_len, width), F32)
    pos, neg, ss = pl.pallas_call(
        _filter_mlp_kernel,
        grid=(n_tiles,),
        in_specs=[rows(LANES), rows(1), rows(LANES), rows(1)] + [full(x) for x in params],
        out_specs=[rows(width), rows(width), pl.BlockSpec((1, 8, width), lambda i: (i, 0, 0))],
        out_shape=[out, out, jax.ShapeDtypeStruct((n_tiles, 8, width), F32)],
        compiler_params=pltpu.CompilerParams(dimension_semantics=("parallel",)),
        name="filter_mlp",
    )(z, t, z[::-1], t[::-1], *params)
    energy = jnp.sum(ss[:, 0, :], axis=0).reshape(HY_ORDER, d_hy)
    shape = (seq_len, HY_ORDER, d_hy)
    return pos.reshape(shape), neg.reshape(shape), lax.rsqrt(energy + 1e-12)


def _hyena_branch(p, conv_w, conv_b, pos, neg, scale, bias):
    seq_len = p.shape[1]
    v, x1, x2 = _shortconv(p, conv_w, conv_b)
    z = _long_conv_gated(v, x1, *_filter_spectrum(pos[:, 0], neg[:, 0], scale[0], seq_len), bias[0])
    return _long_conv_gated(z, x2, *_filter_spectrum(pos[:, 1], neg[:, 1], scale[1], seq_len), bias[1])


def kernel(x, ln0_g, ln0_b, w_in, hy_conv_w, hy_conv_b, hy_ffn_w1, hy_ffn_b1, hy_ffn_w2, hy_ffn_b2,
           hy_ffn_w3, hy_freq, hy_bias, rw_mu, rw_w0, rw_w2, rw_a0, rw_a2, rw_v0, rw_v1, rw_v2, rw_g2,
           rw_k_k, rw_k_a, rw_r_k, rw_lnx_g, rw_lnx_b, w_o_hy, w_o_rw, w_out, ln1_g, ln1_b,
           w_router, b_router, w_gate, w_up, w_down, ln2_g, ln2_b):
    bsz, seq_len, d_model = x.shape
    depth = w_in.shape[0]
    d_hy = hy_bias.shape[-1]
    d_rw = rw_w0.shape[-1]
    p_hy = 3 * d_hy
    p_rw = rw_mu.shape[-1]
    lora = p_rw - 3 * d_rw
    lora_pad = _round_up(lora, LANES)
    widths = (p_hy, 3 * d_rw, lora_pad, 2 * d_model)
    alpha = (2 * depth) ** 0.25
    cap = EC_CAPACITY * seq_len // N_EXPERTS
    tokens = bsz * seq_len
    h = _ln_rows(x.reshape(tokens, d_model), ln0_g, ln0_b).reshape(bsz, seq_len, d_model)
    v_first = None
    wg_all, wu_all, wd_all = w_gate.astype(BF16), w_up.astype(BF16), w_down.astype(BF16)
    for l in range(depth):
        c0, c1, c2 = p_hy, p_hy + 3 * d_rw, p_hy + p_rw
        w = jnp.concatenate([w_in[l][:, :c1], _pad_to(w_in[l][:, c1:c2], lora_pad, 1), w_in[l][:, c2:]], axis=1)
        pa, pb, pl_, pg = [t.reshape(bsz, seq_len, -1) for t in _proj(h.reshape(tokens, d_model), w, widths)]
        pos, neg, scale = _hyena_filters(seq_len, hy_ffn_w1[l], hy_ffn_b1[l], hy_ffn_w2[l], hy_ffn_b2[l],
                                         hy_ffn_w3[l], hy_freq[l], d_hy)
        y_hy = _hyena_branch(pa, hy_conv_w[l], hy_conv_b[l], pos, neg, scale, hy_bias[l])
        v_res = None if l == 0 else (rw_v0[l - 1], rw_v1[l - 1], rw_v2[l - 1])
        mu = rw_mu[l]
        r, k, v, kk, lw, a, g = _rwkv_prep(pb, pl_, v_first, v_res, mu[:, :3 * d_rw],
                                           _pad_to(mu[:, 3 * d_rw:], lora_pad, 1), rw_w0[l], rw_w2[l],
                                           rw_a0[l], rw_a2[l], rw_g2[l], rw_k_k[l])
        if l == 0:
            v_first = v
        y_f, y_b = wkv7(r, k, v, kk, lw, a, rw_k_a[l])
        h, aff_t = _mix(y_f, y_b, r, k, v, a, g, y_hy, pg, h, rw_k_a[l], rw_r_k[l], rw_lnx_g[l], rw_lnx_b[l],
                        w_o_hy[l], w_o_rw[l], w_out[l], ln1_g[l], ln1_b[l], w_router[l], b_router[l], alpha)
        idx, gate = _route(aff_t, cap)
        h = _moe_block(h, idx, gate, wg_all, wu_all, wd_all, l, ln2_g[l], ln2_b[l], alpha)
    return h
```

```python
import functools
import math

import jax
import jax.numpy as jnp
import numpy as np
from jax import lax
from jax.experimental import pallas as pl
from jax.experimental.pallas import tpu as pltpu

F32 = jnp.float32
BF16 = jnp.bfloat16

RW_HEAD = 64
LANES = 128
MXU_WIDTH = 256
WKV_CHUNK = 64
WKV_CHUNKS_PER_STEP = 8
VMEM_LIMIT = 56 * 1024 * 1024


def _split(x):
    hi = x.astype(BF16)
    lo = (x - hi.astype(F32)).astype(BF16)
    return hi, lo


def _dg(a, b, dims):
    return lax.dot_general(a, b, (dims, ((), ())), preferred_element_type=F32)


def _mm(a, b, dims=((1,), (0,))):
    return _dg(a.astype(BF16), b.astype(BF16), dims)


_NN = ((1,), (0,))
_NT = ((1,), (1,))
_TN = ((0,), (0,))


def _wkv_operands(r, k, v, kk, lw, a, ka, reverse):
    C = WKV_CHUNK
    row = lax.broadcasted_iota(jnp.int32, (C, C), 0)
    col = lax.broadcasted_iota(jnp.int32, (C, C), 1)
    incl = (col >= row) if reverse else (col <= row)

    lw_hi, lw_lo = _split(lw)
    tri = incl.astype(BF16)
    cin = _dg(tri, lw_hi, _NN) + _dg(tri, lw_lo, _NN)
    tot = jnp.sum(lw, axis=0, keepdims=True)
    half = 0.5 * tot
    e0 = jnp.exp(half)
    e1 = jnp.exp(cin - half)
    e2 = jnp.exp(half - cin)
    ew = jnp.exp(-lw)
    kdir = k * (1.0 + (a - 1.0) * ka)
    rt = r * e1
    bt = kk * (e1 * ew)
    kt = kdir * e2
    at = -(kk * a) * e2
    r0 = rt * e0
    b0 = bt * e0
    kh = kt * e0
    ah = at * e0
    return dict(bt=bt, rt=rt, kt=kt, at=at, b0=b0, r0=r0, kh=kh, ah=ah, e0sq=e0 * e0, v=v)


def _wkv_kernel(rf_ref, kf_ref, vf_ref, nf_ref, rb_ref, kb_ref, vb_ref, nb_ref, lwf_ref, lwb_ref,
                af_ref, ab_ref, ka_ref, yf_ref, yb_ref, h_ref):
    C = WKV_CHUNK

    @pl.when(pl.program_id(1) == 0)
    def _():
        h_ref[...] = jnp.zeros_like(h_ref)

    ka = ka_ref[...]
    n_pairs = ka.shape[1] // LANES
    outs = [yf_ref, yb_ref]

    row2 = lax.broadcasted_iota(jnp.int32, (C, 2 * C), 0)
    col2 = lax.broadcasted_iota(jnp.int32, (C, 2 * C), 1) % C
    earlier2 = [col2 < row2, col2 > row2]
    incl2 = [col2 <= row2, col2 >= row2]
    lane = lax.broadcasted_iota(jnp.int32, (1, LANES), 1)
    first = lane < RW_HEAD
    ri = lax.broadcasted_iota(jnp.int32, (LANES, LANES), 0)
    ci = lax.broadcasted_iota(jnp.int32, (LANES, LANES), 1)
    eye = ri == ci
    same_head = (ri // RW_HEAD) == (ci // RW_HEAD)

    def stack2(x):
        return jnp.concatenate([jnp.where(first, x, 0.0), jnp.where(first, 0.0, x)], axis=0)

    chains = [(d, p) for d in range(2) for p in range(n_pairs)]
    n = range(len(chains))
    hp = [h_ref[q] for q in n]
    n_sub = rf_ref.shape[1] // C
    for s in range(n_sub):
        fs = slice(s * C, (s + 1) * C)
        bs = slice((n_sub - 1 - s) * C, (n_sub - s) * C)
        rows = [fs, bs]
        ops = [_wkv_operands(rf_ref[0, fs, :], kf_ref[0, fs, :], vf_ref[0, fs, :], nf_ref[0, fs, :],
                             lwf_ref[0, 0, fs, :], af_ref[0, 0, fs, :], ka, False),
               _wkv_operands(rb_ref[0, bs, :], kb_ref[0, bs, :], vb_ref[0, bs, :], nb_ref[0, bs, :],
                             lwb_ref[0, 0, bs, :], ab_ref[0, 0, bs, :], ka, True)]

        def op(name, q):
            d, p = chains[q]
            return ops[d][name][:, p * LANES:(p + 1) * LANES]

        g = [_mm(jnp.concatenate([op("bt", q), op("rt", q)], axis=0),
                 jnp.concatenate([stack2(op("kt", q)), stack2(op("at", q))], axis=0), _NT) for q in n]
        bh = [_mm(jnp.concatenate([op("b0", q), op("r0", q)], axis=0), hp[q]) for q in n]
        v2 = [stack2(op("v", q)) for q in n]
        x = [bh[q][:C] + _mm(jnp.where(earlier2[chains[q][0]], g[q][:C, :2 * C], 0.0), v2[q]) for q in n]
        pw = [jnp.where(earlier2[chains[q][0]], g[q][:C, 2 * C:], 0.0) for q in n]
        n_steps = C.bit_length() - 1
        for i in range(n_steps):
            x = [x[q] + _mm(pw[q], stack2(x[q])) for q in n]
            if i + 1 < n_steps:
                pw = [_mm(pw[q], stack2(pw[q])) for q in n]
        for q in n:
            d, p = chains[q]
            a_r = jnp.concatenate([jnp.where(incl2[d], g[q][C:, :2 * C], 0.0),
                                   jnp.where(incl2[d], g[q][C:, 2 * C:], 0.0)], axis=1)
            outs[d][0, rows[d], p * LANES:(p + 1) * LANES] = bh[q][C:] + _mm(
                a_r, jnp.concatenate([v2[q], stack2(x[q])], axis=0))
        new = []
        for q in n:
            dg = jnp.where(eye, jnp.broadcast_to(op("e0sq", q), (LANES, LANES)), 0.0)
            hn = _mm(jnp.concatenate([op("kh", q), op("ah", q), dg], axis=0),
                     jnp.concatenate([op("v", q), x[q], hp[q]], axis=0), _TN)
            new.append(jnp.where(same_head, hn, 0.0))
        hp = new
    for q in n:
        h_ref[q] = hp[q]


def wkv7(r, k, v, kk, lw, a, k_a):
    B, S, D = r.shape
    C = WKV_CHUNK * WKV_CHUNKS_PER_STEP
    nc = S // C
    assert S % C == 0 and D % LANES == 0
    fwd = pl.BlockSpec((1, C, D), lambda b, c: (b, c, 0))
    bwd = pl.BlockSpec((1, C, D), lambda b, c: (b, nc - 1 - c, 0))
    fwd2 = pl.BlockSpec((1, 1, C, D), lambda b, c: (b, 0, c, 0))
    bwd2 = pl.BlockSpec((1, 1, C, D), lambda b, c: (b, 1, nc - 1 - c, 0))
    out = jax.ShapeDtypeStruct((B, S, D), F32)
    return pl.pallas_call(
        _wkv_kernel,
        grid=(B, nc),
        in_specs=[fwd] * 4 + [bwd] * 4 + [fwd2, bwd2, fwd2, bwd2, pl.BlockSpec((1, D), lambda b, c: (0, 0))],
        out_specs=[fwd, bwd],
        out_shape=[out, out],
        scratch_shapes=[pltpu.VMEM((2 * D // LANES, LANES, LANES), F32)],
        compiler_params=pltpu.CompilerParams(dimension_semantics=("parallel", "arbitrary")),
        name="wkv7_chunked",
    )(r, k, v, kk, r, k, v, kk, lw, lw, a, a, k_a.reshape(1, D))


FFT_N2 = 64
FFT_K1_BLOCK = 8


def _round_up(n, m):
    return (n + m - 1) // m * m


def _twice(m, axis):
    hi = jnp.asarray(m, F32).astype(BF16)
    return jnp.concatenate([hi, hi], axis=axis)


def _hilo(x):
    hi, lo = _split(x)
    return jnp.concatenate([hi, lo], axis=0)


@functools.lru_cache(maxsize=None)
def _fft_tables(seq_len):
    n = 2 * seq_len
    n2 = FFT_N2
    n1 = n // n2
    h1 = n1 // 2
    k1n = h1 + 1
    k1p = _round_up(k1n, FFT_K1_BLOCK)
    k1 = np.arange(k1n)[:, None]
    th1 = 2 * np.pi * k1 * np.arange(h1)[None, :] / n1
    f1 = np.zeros((2 * k1p, h1))
    f1[:k1n] = np.cos(th1)
    f1[k1p:k1p + k1n] = -np.sin(th1)
    m = np.arange(n2)
    th2 = 2 * np.pi * (m[None, None, :] * m[None, :, None] / n2 + m[None, None, :] * np.arange(k1n)[:, None, None] / n)
    gr, gi = np.cos(th2), -np.sin(th2)
    g = np.zeros((k1p, 2 * n2, 2 * n2))
    g[:k1n] = np.block([[gr, -gi], [gi, gr]])
    gt = np.transpose(g, (0, 2, 1))
    coef = np.full((k1n,), 2.0)
    coef[0] = 1.0
    coef[-1] = 1.0
    th3 = th1.T
    m3 = np.zeros((h1, 2 * k1p))
    m3[:, :k1n] = coef * np.cos(th3) / n
    m3[:, k1p:k1p + k1n] = -coef * np.sin(th3) / n
    th1f = 2 * np.pi * k1 * np.arange(n1)[None, :] / n1
    f1_full = np.zeros((2 * k1p, n1))
    f1_full[:k1n] = np.cos(th1f)
    f1_full[k1p:k1p + k1n] = -np.sin(th1f)
    return dict(n1=n1, h1=h1, k1n=k1n, k1p=k1p, f1=f1, g=g, gt=gt, m3=m3, f1_full=f1_full)


FFT_N2_BLOCK = 16


def _rows_to_lanes(ref):
    return jnp.concatenate([ref[0, :, i, :] for i in range(FFT_N2_BLOCK)], axis=1)


def _lanes_to_rows(ref, val):
    ch = ref.shape[3]
    for i in range(FFT_N2_BLOCK):
        ref[0, :, i, :] = val[:, i * ch:(i + 1) * ch]


def _fft_s1_kernel(f_ref, x_ref, o_ref):
    _lanes_to_rows(o_ref, jnp.dot(f_ref[...], _hilo(_rows_to_lanes(x_ref)), preferred_element_type=F32))


def _fft_mid_kernel(g_ref, gt_ref, kr_ref, ki_ref, a_ref, o_ref):
    n2 = FFT_N2
    for j in range(FFT_K1_BLOCK):
        xin = jnp.concatenate([a_ref[0, 0, j], a_ref[0, 1, j]], axis=0)
        z = jnp.dot(g_ref[j], _hilo(xin), preferred_element_type=F32)
        zr, zi = z[:n2], z[n2:]
        kr, ki = kr_ref[j], ki_ref[j]
        y = jnp.concatenate([zr * kr - zi * ki, zr * ki + zi * kr], axis=0)
        b = jnp.dot(gt_ref[j], _hilo(y), preferred_element_type=F32)
        o_ref[0, 0, j] = b[:n2]
        o_ref[0, 1, j] = b[n2:]


def _fft_fwd_kernel(g_ref, a_ref, s_ref, re_ref, im_ref):
    n2 = FFT_N2
    for j in range(FFT_K1_BLOCK):
        xin = jnp.concatenate([a_ref[0, 0, j], a_ref[0, 1, j]], axis=0)
        z = jnp.dot(g_ref[j], _hilo(xin), preferred_element_type=F32) * s_ref[...]
        re_ref[j] = z[:n2]
        im_ref[j] = z[n2:]


def _fft_s3_kernel(m_ref, b_ref, u_ref, x_ref, bias_ref, o_ref):
    y = jnp.dot(m_ref[...], _hilo(_rows_to_lanes(b_ref)), preferred_element_type=F32)
    _lanes_to_rows(o_ref, _rows_to_lanes(x_ref) * (y + _rows_to_lanes(u_ref) * bias_ref[...]))


def _long_conv_gated(u, gate, kf_re, kf_im, bias):
    bsz, seq_len, ch = u.shape
    t = _fft_tables(seq_len)
    n2, h1, k1p = FFT_N2, t["h1"], t["k1p"]
    lanes = n2 * ch
    nb = FFT_N2_BLOCK
    lt = nb * ch
    assert seq_len == h1 * n2 and ch % LANES == 0
    params = pltpu.CompilerParams(dimension_semantics=("parallel", "parallel"),
                                  vmem_limit_bytes=VMEM_LIMIT)
    spec4 = pl.BlockSpec((1, 2 * k1p, nb, ch), lambda b, j: (b, 0, j, 0))
    row = pl.BlockSpec((1, h1, nb, ch), lambda b, j: (b, 0, j, 0))
    f1 = _twice(t["f1"], 1)
    a = pl.pallas_call(
        _fft_s1_kernel,
        grid=(bsz, n2 // nb),
        in_specs=[pl.BlockSpec(f1.shape, lambda b, j: (0, 0)), row],
        out_specs=spec4,
        out_shape=jax.ShapeDtypeStruct((bsz, 2 * k1p, n2, ch), F32),
        compiler_params=params, name="hyena_dft1",
    )(f1, u.reshape(bsz, h1, n2, ch))
    kb = FFT_K1_BLOCK
    g = _twice(t["g"], 2)
    gt = _twice(t["gt"], 2)
    mat_spec = pl.BlockSpec((kb,) + g.shape[1:], lambda j, b: (j, 0, 0))
    kf_spec = pl.BlockSpec((kb, n2, ch), lambda j, b: (j, 0, 0))
    blk = pl.BlockSpec((1, 2, kb, n2, ch), lambda j, b: (b, 0, j, 0, 0))
    bm = pl.pallas_call(
        _fft_mid_kernel,
        grid=(k1p // kb, bsz),
        in_specs=[mat_spec, mat_spec, kf_spec, kf_spec, blk],
        out_specs=blk,
        out_shape=jax.ShapeDtypeStruct((bsz, 2, k1p, n2, ch), F32),
        compiler_params=params, name="hyena_dft2",
    )(g, gt, kf_re, kf_im, a.reshape(bsz, 2, k1p, n2, ch))
    m3 = _twice(t["m3"], 1)
    out = pl.pallas_call(
        _fft_s3_kernel,
        grid=(bsz, n2 // nb),
        in_specs=[pl.BlockSpec(m3.shape, lambda b, j: (0, 0)), spec4,
                  row, row, pl.BlockSpec((1, lt), lambda b, j: (0, j))],
        out_specs=row,
        out_shape=jax.ShapeDtypeStruct((bsz, h1, n2, ch), F32),
        compiler_params=params, name="hyena_dft3",
    )(m3, bm.reshape(bsz, 2 * k1p, n2, ch), u.reshape(bsz, h1, n2, ch), gate.reshape(bsz, h1, n2, ch),
      jnp.tile(bias, n2).reshape(1, lanes))
    return out.reshape(bsz, seq_len, ch)


def _fft_s1_pair_kernel(f_ref, xa_ref, xb_ref, o_ref):
    x = jnp.concatenate([_rows_to_lanes(xa_ref), _rows_to_lanes(xb_ref)], axis=0)
    _lanes_to_rows(o_ref, jnp.dot(f_ref[...], _hilo(x), preferred_element_type=F32))


def _filter_spectrum(pos, neg_rev, scale, seq_len):
    t = _fft_tables(seq_len)
    h1, n2, k1p = t["h1"], FFT_N2, t["k1p"]
    nb, kb = FFT_N2_BLOCK, FFT_K1_BLOCK
    ch = pos.shape[-1]
    f1 = _twice(t["f1_full"], 1)
    params = pltpu.CompilerParams(dimension_semantics=("parallel", "parallel"),
                                  vmem_limit_bytes=VMEM_LIMIT)
    half = pl.BlockSpec((1, h1, nb, ch), lambda b, j: (b, 0, j, 0))
    a = pl.pallas_call(
        _fft_s1_pair_kernel,
        grid=(1, n2 // nb),
        in_specs=[pl.BlockSpec(f1.shape, lambda b, j: (0, 0)), half, half],
        out_specs=pl.BlockSpec((1, 2 * k1p, nb, ch), lambda b, j: (b, 0, j, 0)),
        out_shape=jax.ShapeDtypeStruct((1, 2 * k1p, n2, ch), F32),
        compiler_params=params, name="filter_dft1",
    )(f1, pos.reshape(1, h1, n2, ch), neg_rev.reshape(1, h1, n2, ch))
    g = _twice(t["g"], 2)
    kf_spec = pl.BlockSpec((kb, n2, ch), lambda b, j: (j, 0, 0))
    out = jax.ShapeDtypeStruct((k1p, n2, ch), F32)
    return pl.pallas_call(
        _fft_fwd_kernel,
        grid=(1, k1p // kb),
        in_specs=[pl.BlockSpec((kb,) + g.shape[1:], lambda b, j: (j, 0, 0)),
                  pl.BlockSpec((1, 2, kb, n2, ch), lambda b, j: (b, 0, j, 0, 0)),
                  pl.BlockSpec((1, ch), lambda b, j: (0, 0))],
        out_specs=[kf_spec, kf_spec],
        out_shape=[out, out],
        compiler_params=params, name="filter_dft2",
    )(g, a.reshape(1, 2, k1p, n2, ch), scale.reshape(1, ch))


ROUTE_COLS = 16
ROUTE_TOKEN_CHUNK = 512


def _route_kernel(aff_ref, idx_ref, gate_ref, key_ref, val_ref, *, cap):
    a = aff_ref[0]
    n_exp, seq_len = a.shape
    bits = pltpu.bitcast(a, jnp.int32)

    def bisect(i, t):
        cand = t | jnp.left_shift(1, 30 - i)
        cnt = jnp.sum((bits >= cand).astype(jnp.int32), axis=1, keepdims=True)
        return jnp.where(cnt >= cap, cand, t)

    thr = lax.fori_loop(0, 31, bisect, jnp.zeros((n_exp, 1), jnp.int32))
    gt = bits > thr
    eq = bits == thr

    ri = lax.broadcasted_iota(jnp.int32, (LANES, LANES), 0)
    ci = lax.broadcasted_iota(jnp.int32, (LANES, LANES), 1)
    upper = (ri < ci).astype(BF16)

    def prefix(mask):
        m = mask.astype(F32)
        run = jnp.zeros((n_exp, 1), F32)
        parts = []
        for j in range(seq_len // LANES):
            tile = m[:, j * LANES:(j + 1) * LANES]
            parts.append(run + jnp.dot(tile.astype(BF16), upper, preferred_element_type=F32))
            run = run + jnp.sum(tile, axis=1, keepdims=True)
        return jnp.concatenate(parts, axis=1), run

    eq_before, _ = prefix(eq)
    n_gt = jnp.sum(gt.astype(F32), axis=1, keepdims=True)
    sel = jnp.logical_or(gt, jnp.logical_and(eq, eq_before < cap - n_gt))
    pos, _ = prefix(sel)
    key_ref[...] = jnp.where(sel, pos, -1.0)
    val_ref[...] = a

    tok = lax.broadcasted_iota(jnp.int32, (1, seq_len), 1)
    t_hi = (tok // 64).astype(F32)
    t_lo = (tok % 64).astype(F32)
    slot = lax.broadcasted_iota(jnp.int32, (cap, ROUTE_TOKEN_CHUNK), 0).astype(F32)
    zeros = jnp.zeros((ROUTE_COLS - 5, seq_len), F32)

    def compact(e, carry):
        key = key_ref[pl.ds(e, 1), :]
        g = val_ref[pl.ds(e, 1), :]
        g_hi = g.astype(BF16).astype(F32)
        g_mid = (g - g_hi).astype(BF16).astype(F32)
        g_lo = g - g_hi - g_mid
        rows = jnp.concatenate([t_hi, t_lo, g_hi, g_mid, g_lo, zeros], axis=0).astype(BF16)
        acc = jnp.zeros((cap, ROUTE_COLS), F32)
        for j in range(seq_len // ROUTE_TOKEN_CHUNK):
            sl = slice(j * ROUTE_TOKEN_CHUNK, (j + 1) * ROUTE_TOKEN_CHUNK)
            onehot = jnp.where(key[:, sl] == slot, 1.0, 0.0).astype(BF16)
            acc = acc + _dg(onehot, rows[:, sl], _NT)
        pad = jnp.zeros((cap, LANES - ROUTE_COLS), F32)
        cols = jnp.transpose(jnp.concatenate([acc, pad], axis=1))
        idx_ref[0, pl.ds(e, 1), :] = (cols[0:1] * 64.0 + cols[1:2]).astype(jnp.int32)
        gate_ref[0, pl.ds(e, 1), :] = cols[2:3] + cols[3:4] + cols[4:5]
        return carry

    lax.fori_loop(0, n_exp, compact, 0)


def _route(aff_t, cap):
    bsz, n_exp, seq_len = aff_t.shape
    assert seq_len % ROUTE_TOKEN_CHUNK == 0 and seq_len <= 64 * 256
    out = pl.BlockSpec((1, n_exp, cap), lambda b: (b, 0, 0))
    return pl.pallas_call(
        functools.partial(_route_kernel, cap=cap),
        grid=(bsz,),
        in_specs=[pl.BlockSpec((1, n_exp, seq_len), lambda b: (b, 0, 0))],
        out_specs=[out, out],
        out_shape=[jax.ShapeDtypeStruct((bsz, n_exp, cap), jnp.int32),
                   jax.ShapeDtypeStruct((bsz, n_exp, cap), F32)],
        scratch_shapes=[pltpu.VMEM((n_exp, seq_len), F32), pltpu.VMEM((n_exp, seq_len), F32)],
        compiler_params=pltpu.CompilerParams(dimension_semantics=("parallel",),
                                             vmem_limit_bytes=VMEM_LIMIT),
        name="ec_route",
    )(aff_t)


MOE_FF_TILE = 1024
MOE_GROUP = 16
LN_ROWS = 256


def _gather_rows(h_ref, idx_ref, xb_ref, base):
    rows = [h_ref[0, pl.ds(idx_ref[0, 0, base + j], 1), :] for j in range(MOE_GROUP)]
    xb_ref[pl.ds(base, MOE_GROUP), :] = jnp.concatenate(rows, axis=0).astype(BF16)


def _scatter_rows(o_ref, idx_ref, gate_ref, y_ref, base, enabled):
    y8 = y_ref[pl.ds(base, 8), :]
    toks = [idx_ref[0, 0, base + j] for j in range(8)]
    rows = [o_ref[0, pl.ds(toks[j], 1), :] + jnp.where(enabled, gate_ref[0, 0, base + j], 0.0) * y8[j:j + 1, :]
            for j in range(8)]
    for j in range(8):
        o_ref[0, pl.ds(toks[j], 1), :] = rows[j]


def _moe_kernel(idx_ref, gate_ref, idxn_ref, idxp_ref, gatep_ref, h_ref, wg_ref, wu_ref, wd_ref,
                lg_ref, lb_ref, o_ref, xb_ref, y_ref, *, alpha, n_f):
    e = pl.program_id(1)
    f = pl.program_id(2)
    last_e = pl.num_programs(1) - 1
    last_f = n_f - 1
    cap = xb_ref.shape[1]
    seq_len = h_ref.shape[1]
    share = cap // n_f
    slot = e % 2
    other = 1 - slot
    first_step = jnp.logical_and(e == 0, f == 0)

    @pl.when(first_step)
    def _():
        o_ref[...] = jnp.zeros_like(o_ref)
        y_ref[...] = jnp.zeros_like(y_ref)

        def gather(i, carry):
            _gather_rows(h_ref, idx_ref, xb_ref.at[0], pl.multiple_of(i * MOE_GROUP, MOE_GROUP))
            return carry

        lax.fori_loop(0, cap // MOE_GROUP, gather, 0)

    xb = xb_ref[slot]
    a = jnp.dot(xb, wg_ref[0], preferred_element_type=F32)
    u = jnp.dot(xb, wu_ref[0], preferred_element_type=F32)
    act = (a * jax.nn.sigmoid(a) * u).astype(BF16)
    part = jnp.dot(act, wd_ref[0], preferred_element_type=F32)

    for i in range(share // MOE_GROUP):
        _gather_rows(h_ref, idxn_ref, xb_ref.at[other], pl.multiple_of(f * share + i * MOE_GROUP, MOE_GROUP))
    for i in range(share // 8):
        _scatter_rows(o_ref, idxp_ref, gatep_ref, y_ref.at[other], pl.multiple_of(f * share + i * 8, 8), e > 0)

    @pl.when(f == 0)
    def _():
        y_ref[slot] = part

    @pl.when(f > 0)
    def _():
        y_ref[slot] += part

    @pl.when(jnp.logical_and(e == last_e, f == last_f))
    def _():
        def scatter(i, carry):
            _scatter_rows(o_ref, idx_ref, gate_ref, y_ref.at[slot], pl.multiple_of(i * 8, 8), True)
            return carry

        lax.fori_loop(0, cap // 8, scatter, 0)

    @pl.when(jnp.logical_and(e == last_e, f == last_f))
    def _():
        def norm(i, carry):
            rows = pl.ds(pl.multiple_of(i * LN_ROWS, LN_ROWS), LN_ROWS)
            z = alpha * h_ref[0, rows, :] + o_ref[0, rows, :]
            o_ref[0, rows, :] = _layer_norm(z, lg_ref[...], lb_ref[...])
            return carry

        lax.fori_loop(0, seq_len // LN_ROWS, norm, 0)


def _moe_block(h, idx, gate, wg, wu, wd, layer, ln_g, ln_b, alpha):
    bsz, seq_len, d = h.shape
    _, n_exp, _, ff = wg.shape
    cap = idx.shape[-1]
    ft = min(MOE_FF_TILE, ff)
    assert ff % ft == 0 and seq_len % LN_ROWS == 0
    n_f = ff // ft
    assert cap % (n_f * MOE_GROUP) == 0

    def smem(shift):
        def index(b, e, f):
            return (b * n_exp + jnp.clip(e + shift, 0, n_exp - 1), 0, 0)
        return pl.BlockSpec((1, 1, cap), index, memory_space=pltpu.SMEM)

    whole = pl.BlockSpec((1, seq_len, d), lambda b, e, f: (b, 0, 0), pipeline_mode=pl.Buffered(1))
    vec = pl.BlockSpec((1, d), lambda b, e, f: (0, 0))
    idx3 = idx.reshape(bsz * n_exp, 1, cap)
    gate3 = gate.reshape(bsz * n_exp, 1, cap)
    return pl.pallas_call(
        functools.partial(_moe_kernel, alpha=alpha, n_f=n_f),
        grid=(bsz, n_exp, n_f),
        in_specs=[smem(0), smem(0), smem(1), smem(-1), smem(-1), whole,
                  pl.BlockSpec((None, 1, d, ft), lambda b, e, f: (layer, e, 0, f)),
                  pl.BlockSpec((None, 1, d, ft), lambda b, e, f: (layer, e, 0, f)),
                  pl.BlockSpec((None, 1, ft, d), lambda b, e, f: (layer, e, f, 0)),
                  vec, vec],
        out_specs=whole,
        out_shape=jax.ShapeDtypeStruct((bsz, seq_len, d), F32),
        scratch_shapes=[pltpu.VMEM((2, cap, d), BF16), pltpu.VMEM((2, cap, d), F32)],
        compiler_params=pltpu.CompilerParams(
            dimension_semantics=("arbitrary", "arbitrary", "arbitrary"), vmem_limit_bytes=VMEM_LIMIT),
        name="ec_moe",
    )(idx3, gate3, idx3, idx3, gate3, h, wg, wu, wd, ln_g.reshape(1, d), ln_b.reshape(1, d))


ROW_TILE = 512
MIX_TILE = 512


def _proj_kernel(x_ref, w_ref, *o_refs):
    xb = x_ref[...].astype(BF16)
    off = 0
    for o_ref in o_refs:
        n = o_ref.shape[1]
        o_ref[...] = jnp.dot(xb, w_ref[:, off:off + n], preferred_element_type=F32)
        off += n


def _proj(x, w, widths):
    M, K = x.shape
    tm = ROW_TILE
    assert M % tm == 0 and sum(widths) == w.shape[1] and all(n % LANES == 0 for n in widths)
    return pl.pallas_call(
        _proj_kernel,
        grid=(M // tm,),
        in_specs=[pl.BlockSpec((tm, K), lambda i: (i, 0)),
                  pl.BlockSpec(w.shape, lambda i: (0, 0), pipeline_mode=pl.Buffered(1))],
        out_specs=[pl.BlockSpec((tm, n), lambda i: (i, 0)) for n in widths],
        out_shape=[jax.ShapeDtypeStruct((M, n), F32) for n in widths],
        compiler_params=pltpu.CompilerParams(dimension_semantics=("parallel",),
                                             vmem_limit_bytes=VMEM_LIMIT),
        name="proj",
    )(x, w.astype(BF16))


def _pad_to(w, n, axis):
    pad = [(0, 0)] * w.ndim
    pad[axis] = (0, n - w.shape[axis])
    return jnp.pad(w, pad)


def _ln_kernel(x_ref, g_ref, b_ref, o_ref):
    o_ref[...] = _layer_norm(x_ref[...], g_ref[...], b_ref[...])


def _ln_rows(x, g, b):
    M, D = x.shape
    row = pl.BlockSpec((ROW_TILE, D), lambda i: (i, 0))
    vec = pl.BlockSpec((1, D), lambda i: (0, 0))
    return pl.pallas_call(
        _ln_kernel, grid=(M // ROW_TILE,), in_specs=[row, vec, vec], out_specs=row,
        out_shape=jax.ShapeDtypeStruct((M, D), F32),
        compiler_params=pltpu.CompilerParams(dimension_semantics=("parallel",)),
        name="ln_in",
    )(x, g.reshape(1, D), b.reshape(1, D))


def _halo_specs(ts, width, seq_len):
    per = ts // 8
    last = seq_len // 8 - 1
    cur = pl.BlockSpec((1, ts, width), lambda b, i: (b, i, 0))
    prev = pl.BlockSpec((1, 8, width), lambda b, i: (b, jnp.maximum(i * per - 1, 0), 0))
    nxt = pl.BlockSpec((1, 8, width), lambda b, i: (b, jnp.minimum((i + 1) * per, last), 0))
    return [cur, prev, nxt]


def _neighbours(x, prev_blk, next_blk):
    ts = x.shape[0]
    i = pl.program_id(1)
    rid = lax.broadcasted_iota(jnp.int32, x.shape, 0)
    prev_row = jnp.where(i > 0, prev_blk[7:8, :], 0.0)
    next_row = jnp.where(i < pl.num_programs(1) - 1, next_blk[0:1, :], 0.0)
    zp = jnp.where(rid == 0, prev_row, pltpu.roll(x, 1, 0))
    zn = jnp.where(rid == ts - 1, next_row, pltpu.roll(x, ts - 1, 0))
    return zp, zn


def _shortconv_kernel(p_ref, pp_ref, pn_ref, w_ref, b_ref, v_ref, x1_ref, x2_ref):
    p = p_ref[0]
    zp, zn = _neighbours(p, pp_ref[0], pn_ref[0])
    q = w_ref[0:1, :] * zp + w_ref[1:2, :] * p + w_ref[2:3, :] * zn + b_ref[...]
    d = v_ref.shape[2]
    v_ref[0] = q[:, :d]
    x1_ref[0] = q[:, d:2 * d]
    x2_ref[0] = q[:, 2 * d:]


def _shortconv(p, conv_w, conv_b):
    bsz, seq_len, width = p.shape
    d = width // 3
    ts = ROW_TILE
    out = pl.BlockSpec((1, ts, d), lambda b, i: (b, i, 0))
    return pl.pallas_call(
        _shortconv_kernel,
        grid=(bsz, seq_len // ts),
        in_specs=_halo_specs(ts, width, seq_len) + [pl.BlockSpec((3, width), lambda b, i: (0, 0)),
                                                     pl.BlockSpec((1, width), lambda b, i: (0, 0))],
        out_specs=[out, out, out],
        out_shape=[jax.ShapeDtypeStruct((bsz, seq_len, d), F32)] * 3,
        compiler_params=pltpu.CompilerParams(dimension_semantics=("parallel", "parallel")),
        name="hyena_shortconv",
    )(p, p, p, conv_w, conv_b.reshape(1, width))


def _head_sum(x, bd):
    hi, lo = _split(x)
    w = MXU_WIDTH
    parts = [_dg(hi[:, j:j + w], bd[j:j + w, j:j + w], _NN) + _dg(lo[:, j:j + w], bd[j:j + w, j:j + w], _NN)
             for j in range(0, x.shape[1], w)]
    return jnp.concatenate(parts, axis=1)


def _rwkv_prep_kernel(*refs, has_res):
    (p_ref, pp_ref, pn_ref, q_ref, qp_ref, qn_ref) = refs[:6]
    refs = refs[6:]
    if has_res:
        vf_ref, refs = refs[0], refs[1:]
    (mu_ref, mul_ref, w2_ref, w0_ref, a2_ref, a0_ref, g2_ref, kk_ref, v0_ref, v1_ref, v2_ref, bd_ref,
     r_out, k_out, v_out, kn_out, lw_out, a_out, g_out) = refs
    d = r_out.shape[2]
    p = p_ref[0]
    zp, zn = _neighbours(p, pp_ref[0], pn_ref[0])
    p = p + mu_ref[0:1, :] * (zp - p) + mu_ref[1:2, :] * (zn - p)
    q = q_ref[0]
    zp, zn = _neighbours(q, qp_ref[0], qn_ref[0])
    q = q + mul_ref[0:1, :] * (zp - q) + mul_ref[1:2, :] * (zn - q)
    r, k, v = p[:, :d], p[:, d:2 * d], p[:, 2 * d:]
    if has_res:
        mix = _dg(_dg(v.astype(BF16), v1_ref[...], _NN).astype(BF16), v2_ref[...], _NN)
        v = v + (vf_ref[0] - v) * jax.nn.sigmoid(v0_ref[...] + mix)
    wd, ad, gd = q[:, :LANES], q[:, LANES:2 * LANES], q[:, 2 * LANES:]
    wpre = w0_ref[...] + _dg(jnp.tanh(wd).astype(BF16), w2_ref[...], _NN)
    lw = -math.exp(-0.5) * jax.nn.sigmoid(wpre)
    a = jax.nn.sigmoid(a0_ref[...] + _dg(ad.astype(BF16), a2_ref[...], _NN))
    g = _dg(jax.nn.sigmoid(gd).astype(BF16), g2_ref[...], _NN)
    kq = k * kk_ref[...]
    norm = jnp.sqrt(_head_sum(kq * kq, bd_ref[...]))
    r_out[0] = r
    k_out[0] = k
    v_out[0] = v
    kn_out[0] = kq / jnp.maximum(norm, 1e-12)
    lw_out[0, 0] = lw[:, :d]
    lw_out[0, 1] = lw[:, d:]
    a_out[0, 0] = a[:, :d]
    a_out[0, 1] = a[:, d:]
    g_out[0] = g


def _block_diag2(m):
    z = jnp.zeros_like(m[0])
    return jnp.concatenate([jnp.concatenate([m[0], z], axis=1), jnp.concatenate([z, m[1]], axis=1)], axis=0)


def _head_ones(d):
    i = np.arange(d) // RW_HEAD
    return jnp.asarray(i[:, None] == i[None, :], BF16)


def _rwkv_prep(p_rkv, p_lora, v_first, v_res, mu_rkv, mu_lora, w0, w2, a0, a2, g2, k_k):
    bsz, seq_len, w3 = p_rkv.shape
    d = w3 // 3
    wl = p_lora.shape[2]
    ts = ROW_TILE
    assert 2 * w2.shape[1] == LANES and 2 * a2.shape[1] == LANES
    has_res = v_res is not None
    const = lambda shape: pl.BlockSpec(shape, lambda b, i: (0,) * len(shape))
    tile = pl.BlockSpec((1, ts, d), lambda b, i: (b, i, 0))
    pair = pl.BlockSpec((1, 2, ts, d), lambda b, i: (b, 0, i, 0))
    if has_res:
        v0, v1, v2 = v_res
        rank = _round_up(v1.shape[1], LANES)
        v0, v1, v2 = v0.reshape(1, d), _pad_to(v1, rank, 1).astype(BF16), _pad_to(v2, rank, 0).astype(BF16)
    else:
        v0, v1, v2 = jnp.zeros((1, d), F32), jnp.zeros((d, LANES), BF16), jnp.zeros((LANES, d), BF16)
    params = [mu_rkv, mu_lora, _block_diag2(w2).astype(BF16), w0.reshape(1, 2 * d),
              _block_diag2(a2).astype(BF16), a0.reshape(1, 2 * d),
              _pad_to(g2, wl - 2 * LANES, 0).astype(BF16), k_k.reshape(1, d), v0, v1, v2, _head_ones(d)]
    args = [p_rkv, p_rkv, p_rkv, p_lora, p_lora, p_lora] + ([v_first] if has_res else []) + params
    tok = jax.ShapeDtypeStruct((bsz, seq_len, d), F32)
    two = jax.ShapeDtypeStruct((bsz, 2, seq_len, d), F32)
    return pl.pallas_call(
        functools.partial(_rwkv_prep_kernel, has_res=has_res),
        grid=(bsz, seq_len // ts),
        in_specs=(_halo_specs(ts, w3, seq_len) + _halo_specs(ts, wl, seq_len) + ([tile] if has_res else [])
                  + [const(x.shape) for x in params]),
        out_specs=[tile, tile, tile, tile, pair, pair, tile],
        out_shape=[tok, tok, tok, tok, two, two, tok],
        compiler_params=pltpu.CompilerParams(dimension_semantics=("parallel", "parallel"),
                                             vmem_limit_bytes=VMEM_LIMIT),
        name="rwkv_prep",
    )(*args)


def _mix_kernel(yf_ref, yb_ref, r_ref, k_ref, v_ref, a_ref, g_ref, yh_ref, pg_ref, h_ref,
                ka_ref, rk_ref, lg_ref, lb_ref, bd_ref, why_ref, wrw_ref, wo_ref, n1g_ref, n1b_ref,
                wr_ref, br_ref, h_out, aff_out, *, alpha, n_exp):
    d_model = h_ref.shape[2]
    bd = bd_ref[...]
    inv = 1.0 / RW_HEAD
    y = yf_ref[0] + yb_ref[0]
    mu = _head_sum(y, bd) * inv
    dy = y - mu
    var = _head_sum(dy * dy, bd) * inv
    yn = dy * lax.rsqrt(var + GN_EPS) * lg_ref[...] + lb_ref[...]
    k = k_ref[0]
    ka = ka_ref[...]
    k_sum = k * (1.0 + (a_ref[0, 0] - 1.0) * ka) + k * (1.0 + (a_ref[0, 1] - 1.0) * ka)
    bonus = _head_sum(r_ref[0] * k_sum * rk_ref[...], bd) * v_ref[0]
    y_rw = (yn + bonus) * g_ref[0]
    o_hy = _dg(yh_ref[0].astype(BF16), why_ref[...], _NN)
    o_rw = _dg(y_rw.astype(BF16), wrw_ref[...], _NN)
    gates = jax.nn.sigmoid(pg_ref[0])
    merged = gates[:, :d_model] * o_hy + gates[:, d_model:] * o_rw
    mo = _dg(merged.astype(BF16), wo_ref[...], _NN)
    h1 = _layer_norm(alpha * h_ref[0] + mo, n1g_ref[...], n1b_ref[...])
    h_out[0] = h1
    logits = _dg(h1.astype(BF16), wr_ref[...], _NN) + br_ref[...]
    col = lax.broadcasted_iota(jnp.int32, logits.shape, 1)
    logits = jnp.where(col < n_exp, logits, -1e30)
    ex = jnp.exp(logits - jnp.max(logits, axis=1, keepdims=True))
    aff = ex / jnp.sum(ex, axis=1, keepdims=True)
    aff_out[0] = jnp.transpose(aff)[:n_exp, :]


def _mix(y_f, y_b, r, k, v, a, g, y_hy, p_gate, h, k_a, r_k, lnx_g, lnx_b, w_o_hy, w_o_rw, w_out, ln_g, ln_b,
         w_router, b_router, alpha):
    bsz, seq_len, d = r.shape
    d_model = h.shape[2]
    n_exp = w_router.shape[1]
    ts = MIX_TILE
    tile = lambda w: pl.BlockSpec((1, ts, w), lambda b, i: (b, i, 0))
    const = lambda shape: pl.BlockSpec(shape, lambda b, i: (0,) * len(shape))
    vec = lambda x: x.reshape(1, -1)
    params = [vec(k_a), vec(r_k), vec(lnx_g), vec(lnx_b), _head_ones(d), w_o_hy.astype(BF16),
              w_o_rw.astype(BF16), w_out.astype(BF16), vec(ln_g), vec(ln_b),
              _pad_to(w_router, LANES, 1).astype(BF16), _pad_to(vec(b_router), LANES, 1)]
    return pl.pallas_call(
        functools.partial(_mix_kernel, alpha=alpha, n_exp=n_exp),
        grid=(bsz, seq_len // ts),
        in_specs=[tile(d), tile(d),
                  tile(d), tile(d), tile(d), pl.BlockSpec((1, 2, ts, d), lambda b, i: (b, 0, i, 0)),
                  tile(d), tile(d), tile(2 * d_model), tile(d_model)] + [const(x.shape) for x in params],
        out_specs=[tile(d_model), pl.BlockSpec((1, n_exp, ts), lambda b, i: (b, 0, i))],
        out_shape=[jax.ShapeDtypeStruct((bsz, seq_len, d_model), F32),
                   jax.ShapeDtypeStruct((bsz, n_exp, seq_len), F32)],
        compiler_params=pltpu.CompilerParams(dimension_semantics=("parallel", "parallel"),
                                             vmem_limit_bytes=VMEM_LIMIT),
        name="mix_out",
    )(y_f, y_b, r, k, v, a, g, y_hy, p_gate, h, *params)


LN_EPS = 1e-5
GN_EPS = 64e-5
HY_ORDER = 2
HY_TARGET = 1e-2
HY_SHORT_DECAY_PCT = 0.3
HY_LONG_DECAY_PCT = 1.5
N_EXPERTS = 16
EC_CAPACITY = 2


def _layer_norm(x, g, b):
    mu = jnp.mean(x, -1, keepdims=True)
    var = jnp.mean(jnp.square(x - mu), -1, keepdims=True)
    return (x - mu) * lax.rsqrt(var + LN_EPS) * g + b


def _filter_mlp_kernel(zf_ref, tf_ref, zr_ref, tr_ref, w1_ref, b1_ref, w2_ref, b2_ref, w3_ref, fr_ref, dl_ref,
                       pos_ref, neg_ref, ss_ref):
    fr = fr_ref[...]
    total = 0.0
    for d, (z_ref, t_ref, o_ref) in enumerate([(zf_ref, tf_ref, pos_ref), (zr_ref, tr_ref, neg_ref)]):
        a = jnp.sin(fr * (_dg(z_ref[...].astype(BF16), w1_ref[...], _NN) + b1_ref[...]))
        for i in range(w2_ref.shape[0]):
            a = jnp.sin(fr * (_dg(a.astype(BF16), w2_ref[i], _NN) + b2_ref[i]))
        filt = _dg(a.astype(BF16), w3_ref[d], _NN) * jnp.exp(-t_ref[...] * dl_ref[...])
        o_ref[...] = filt
        total = total + jnp.sum(filt * filt, axis=0, keepdims=True)
    ss_ref[0] = jnp.broadcast_to(total, ss_ref.shape[1:])


def _hyena_filters(seq_len, w1, b1, w2, b2, w3, freq, d_hy):
    emb, fh = w1.shape
    bands = (emb - 1) // 2
    width = w3.shape[1] // 2
    t = jnp.linspace(0.0, 1.0, seq_len, dtype=F32)[:, None]
    w = 2.0 * math.pi * jnp.arange(seq_len, dtype=F32)[:, None] / seq_len
    f = jnp.linspace(1e-4, bands - 1, bands, dtype=F32)[None, :]
    z = _pad_to(jnp.concatenate([t, jnp.cos(f * w), -jnp.sin(f * w)], axis=-1), LANES, 1)
    max_decay = math.log(HY_TARGET) / HY_SHORT_DECAY_PCT
    min_decay = math.log(HY_TARGET) / HY_LONG_DECAY_PCT
    deltas = jnp.abs(jnp.linspace(min_decay, max_decay, d_hy, dtype=F32))
    tl = ROW_TILE
    n_tiles = seq_len // tl
    full = lambda x: pl.BlockSpec(x.shape, lambda i: (0,) * x.ndim)
    w3d = jnp.transpose(w3.reshape(fh, HY_ORDER, 2, d_hy), (2, 0, 1, 3)).reshape(2, fh, width)
    params = [_pad_to(w1, LANES, 0).astype(BF16), b1.reshape(1, fh), w2.astype(BF16),
              b2.reshape(b2.shape[0], 1, fh), w3d.astype(BF16), freq.reshape(1, fh),
              jnp.tile(deltas, width // d_hy).reshape(1, width)]
    rows = lambda w: pl.BlockSpec((tl, w), lambda i: (i, 0))
    out = jax.ShapeDtypeStruct((seq_len, width), F32)
    pos, neg, ss = pl.pallas_call(
        _filter_mlp_kernel,
        grid=(n_tiles,),
        in_specs=[rows(LANES), rows(1), rows(LANES), rows(1)] + [full(x) for x in params],
        out_specs=[rows(width), rows(width), pl.BlockSpec((1, 8, width), lambda i: (i, 0, 0))],
        out_shape=[out, out, jax.ShapeDtypeStruct((n_tiles, 8, width), F32)],
        compiler_params=pltpu.CompilerParams(dimension_semantics=("parallel",)),
        name="filter_mlp",
    )(z, t, z[::-1], t[::-1], *params)
    energy = jnp.sum(ss[:, 0, :], axis=0).reshape(HY_ORDER, d_hy)
    shape = (seq_len, HY_ORDER, d_hy)
    return pos.reshape(shape), neg.reshape(shape), lax.rsqrt(energy + 1e-12)


def _hyena_branch(p, conv_w, conv_b, pos, neg, scale, bias):
    seq_len = p.shape[1]
    v, x1, x2 = _shortconv(p, conv_w, conv_b)
    z = _long_conv_gated(v, x1, *_filter_spectrum(pos[:, 0], neg[:, 0], scale[0], seq_len), bias[0])
    return _long_conv_gated(z, x2, *_filter_spectrum(pos[:, 1], neg[:, 1], scale[1], seq_len), bias[1])


def kernel(x, ln0_g, ln0_b, w_in, hy_conv_w, hy_conv_b, hy_ffn_w1, hy_ffn_b1, hy_ffn_w2, hy_ffn_b2,
           hy_ffn_w3, hy_freq, hy_bias, rw_mu, rw_w0, rw_w2, rw_a0, rw_a2, rw_v0, rw_v1, rw_v2, rw_g2,
           rw_k_k, rw_k_a, rw_r_k, rw_lnx_g, rw_lnx_b, w_o_hy, w_o_rw, w_out, ln1_g, ln1_b,
           w_router, b_router, w_gate, w_up, w_down, ln2_g, ln2_b):
    bsz, seq_len, d_model = x.shape
    depth = w_in.shape[0]
    d_hy = hy_bias.shape[-1]
    d_rw = rw_w0.shape[-1]
    p_hy = 3 * d_hy
    p_rw = rw_mu.shape[-1]
    lora = p_rw - 3 * d_rw
    lora_pad = _round_up(lora, LANES)
    widths = (p_hy, 3 * d_rw, lora_pad, 2 * d_model)
    alpha = (2 * depth) ** 0.25
    cap = EC_CAPACITY * seq_len // N_EXPERTS
    tokens = bsz * seq_len
    h = _ln_rows(x.reshape(tokens, d_model), ln0_g, ln0_b).reshape(bsz, seq_len, d_model)
    v_first = None
    wg_all, wu_all, wd_all = w_gate.astype(BF16), w_up.astype(BF16), w_down.astype(BF16)
    for l in range(depth):
        c0, c1, c2 = p_hy, p_hy + 3 * d_rw, p_hy + p_rw
        w = jnp.concatenate([w_in[l][:, :c1], _pad_to(w_in[l][:, c1:c2], lora_pad, 1), w_in[l][:, c2:]], axis=1)
        pa, pb, pl_, pg = [t.reshape(bsz, seq_len, -1) for t in _proj(h.reshape(tokens, d_model), w, widths)]
        pos, neg, scale = _hyena_filters(seq_len, hy_ffn_w1[l], hy_ffn_b1[l], hy_ffn_w2[l], hy_ffn_b2[l],
                                         hy_ffn_w3[l], hy_freq[l], d_hy)
        y_hy = _hyena_branch(pa, hy_conv_w[l], hy_conv_b[l], pos, neg, scale, hy_bias[l])
        v_res = None if l == 0 else (rw_v0[l - 1], rw_v1[l - 1], rw_v2[l - 1])
        mu = rw_mu[l]
        r, k, v, kk, lw, a, g = _rwkv_prep(pb, pl_, v_first, v_res, mu[:, :3 * d_rw],
                                           _pad_to(mu[:, 3 * d_rw:], lora_pad, 1), rw_w0[l], rw_w2[l],
                                           rw_a0[l], rw_a2[l], rw_g2[l], rw_k_k[l])
        if l == 0:
            v_first = v
        y_f, y_b = wkv7(r, k, v, kk, lw, a, rw_k_a[l])
        h, aff_t = _mix(y_f, y_b, r, k, v, a, g, y_hy, pg, h, rw_k_a[l], rw_r_k[l], rw_lnx_g[l], rw_lnx_b[l],
                        w_o_hy[l], w_o_rw[l], w_out[l], ln1_g[l], ln1_b[l], w_router[l], b_router[l], alpha)
        idx, gate = _route(aff_t, cap)
        h = _moe_block(h, idx, gate, wg_all, wu_all, wd_all, l, ln2_g[l], ln2_b[l], alpha)
    return h
```

```python
import functools
import math

import jax
import jax.numpy as jnp
import numpy as np
from jax import lax
from jax.experimental import pallas as pl
from jax.experimental.pallas import tpu as pltpu

F32 = jnp.float32
BF16 = jnp.bfloat16

RW_HEAD = 64
LANES = 128
MXU_WIDTH = 256
WKV_CHUNK = 64
WKV_CHUNKS_PER_STEP = 8
VMEM_LIMIT = 56 * 1024 * 1024


def _split(x):
    hi = x.astype(BF16)
    lo = (x - hi.astype(F32)).astype(BF16)
    return hi, lo


def _dg(a, b, dims):
    return lax.dot_general(a, b, (dims, ((), ())), preferred_element_type=F32)


def _mm(a, b, dims=((1,), (0,))):
    return _dg(a.astype(BF16), b.astype(BF16), dims)


_NN = ((1,), (0,))
_NT = ((1,), (1,))
_TN = ((0,), (0,))


def _wkv_operands(r, k, v, kk, lw, a, ka, reverse):
    C = WKV_CHUNK
    row = lax.broadcasted_iota(jnp.int32, (C, C), 0)
    col = lax.broadcasted_iota(jnp.int32, (C, C), 1)
    incl = (col >= row) if reverse else (col <= row)

    lw_hi, lw_lo = _split(lw)
    tri = incl.astype(BF16)
    cin = _dg(tri, lw_hi, _NN) + _dg(tri, lw_lo, _NN)
    tot = jnp.sum(lw, axis=0, keepdims=True)
    half = 0.5 * tot
    e0 = jnp.exp(half)
    e1 = jnp.exp(cin - half)
    e2 = jnp.exp(half - cin)
    ew = jnp.exp(-lw)
    kdir = k * (1.0 + (a - 1.0) * ka)
    rt = r * e1
    bt = kk * (e1 * ew)
    kt = kdir * e2
    at = -(kk * a) * e2
    r0 = rt * e0
    b0 = bt * e0
    kh = kt * e0
    ah = at * e0
    return dict(bt=bt, rt=rt, kt=kt, at=at, b0=b0, r0=r0, kh=kh, ah=ah, e0sq=e0 * e0, v=v)


def _wkv_kernel(rf_ref, kf_ref, vf_ref, nf_ref, rb_ref, kb_ref, vb_ref, nb_ref, lwf_ref, lwb_ref,
                af_ref, ab_ref, ka_ref, yf_ref, yb_ref, h_ref):
    C = WKV_CHUNK

    @pl.when(pl.program_id(1) == 0)
    def _():
        h_ref[...] = jnp.zeros_like(h_ref)

    ka = ka_ref[...]
    n_pairs = ka.shape[1] // LANES
    outs = [yf_ref, yb_ref]

    row2 = lax.broadcasted_iota(jnp.int32, (C, 2 * C), 0)
    col2 = lax.broadcasted_iota(jnp.int32, (C, 2 * C), 1) % C
    earlier2 = [col2 < row2, col2 > row2]
    incl2 = [col2 <= row2, col2 >= row2]
    lane = lax.broadcasted_iota(jnp.int32, (1, LANES), 1)
    first = lane < RW_HEAD
    ri = lax.broadcasted_iota(jnp.int32, (LANES, LANES), 0)
    ci = lax.broadcasted_iota(jnp.int32, (LANES, LANES), 1)
    eye = ri == ci
    same_head = (ri // RW_HEAD) == (ci // RW_HEAD)

    def stack2(x):
        return jnp.concatenate([jnp.where(first, x, 0.0), jnp.where(first, 0.0, x)], axis=0)

    chains = [(d, p) for d in range(2) for p in range(n_pairs)]
    n = range(len(chains))
    hp = [h_ref[q] for q in n]
    n_sub = rf_ref.shape[1] // C
    for s in range(n_sub):
        fs = slice(s * C, (s + 1) * C)
        bs = slice((n_sub - 1 - s) * C, (n_sub - s) * C)
        rows = [fs, bs]
        ops = [_wkv_operands(rf_ref[0, fs, :], kf_ref[0, fs, :], vf_ref[0, fs, :], nf_ref[0, fs, :],
                             lwf_ref[0, 0, fs, :], af_ref[0, 0, fs, :], ka, False),
               _wkv_operands(rb_ref[0, bs, :], kb_ref[0, bs, :], vb_ref[0, bs, :], nb_ref[0, bs, :],
                             lwb_ref[0, 0, bs, :], ab_ref[0, 0, bs, :], ka, True)]

        def op(name, q):
            d, p = chains[q]
            return ops[d][name][:, p * LANES:(p + 1) * LANES]

        g = [_mm(jnp.concatenate([op("bt", q), op("rt", q)], axis=0),
                 jnp.concatenate([stack2(op("kt", q)), stack2(op("at", q))], axis=0), _NT) for q in n]
        bh = [_mm(jnp.concatenate([op("b0", q), op("r0", q)], axis=0), hp[q]) for q in n]
        v2 = [stack2(op("v", q)) for q in n]
        x = [bh[q][:C] + _mm(jnp.where(earlier2[chains[q][0]], g[q][:C, :2 * C], 0.0), v2[q]) for q in n]
        pw = [jnp.where(earlier2[chains[q][0]], g[q][:C, 2 * C:], 0.0) for q in n]
        n_steps = C.bit_length() - 1
        for i in range(n_steps):
            x = [x[q] + _mm(pw[q], stack2(x[q])) for q in n]
            if i + 1 < n_steps:
                pw = [_mm(pw[q], stack2(pw[q])) for q in n]
        for q in n:
            d, p = chains[q]
            a_r = jnp.concatenate([jnp.where(incl2[d], g[q][C:, :2 * C], 0.0),
                                   jnp.where(incl2[d], g[q][C:, 2 * C:], 0.0)], axis=1)
            outs[d][0, rows[d], p * LANES:(p + 1) * LANES] = bh[q][C:] + _mm(
                a_r, jnp.concatenate([v2[q], stack2(x[q])], axis=0))
        new = []
        for q in n:
            dg = jnp.where(eye, jnp.broadcast_to(op("e0sq", q), (LANES, LANES)), 0.0)
            hn = _mm(jnp.concatenate([op("kh", q), op("ah", q), dg], axis=0),
                     jnp.concatenate([op("v", q), x[q], hp[q]], axis=0), _TN)
            new.append(jnp.where(same_head, hn, 0.0))
        hp = new
    for q in n:
        h_ref[q] = hp[q]


def wkv7(r, k, v, kk, lw, a, k_a):
    B, S, D = r.shape
    C = WKV_CHUNK * WKV_CHUNKS_PER_STEP
    nc = S // C
    assert S % C == 0 and D % LANES == 0
    fwd = pl.BlockSpec((1, C, D), lambda b, c: (b, c, 0))
    bwd = pl.BlockSpec((1, C, D), lambda b, c: (b, nc - 1 - c, 0))
    fwd2 = pl.BlockSpec((1, 1, C, D), lambda b, c: (b, 0, c, 0))
    bwd2 = pl.BlockSpec((1, 1, C, D), lambda b, c: (b, 1, nc - 1 - c, 0))
    out = jax.ShapeDtypeStruct((B, S, D), F32)
    return pl.pallas_call(
        _wkv_kernel,
        grid=(B, nc),
        in_specs=[fwd] * 4 + [bwd] * 4 + [fwd2, bwd2, fwd2, bwd2, pl.BlockSpec((1, D), lambda b, c: (0, 0))],
        out_specs=[fwd, bwd],
        out_shape=[out, out],
        scratch_shapes=[pltpu.VMEM((2 * D // LANES, LANES, LANES), F32)],
        compiler_params=pltpu.CompilerParams(dimension_semantics=("parallel", "arbitrary")),
        name="wkv7_chunked",
    )(r, k, v, kk, r, k, v, kk, lw, lw, a, a, k_a.reshape(1, D))


FFT_N2 = 64
FFT_K1_BLOCK = 8


def _round_up(n, m):
    return (n + m - 1) // m * m


def _twice(m, axis):
    hi = jnp.asarray(m, F32).astype(BF16)
    return jnp.concatenate([hi, hi], axis=axis)


def _hilo(x):
    hi, lo = _split(x)
    return jnp.concatenate([hi, lo], axis=0)


@functools.lru_cache(maxsize=None)
def _fft_tables(seq_len):
    n = 2 * seq_len
    n2 = FFT_N2
    n1 = n // n2
    h1 = n1 // 2
    k1n = h1 + 1
    k1p = _round_up(k1n, FFT_K1_BLOCK)
    k1 = np.arange(k1n)[:, None]
    th1 = 2 * np.pi * k1 * np.arange(h1)[None, :] / n1
    f1 = np.zeros((2 * k1p, h1))
    f1[:k1n] = np.cos(th1)
    f1[k1p:k1p + k1n] = -np.sin(th1)
    m = np.arange(n2)
    th2 = 2 * np.pi * (m[None, None, :] * m[None, :, None] / n2 + m[None, None, :] * np.arange(k1n)[:, None, None] / n)
    gr, gi = np.cos(th2), -np.sin(th2)
    g = np.zeros((k1p, 2 * n2, 2 * n2))
    g[:k1n] = np.block([[gr, -gi], [gi, gr]])
    gt = np.transpose(g, (0, 2, 1))
    coef = np.full((k1n,), 2.0)
    coef[0] = 1.0
    coef[-1] = 1.0
    th3 = th1.T
    m3 = np.zeros((h1, 2 * k1p))
    m3[:, :k1n] = coef * np.cos(th3) / n
    m3[:, k1p:k1p + k1n] = -coef * np.sin(th3) / n
    th1f = 2 * np.pi * k1 * np.arange(n1)[None, :] / n1
    f1_full = np.zeros((2 * k1p, n1))
    f1_full[:k1n] = np.cos(th1f)
    f1_full[k1p:k1p + k1n] = -np.sin(th1f)
    return dict(n1=n1, h1=h1, k1n=k1n, k1p=k1p, f1=f1, g=g, gt=gt, m3=m3, f1_full=f1_full)


FFT_N2_BLOCK = 16


def _rows_to_lanes(ref):
    return jnp.concatenate([ref[0, :, i, :] for i in range(FFT_N2_BLOCK)], axis=1)


def _lanes_to_rows(ref, val):
    ch = ref.shape[3]
    for i in range(FFT_N2_BLOCK):
        ref[0, :, i, :] = val[:, i * ch:(i + 1) * ch]


def _fft_s1_kernel(f_ref, x_ref, o_ref):
    _lanes_to_rows(o_ref, jnp.dot(f_ref[...], _hilo(_rows_to_lanes(x_ref)), preferred_element_type=F32))


def _fft_mid_kernel(g_ref, gt_ref, kr_ref, ki_ref, a_ref, o_ref):
    n2 = FFT_N2
    for j in range(FFT_K1_BLOCK):
        xin = jnp.concatenate([a_ref[0, 0, j], a_ref[0, 1, j]], axis=0)
        z = jnp.dot(g_ref[j], _hilo(xin), preferred_element_type=F32)
        zr, zi = z[:n2], z[n2:]
        kr, ki = kr_ref[j], ki_ref[j]
        y = jnp.concatenate([zr * kr - zi * ki, zr * ki + zi * kr], axis=0)
        b = jnp.dot(gt_ref[j], _hilo(y), preferred_element_type=F32)
        o_ref[0, 0, j] = b[:n2]
        o_ref[0, 1, j] = b[n2:]


def _fft_fwd_kernel(g_ref, a_ref, s_ref, re_ref, im_ref):
    n2 = FFT_N2
    for j in range(FFT_K1_BLOCK):
        xin = jnp.concatenate([a_ref[0, 0, j], a_ref[0, 1, j]], axis=0)
        z = jnp.dot(g_ref[j], _hilo(xin), preferred_element_type=F32) * s_ref[...]
        re_ref[j] = z[:n2]
        im_ref[j] = z[n2:]


def _fft_s3_kernel(m_ref, b_ref, u_ref, x_ref, bias_ref, o_ref):
    y = jnp.dot(m_ref[...], _hilo(_rows_to_lanes(b_ref)), preferred_element_type=F32)
    _lanes_to_rows(o_ref, _rows_to_lanes(x_ref) * (y + _rows_to_lanes(u_ref) * bias_ref[...]))


def _fft_s3_s1_kernel(m_ref, b_ref, u_ref, x_ref, bias_ref, f_ref, o_ref, a_ref):
    y = jnp.dot(m_ref[...], _hilo(_rows_to_lanes(b_ref)), preferred_element_type=F32)
    z = _rows_to_lanes(x_ref) * (y + _rows_to_lanes(u_ref) * bias_ref[...])
    _lanes_to_rows(o_ref, z)
    _lanes_to_rows(a_ref, jnp.dot(f_ref[...], _hilo(z), preferred_element_type=F32))


def _long_conv_gated(u, gate, kf_re, kf_im, bias, a=None, emit_next=False):
    bsz, seq_len, ch = u.shape
    t = _fft_tables(seq_len)
    n2, h1, k1p = FFT_N2, t["h1"], t["k1p"]
    lanes = n2 * ch
    nb = FFT_N2_BLOCK
    lt = nb * ch
    assert seq_len == h1 * n2 and ch % LANES == 0
    params = pltpu.CompilerParams(dimension_semantics=("parallel", "parallel"),
                                  vmem_limit_bytes=VMEM_LIMIT)
    spec4 = pl.BlockSpec((1, 2 * k1p, nb, ch), lambda b, j: (b, 0, j, 0))
    row = pl.BlockSpec((1, h1, nb, ch), lambda b, j: (b, 0, j, 0))
    f1 = _twice(t["f1"], 1)
    a_shape = jax.ShapeDtypeStruct((bsz, 2 * k1p, n2, ch), F32)
    if a is None:
        a = pl.pallas_call(
            _fft_s1_kernel,
            grid=(bsz, n2 // nb),
            in_specs=[pl.BlockSpec(f1.shape, lambda b, j: (0, 0)), row],
            out_specs=spec4,
            out_shape=a_shape,
            compiler_params=params, name="hyena_dft1",
        )(f1, u.reshape(bsz, h1, n2, ch))
    kb = FFT_K1_BLOCK
    g = _twice(t["g"], 2)
    gt = _twice(t["gt"], 2)
    mat_spec = pl.BlockSpec((kb,) + g.shape[1:], lambda j, b: (j, 0, 0))
    kf_spec = pl.BlockSpec((kb, n2, ch), lambda j, b: (j, 0, 0))
    blk = pl.BlockSpec((1, 2, kb, n2, ch), lambda j, b: (b, 0, j, 0, 0))
    bm = pl.pallas_call(
        _fft_mid_kernel,
        grid=(k1p // kb, bsz),
        in_specs=[mat_spec, mat_spec, kf_spec, kf_spec, blk],
        out_specs=blk,
        out_shape=jax.ShapeDtypeStruct((bsz, 2, k1p, n2, ch), F32),
        compiler_params=params, name="hyena_dft2",
    )(g, gt, kf_re, kf_im, a.reshape(bsz, 2, k1p, n2, ch))
    m3 = _twice(t["m3"], 1)
    in_specs = [pl.BlockSpec(m3.shape, lambda b, j: (0, 0)), spec4,
                row, row, pl.BlockSpec((1, lt), lambda b, j: (0, j))]
    args = (m3, bm.reshape(bsz, 2 * k1p, n2, ch), u.reshape(bsz, h1, n2, ch), gate.reshape(bsz, h1, n2, ch),
            jnp.tile(bias, n2).reshape(1, lanes))
    out_shape = jax.ShapeDtypeStruct((bsz, h1, n2, ch), F32)
    if emit_next:
        out, a_next = pl.pallas_call(
            _fft_s3_s1_kernel,
            grid=(bsz, n2 // nb),
            in_specs=in_specs + [pl.BlockSpec(f1.shape, lambda b, j: (0, 0))],
            out_specs=[row, spec4],
            out_shape=[out_shape, a_shape],
            compiler_params=params, name="hyena_dft3_dft1",
        )(*args, f1)
        return out.reshape(bsz, seq_len, ch), a_next
    out = pl.pallas_call(
        _fft_s3_kernel,
        grid=(bsz, n2 // nb),
        in_specs=in_specs,
        out_specs=row,
        out_shape=out_shape,
        compiler_params=params, name="hyena_dft3",
    )(*args)
    return out.reshape(bsz, seq_len, ch)


def _fft_s1_pair_kernel(f_ref, xa_ref, xb_ref, o_ref):
    x = jnp.concatenate([_rows_to_lanes(xa_ref), _rows_to_lanes(xb_ref)], axis=0)
    _lanes_to_rows(o_ref, jnp.dot(f_ref[...], _hilo(x), preferred_element_type=F32))


def _filter_spectrum(pos, neg_rev, scale, seq_len):
    t = _fft_tables(seq_len)
    h1, n2, k1p = t["h1"], FFT_N2, t["k1p"]
    nb, kb = FFT_N2_BLOCK, FFT_K1_BLOCK
    ch = pos.shape[-1]
    f1 = _twice(t["f1_full"], 1)
    params = pltpu.CompilerParams(dimension_semantics=("parallel", "parallel"),
                                  vmem_limit_bytes=VMEM_LIMIT)
    half = pl.BlockSpec((1, h1, nb, ch), lambda b, j: (b, 0, j, 0))
    a = pl.pallas_call(
        _fft_s1_pair_kernel,
        grid=(1, n2 // nb),
        in_specs=[pl.BlockSpec(f1.shape, lambda b, j: (0, 0)), half, half],
        out_specs=pl.BlockSpec((1, 2 * k1p, nb, ch), lambda b, j: (b, 0, j, 0)),
        out_shape=jax.ShapeDtypeStruct((1, 2 * k1p, n2, ch), F32),
        compiler_params=params, name="filter_dft1",
    )(f1, pos.reshape(1, h1, n2, ch), neg_rev.reshape(1, h1, n2, ch))
    g = _twice(t["g"], 2)
    kf_spec = pl.BlockSpec((kb, n2, ch), lambda b, j: (j, 0, 0))
    out = jax.ShapeDtypeStruct((k1p, n2, ch), F32)
    return pl.pallas_call(
        _fft_fwd_kernel,
        grid=(1, k1p // kb),
        in_specs=[pl.BlockSpec((kb,) + g.shape[1:], lambda b, j: (j, 0, 0)),
                  pl.BlockSpec((1, 2, kb, n2, ch), lambda b, j: (b, 0, j, 0, 0)),
                  pl.BlockSpec((1, ch), lambda b, j: (0, 0))],
        out_specs=[kf_spec, kf_spec],
        out_shape=[out, out],
        compiler_params=params, name="filter_dft2",
    )(g, a.reshape(1, 2, k1p, n2, ch), scale.reshape(1, ch))


ROUTE_COLS = 16
ROUTE_TOKEN_CHUNK = 512


def _route_kernel(aff_ref, idx_ref, gate_ref, key_ref, val_ref, *, cap):
    a = aff_ref[0]
    n_exp, seq_len = a.shape
    bits = pltpu.bitcast(a, jnp.int32)

    def bisect(i, t):
        cand = t | jnp.left_shift(1, 30 - i)
        cnt = jnp.sum((bits >= cand).astype(jnp.int32), axis=1, keepdims=True)
        return jnp.where(cnt >= cap, cand, t)

    thr = lax.fori_loop(0, 31, bisect, jnp.zeros((n_exp, 1), jnp.int32))
    gt = bits > thr
    eq = bits == thr

    ri = lax.broadcasted_iota(jnp.int32, (LANES, LANES), 0)
    ci = lax.broadcasted_iota(jnp.int32, (LANES, LANES), 1)
    upper = (ri < ci).astype(BF16)

    def prefix(mask):
        m = mask.astype(F32)
        run = jnp.zeros((n_exp, 1), F32)
        parts = []
        for j in range(seq_len // LANES):
            tile = m[:, j * LANES:(j + 1) * LANES]
            parts.append(run + jnp.dot(tile.astype(BF16), upper, preferred_element_type=F32))
            run = run + jnp.sum(tile, axis=1, keepdims=True)
        return jnp.concatenate(parts, axis=1), run

    eq_before, _ = prefix(eq)
    n_gt = jnp.sum(gt.astype(F32), axis=1, keepdims=True)
    sel = jnp.logical_or(gt, jnp.logical_and(eq, eq_before < cap - n_gt))
    pos, _ = prefix(sel)
    key_ref[...] = jnp.where(sel, pos, -1.0)
    val_ref[...] = a

    tok = lax.broadcasted_iota(jnp.int32, (1, seq_len), 1)
    t_hi = (tok // 64).astype(F32)
    t_lo = (tok % 64).astype(F32)
    slot = lax.broadcasted_iota(jnp.int32, (cap, ROUTE_TOKEN_CHUNK), 0).astype(F32)
    zeros = jnp.zeros((ROUTE_COLS - 5, seq_len), F32)

    def compact(e, carry):
        key = key_ref[pl.ds(e, 1), :]
        g = val_ref[pl.ds(e, 1), :]
        g_hi = g.astype(BF16).astype(F32)
        g_mid = (g - g_hi).astype(BF16).astype(F32)
        g_lo = g - g_hi - g_mid
        rows = jnp.concatenate([t_hi, t_lo, g_hi, g_mid, g_lo, zeros], axis=0).astype(BF16)
        acc = jnp.zeros((cap, ROUTE_COLS), F32)
        for j in range(seq_len // ROUTE_TOKEN_CHUNK):
            sl = slice(j * ROUTE_TOKEN_CHUNK, (j + 1) * ROUTE_TOKEN_CHUNK)
            onehot = jnp.where(key[:, sl] == slot, 1.0, 0.0).astype(BF16)
            acc = acc + _dg(onehot, rows[:, sl], _NT)
        pad = jnp.zeros((cap, LANES - ROUTE_COLS), F32)
        cols = jnp.transpose(jnp.concatenate([acc, pad], axis=1))
        idx_ref[0, pl.ds(e, 1), :] = (cols[0:1] * 64.0 + cols[1:2]).astype(jnp.int32)
        gate_ref[0, pl.ds(e, 1), :] = cols[2:3] + cols[3:4] + cols[4:5]
        return carry

    lax.fori_loop(0, n_exp, compact, 0)


def _route(aff_t, cap):
    bsz, n_exp, seq_len = aff_t.shape
    assert seq_len % ROUTE_TOKEN_CHUNK == 0 and seq_len <= 64 * 256
    out = pl.BlockSpec((1, n_exp, cap), lambda b: (b, 0, 0))
    return pl.pallas_call(
        functools.partial(_route_kernel, cap=cap),
        grid=(bsz,),
        in_specs=[pl.BlockSpec((1, n_exp, seq_len), lambda b: (b, 0, 0))],
        out_specs=[out, out],
        out_shape=[jax.ShapeDtypeStruct((bsz, n_exp, cap), jnp.int32),
                   jax.ShapeDtypeStruct((bsz, n_exp, cap), F32)],
        scratch_shapes=[pltpu.VMEM((n_exp, seq_len), F32), pltpu.VMEM((n_exp, seq_len), F32)],
        compiler_params=pltpu.CompilerParams(dimension_semantics=("parallel",),
                                             vmem_limit_bytes=VMEM_LIMIT),
        name="ec_route",
    )(aff_t)


MOE_FF_TILE = 1024
MOE_GROUP = 16
LN_ROWS = 256


def _gather_rows(h_ref, idx_ref, xb_ref, base):
    rows = [h_ref[0, pl.ds(idx_ref[0, 0, base + j], 1), :] for j in range(MOE_GROUP)]
    xb_ref[pl.ds(base, MOE_GROUP), :] = jnp.concatenate(rows, axis=0).astype(BF16)


def _scatter_rows(o_ref, idx_ref, gate_ref, y_ref, base, enabled):
    y8 = y_ref[pl.ds(base, 8), :]
    toks = [idx_ref[0, 0, base + j] for j in range(8)]
    rows = [o_ref[0, pl.ds(toks[j], 1), :] + jnp.where(enabled, gate_ref[0, 0, base + j], 0.0) * y8[j:j + 1, :]
            for j in range(8)]
    for j in range(8):
        o_ref[0, pl.ds(toks[j], 1), :] = rows[j]


def _moe_kernel(idx_ref, gate_ref, idxn_ref, idxp_ref, gatep_ref, h_ref, wg_ref, wu_ref, wd_ref,
                lg_ref, lb_ref, o_ref, xb_ref, y_ref, *, alpha, n_f):
    e = pl.program_id(1)
    f = pl.program_id(2)
    last_e = pl.num_programs(1) - 1
    last_f = n_f - 1
    cap = xb_ref.shape[1]
    seq_len = h_ref.shape[1]
    share = cap // n_f
    slot = e % 2
    other = 1 - slot
    first_step = jnp.logical_and(e == 0, f == 0)

    @pl.when(first_step)
    def _():
        o_ref[...] = jnp.zeros_like(o_ref)
        y_ref[...] = jnp.zeros_like(y_ref)

        def gather(i, carry):
            _gather_rows(h_ref, idx_ref, xb_ref.at[0], pl.multiple_of(i * MOE_GROUP, MOE_GROUP))
            return carry

        lax.fori_loop(0, cap // MOE_GROUP, gather, 0)

    xb = xb_ref[slot]
    a = jnp.dot(xb, wg_ref[0], preferred_element_type=F32)
    u = jnp.dot(xb, wu_ref[0], preferred_element_type=F32)
    act = (a * jax.nn.sigmoid(a) * u).astype(BF16)
    part = jnp.dot(act, wd_ref[0], preferred_element_type=F32)

    for i in range(share // MOE_GROUP):
        _gather_rows(h_ref, idxn_ref, xb_ref.at[other], pl.multiple_of(f * share + i * MOE_GROUP, MOE_GROUP))
    for i in range(share // 8):
        _scatter_rows(o_ref, idxp_ref, gatep_ref, y_ref.at[other], pl.multiple_of(f * share + i * 8, 8), e > 0)

    @pl.when(f == 0)
    def _():
        y_ref[slot] = part

    @pl.when(f > 0)
    def _():
        y_ref[slot] += part

    @pl.when(jnp.logical_and(e == last_e, f == last_f))
    def _():
        def scatter(i, carry):
            _scatter_rows(o_ref, idx_ref, gate_ref, y_ref.at[slot], pl.multiple_of(i * 8, 8), True)
            return carry

        lax.fori_loop(0, cap // 8, scatter, 0)

    @pl.when(jnp.logical_and(e == last_e, f == last_f))
    def _():
        def norm(i, carry):
            rows = pl.ds(pl.multiple_of(i * LN_ROWS, LN_ROWS), LN_ROWS)
            z = alpha * h_ref[0, rows, :] + o_ref[0, rows, :]
            o_ref[0, rows, :] = _layer_norm(z, lg_ref[...], lb_ref[...])
            return carry

        lax.fori_loop(0, seq_len // LN_ROWS, norm, 0)


def _moe_block(h, idx, gate, wg, wu, wd, layer, ln_g, ln_b, alpha):
    bsz, seq_len, d = h.shape
    _, n_exp, _, ff = wg.shape
    cap = idx.shape[-1]
    ft = min(MOE_FF_TILE, ff)
    assert ff % ft == 0 and seq_len % LN_ROWS == 0
    n_f = ff // ft
    assert cap % (n_f * MOE_GROUP) == 0

    def smem(shift):
        def index(b, e, f):
            return (b * n_exp + jnp.clip(e + shift, 0, n_exp - 1), 0, 0)
        return pl.BlockSpec((1, 1, cap), index, memory_space=pltpu.SMEM)

    whole = pl.BlockSpec((1, seq_len, d), lambda b, e, f: (b, 0, 0), pipeline_mode=pl.Buffered(1))
    vec = pl.BlockSpec((1, d), lambda b, e, f: (0, 0))
    idx3 = idx.reshape(bsz * n_exp, 1, cap)
    gate3 = gate.reshape(bsz * n_exp, 1, cap)
    return pl.pallas_call(
        functools.partial(_moe_kernel, alpha=alpha, n_f=n_f),
        grid=(bsz, n_exp, n_f),
        in_specs=[smem(0), smem(0), smem(1), smem(-1), smem(-1), whole,
                  pl.BlockSpec((None, 1, d, ft), lambda b, e, f: (layer, e, 0, f)),
                  pl.BlockSpec((None, 1, d, ft), lambda b, e, f: (layer, e, 0, f)),
                  pl.BlockSpec((None, 1, ft, d), lambda b, e, f: (layer, e, f, 0)),
                  vec, vec],
        out_specs=whole,
        out_shape=jax.ShapeDtypeStruct((bsz, seq_len, d), F32),
        scratch_shapes=[pltpu.VMEM((2, cap, d), BF16), pltpu.VMEM((2, cap, d), F32)],
        compiler_params=pltpu.CompilerParams(
            dimension_semantics=("arbitrary", "arbitrary", "arbitrary"), vmem_limit_bytes=VMEM_LIMIT),
        name="ec_moe",
    )(idx3, gate3, idx3, idx3, gate3, h, wg, wu, wd, ln_g.reshape(1, d), ln_b.reshape(1, d))


ROW_TILE = 512
MIX_TILE = 512


def _proj_kernel(x_ref, w_ref, *o_refs):
    xb = x_ref[...].astype(BF16)
    off = 0
    for o_ref in o_refs:
        n = o_ref.shape[1]
        o_ref[...] = jnp.dot(xb, w_ref[:, off:off + n], preferred_element_type=F32)
        off += n


def _proj(x, w, widths):
    M, K = x.shape
    tm = ROW_TILE
    assert M % tm == 0 and sum(widths) == w.shape[1] and all(n % LANES == 0 for n in widths)
    return pl.pallas_call(
        _proj_kernel,
        grid=(M // tm,),
        in_specs=[pl.BlockSpec((tm, K), lambda i: (i, 0)),
                  pl.BlockSpec(w.shape, lambda i: (0, 0), pipeline_mode=pl.Buffered(1))],
        out_specs=[pl.BlockSpec((tm, n), lambda i: (i, 0)) for n in widths],
        out_shape=[jax.ShapeDtypeStruct((M, n), F32) for n in widths],
        compiler_params=pltpu.CompilerParams(dimension_semantics=("parallel",),
                                             vmem_limit_bytes=VMEM_LIMIT),
        name="proj",
    )(x, w.astype(BF16))


def _pad_to(w, n, axis):
    pad = [(0, 0)] * w.ndim
    pad[axis] = (0, n - w.shape[axis])
    return jnp.pad(w, pad)


def _ln_kernel(x_ref, g_ref, b_ref, o_ref):
    o_ref[...] = _layer_norm(x_ref[...], g_ref[...], b_ref[...])


def _ln_rows(x, g, b):
    M, D = x.shape
    row = pl.BlockSpec((ROW_TILE, D), lambda i: (i, 0))
    vec = pl.BlockSpec((1, D), lambda i: (0, 0))
    return pl.pallas_call(
        _ln_kernel, grid=(M // ROW_TILE,), in_specs=[row, vec, vec], out_specs=row,
        out_shape=jax.ShapeDtypeStruct((M, D), F32),
        compiler_params=pltpu.CompilerParams(dimension_semantics=("parallel",)),
        name="ln_in",
    )(x, g.reshape(1, D), b.reshape(1, D))


def _halo_specs(ts, width, seq_len):
    per = ts // 8
    last = seq_len // 8 - 1
    cur = pl.BlockSpec((1, ts, width), lambda b, i: (b, i, 0))
    prev = pl.BlockSpec((1, 8, width), lambda b, i: (b, jnp.maximum(i * per - 1, 0), 0))
    nxt = pl.BlockSpec((1, 8, width), lambda b, i: (b, jnp.minimum((i + 1) * per, last), 0))
    return [cur, prev, nxt]


def _neighbours(x, prev_blk, next_blk):
    ts = x.shape[0]
    i = pl.program_id(1)
    rid = lax.broadcasted_iota(jnp.int32, x.shape, 0)
    prev_row = jnp.where(i > 0, prev_blk[7:8, :], 0.0)
    next_row = jnp.where(i < pl.num_programs(1) - 1, next_blk[0:1, :], 0.0)
    zp = jnp.where(rid == 0, prev_row, pltpu.roll(x, 1, 0))
    zn = jnp.where(rid == ts - 1, next_row, pltpu.roll(x, ts - 1, 0))
    return zp, zn


def _shortconv_kernel(p_ref, pp_ref, pn_ref, w_ref, b_ref, v_ref, x1_ref, x2_ref):
    p = p_ref[0]
    zp, zn = _neighbours(p, pp_ref[0], pn_ref[0])
    q = w_ref[0:1, :] * zp + w_ref[1:2, :] * p + w_ref[2:3, :] * zn + b_ref[...]
    d = v_ref.shape[2]
    v_ref[0] = q[:, :d]
    x1_ref[0] = q[:, d:2 * d]
    x2_ref[0] = q[:, 2 * d:]


def _shortconv(p, conv_w, conv_b):
    bsz, seq_len, width = p.shape
    d = width // 3
    ts = ROW_TILE
    out = pl.BlockSpec((1, ts, d), lambda b, i: (b, i, 0))
    return pl.pallas_call(
        _shortconv_kernel,
        grid=(bsz, seq_len // ts),
        in_specs=_halo_specs(ts, width, seq_len) + [pl.BlockSpec((3, width), lambda b, i: (0, 0)),
                                                     pl.BlockSpec((1, width), lambda b, i: (0, 0))],
        out_specs=[out, out, out],
        out_shape=[jax.ShapeDtypeStruct((bsz, seq_len, d), F32)] * 3,
        compiler_params=pltpu.CompilerParams(dimension_semantics=("parallel", "parallel")),
        name="hyena_shortconv",
    )(p, p, p, conv_w, conv_b.reshape(1, width))


def _head_sum(x, bd):
    hi, lo = _split(x)
    w = MXU_WIDTH
    parts = [_dg(hi[:, j:j + w], bd[j:j + w, j:j + w], _NN) + _dg(lo[:, j:j + w], bd[j:j + w, j:j + w], _NN)
             for j in range(0, x.shape[1], w)]
    return jnp.concatenate(parts, axis=1)


def _rwkv_prep_kernel(*refs, has_res):
    (p_ref, pp_ref, pn_ref, q_ref, qp_ref, qn_ref) = refs[:6]
    refs = refs[6:]
    if has_res:
        vf_ref, refs = refs[0], refs[1:]
    (mu_ref, mul_ref, w2_ref, w0_ref, a2_ref, a0_ref, g2_ref, kk_ref, v0_ref, v1_ref, v2_ref, bd_ref,
     r_out, k_out, v_out, kn_out, lw_out, a_out, g_out) = refs
    d = r_out.shape[2]
    p = p_ref[0]
    zp, zn = _neighbours(p, pp_ref[0], pn_ref[0])
    p = p + mu_ref[0:1, :] * (zp - p) + mu_ref[1:2, :] * (zn - p)
    q = q_ref[0]
    zp, zn = _neighbours(q, qp_ref[0], qn_ref[0])
    q = q + mul_ref[0:1, :] * (zp - q) + mul_ref[1:2, :] * (zn - q)
    r, k, v = p[:, :d], p[:, d:2 * d], p[:, 2 * d:]
    if has_res:
        mix = _dg(_dg(v.astype(BF16), v1_ref[...], _NN).astype(BF16), v2_ref[...], _NN)
        v = v + (vf_ref[0] - v) * jax.nn.sigmoid(v0_ref[...] + mix)
    wd, ad, gd = q[:, :LANES], q[:, LANES:2 * LANES], q[:, 2 * LANES:]
    wpre = w0_ref[...] + _dg(jnp.tanh(wd).astype(BF16), w2_ref[...], _NN)
    lw = -math.exp(-0.5) * jax.nn.sigmoid(wpre)
    a = jax.nn.sigmoid(a0_ref[...] + _dg(ad.astype(BF16), a2_ref[...], _NN))
    g = _dg(jax.nn.sigmoid(gd).astype(BF16), g2_ref[...], _NN)
    kq = k * kk_ref[...]
    norm = jnp.sqrt(_head_sum(kq * kq, bd_ref[...]))
    r_out[0] = r
    k_out[0] = k
    v_out[0] = v
    kn_out[0] = kq / jnp.maximum(norm, 1e-12)
    lw_out[0, 0] = lw[:, :d]
    lw_out[0, 1] = lw[:, d:]
    a_out[0, 0] = a[:, :d]
    a_out[0, 1] = a[:, d:]
    g_out[0] = g


def _block_diag2(m):
    z = jnp.zeros_like(m[0])
    return jnp.concatenate([jnp.concatenate([m[0], z], axis=1), jnp.concatenate([z, m[1]], axis=1)], axis=0)


def _head_ones(d):
    i = np.arange(d) // RW_HEAD
    return jnp.asarray(i[:, None] == i[None, :], BF16)


def _rwkv_prep(p_rkv, p_lora, v_first, v_res, mu_rkv, mu_lora, w0, w2, a0, a2, g2, k_k):
    bsz, seq_len, w3 = p_rkv.shape
    d = w3 // 3
    wl = p_lora.shape[2]
    ts = ROW_TILE
    assert 2 * w2.shape[1] == LANES and 2 * a2.shape[1] == LANES
    has_res = v_res is not None
    const = lambda shape: pl.BlockSpec(shape, lambda b, i: (0,) * len(shape))
    tile = pl.BlockSpec((1, ts, d), lambda b, i: (b, i, 0))
    pair = pl.BlockSpec((1, 2, ts, d), lambda b, i: (b, 0, i, 0))
    if has_res:
        v0, v1, v2 = v_res
        rank = _round_up(v1.shape[1], LANES)
        v0, v1, v2 = v0.reshape(1, d), _pad_to(v1, rank, 1).astype(BF16), _pad_to(v2, rank, 0).astype(BF16)
    else:
        v0, v1, v2 = jnp.zeros((1, d), F32), jnp.zeros((d, LANES), BF16), jnp.zeros((LANES, d), BF16)
    params = [mu_rkv, mu_lora, _block_diag2(w2).astype(BF16), w0.reshape(1, 2 * d),
              _block_diag2(a2).astype(BF16), a0.reshape(1, 2 * d),
              _pad_to(g2, wl - 2 * LANES, 0).astype(BF16), k_k.reshape(1, d), v0, v1, v2, _head_ones(d)]
    args = [p_rkv, p_rkv, p_rkv, p_lora, p_lora, p_lora] + ([v_first] if has_res else []) + params
    tok = jax.ShapeDtypeStruct((bsz, seq_len, d), F32)
    two = jax.ShapeDtypeStruct((bsz, 2, seq_len, d), F32)
    return pl.pallas_call(
        functools.partial(_rwkv_prep_kernel, has_res=has_res),
        grid=(bsz, seq_len // ts),
        in_specs=(_halo_specs(ts, w3, seq_len) + _halo_specs(ts, wl, seq_len) + ([tile] if has_res else [])
                  + [const(x.shape) for x in params]),
        out_specs=[tile, tile, tile, tile, pair, pair, tile],
        out_shape=[tok, tok, tok, tok, two, two, tok],
        compiler_params=pltpu.CompilerParams(dimension_semantics=("parallel", "parallel"),
                                             vmem_limit_bytes=VMEM_LIMIT),
        name="rwkv_prep",
    )(*args)


def _mix_kernel(yf_ref, yb_ref, r_ref, k_ref, v_ref, a_ref, g_ref, yh_ref, pg_ref, h_ref,
                ka_ref, rk_ref, lg_ref, lb_ref, bd_ref, why_ref, wrw_ref, wo_ref, n1g_ref, n1b_ref,
                wr_ref, br_ref, h_out, aff_out, *, alpha, n_exp):
    d_model = h_ref.shape[2]
    bd = bd_ref[...]
    inv = 1.0 / RW_HEAD
    y = yf_ref[0] + yb_ref[0]
    mu = _head_sum(y, bd) * inv
    dy = y - mu
    var = _head_sum(dy * dy, bd) * inv
    yn = dy * lax.rsqrt(var + GN_EPS) * lg_ref[...] + lb_ref[...]
    k = k_ref[0]
    ka = ka_ref[...]
    k_sum = k * (1.0 + (a_ref[0, 0] - 1.0) * ka) + k * (1.0 + (a_ref[0, 1] - 1.0) * ka)
    bonus = _head_sum(r_ref[0] * k_sum * rk_ref[...], bd) * v_ref[0]
    y_rw = (yn + bonus) * g_ref[0]
    o_hy = _dg(yh_ref[0].astype(BF16), why_ref[...], _NN)
    o_rw = _dg(y_rw.astype(BF16), wrw_ref[...], _NN)
    gates = jax.nn.sigmoid(pg_ref[0])
    merged = gates[:, :d_model] * o_hy + gates[:, d_model:] * o_rw
    mo = _dg(merged.astype(BF16), wo_ref[...], _NN)
    h1 = _layer_norm(alpha * h_ref[0] + mo, n1g_ref[...], n1b_ref[...])
    h_out[0] = h1
    logits = _dg(h1.astype(BF16), wr_ref[...], _NN) + br_ref[...]
    col = lax.broadcasted_iota(jnp.int32, logits.shape, 1)
    logits = jnp.where(col < n_exp, logits, -1e30)
    ex = jnp.exp(logits - jnp.max(logits, axis=1, keepdims=True))
    aff = ex / jnp.sum(ex, axis=1, keepdims=True)
    aff_out[0] = jnp.transpose(aff)[:n_exp, :]


def _mix(y_f, y_b, r, k, v, a, g, y_hy, p_gate, h, k_a, r_k, lnx_g, lnx_b, w_o_hy, w_o_rw, w_out, ln_g, ln_b,
         w_router, b_router, alpha):
    bsz, seq_len, d = r.shape
    d_model = h.shape[2]
    n_exp = w_router.shape[1]
    ts = MIX_TILE
    tile = lambda w: pl.BlockSpec((1, ts, w), lambda b, i: (b, i, 0))
    const = lambda shape: pl.BlockSpec(shape, lambda b, i: (0,) * len(shape))
    vec = lambda x: x.reshape(1, -1)
    params = [vec(k_a), vec(r_k), vec(lnx_g), vec(lnx_b), _head_ones(d), w_o_hy.astype(BF16),
              w_o_rw.astype(BF16), w_out.astype(BF16), vec(ln_g), vec(ln_b),
              _pad_to(w_router, LANES, 1).astype(BF16), _pad_to(vec(b_router), LANES, 1)]
    return pl.pallas_call(
        functools.partial(_mix_kernel, alpha=alpha, n_exp=n_exp),
        grid=(bsz, seq_len // ts),
        in_specs=[tile(d), tile(d),
                  tile(d), tile(d), tile(d), pl.BlockSpec((1, 2, ts, d), lambda b, i: (b, 0, i, 0)),
                  tile(d), tile(d), tile(2 * d_model), tile(d_model)] + [const(x.shape) for x in params],
        out_specs=[tile(d_model), pl.BlockSpec((1, n_exp, ts), lambda b, i: (b, 0, i))],
        out_shape=[jax.ShapeDtypeStruct((bsz, seq_len, d_model), F32),
                   jax.ShapeDtypeStruct((bsz, n_exp, seq_len), F32)],
        compiler_params=pltpu.CompilerParams(dimension_semantics=("parallel", "parallel"),
                                             vmem_limit_bytes=VMEM_LIMIT),
        name="mix_out",
    )(y_f, y_b, r, k, v, a, g, y_hy, p_gate, h, *params)


LN_EPS = 1e-5
GN_EPS = 64e-5
HY_ORDER = 2
HY_TARGET = 1e-2
HY_SHORT_DECAY_PCT = 0.3
HY_LONG_DECAY_PCT = 1.5
N_EXPERTS = 16
EC_CAPACITY = 2


def _layer_norm(x, g, b):
    mu = jnp.mean(x, -1, keepdims=True)
    var = jnp.mean(jnp.square(x - mu), -1, keepdims=True)
    return (x - mu) * lax.rsqrt(var + LN_EPS) * g + b


def _filter_mlp_kernel(zf_ref, tf_ref, zr_ref, tr_ref, w1_ref, b1_ref, w2_ref, b2_ref, w3_ref, fr_ref, dl_ref,
                       pos_ref, neg_ref, ss_ref):
    fr = fr_ref[...]
    total = 0.0
    for d, (z_ref, t_ref, o_ref) in enumerate([(zf_ref, tf_ref, pos_ref), (zr_ref, tr_ref, neg_ref)]):
        a = jnp.sin(fr * (_dg(z_ref[...].astype(BF16), w1_ref[...], _NN) + b1_ref[...]))
        for i in range(w2_ref.shape[0]):
            a = jnp.sin(fr * (_dg(a.astype(BF16), w2_ref[i], _NN) + b2_ref[i]))
        filt = _dg(a.astype(BF16), w3_ref[d], _NN) * jnp.exp(-t_ref[...] * dl_ref[...])
        o_ref[...] = filt
        total = total + jnp.sum(filt * filt, axis=0, keepdims=True)
    ss_ref[0] = jnp.broadcast_to(total, ss_ref.shape[1:])


def _hyena_filters(seq_len, w1, b1, w2, b2, w3, freq, d_hy):
    emb, fh = w1.shape
    bands = (emb - 1) // 2
    width = w3.shape[1] // 2
    t = jnp.linspace(0.0, 1.0, seq_len, dtype=F32)[:, None]
    w = 2.0 * math.pi * jnp.arange(seq_len, dtype=F32)[:, None] / seq_len
    f = jnp.linspace(1e-4, bands - 1, bands, dtype=F32)[None, :]
    z = _pad_to(jnp.concatenate([t, jnp.cos(f * w), -jnp.sin(f * w)], axis=-1), LANES, 1)
    max_decay = math.log(HY_TARGET) / HY_SHORT_DECAY_PCT
    min_decay = math.log(HY_TARGET) / HY_LONG_DECAY_PCT
    deltas = jnp.abs(jnp.linspace(min_decay, max_decay, d_hy, dtype=F32))
    tl = ROW_TILE
    n_tiles = seq_len // tl
    full = lambda x: pl.BlockSpec(x.shape, lambda i: (0,) * x.ndim)
    w3d = jnp.transpose(w3.reshape(fh, HY_ORDER, 2, d_hy), (2, 0, 1, 3)).reshape(2, fh, width)
    params = [_pad_to(w1, LANES, 0).astype(BF16), b1.reshape(1, fh), w2.astype(BF16),
              b2.reshape(b2.shape[0], 1, fh), w3d.astype(BF16), freq.reshape(1, fh),
              jnp.tile(deltas, width // d_hy).reshape(1, width)]
    rows = lambda w: pl.BlockSpec((tl, w), lambda i: (i, 0))
    out = jax.ShapeDtypeStruct((seq_len, width), F32)
    pos, neg, ss = pl.pallas_call(
        _filter_mlp_kernel,
        grid=(n_tiles,),
        in_specs=[rows(LANES), rows(1), rows(LANES), rows(1)] + [full(x) for x in params],
        out_specs=[rows(width), rows(width), pl.BlockSpec((1, 8, width), lambda i: (i, 0, 0))],
        out_shape=[out, out, jax.ShapeDtypeStruct((n_tiles, 8, width), F32)],
        compiler_params=pltpu.CompilerParams(dimension_semantics=("parallel",)),
        name="filter_mlp",
    )(z, t, z[::-1], t[::-1], *params)
    energy = jnp.sum(ss[:, 0, :], axis=0).reshape(HY_ORDER, d_hy)
    shape = (seq_len, HY_ORDER, d_hy)
    return pos.reshape(shape), neg.reshape(shape), lax.rsqrt(energy + 1e-12)


def _hyena_branch(p, conv_w, conv_b, pos, neg, scale, bias):
    seq_len = p.shape[1]
    v, x1, x2 = _shortconv(p, conv_w, conv_b)
    z, a = _long_conv_gated(v, x1, *_filter_spectrum(pos[:, 0], neg[:, 0], scale[0], seq_len), bias[0],
                            emit_next=True)
    return _long_conv_gated(z, x2, *_filter_spectrum(pos[:, 1], neg[:, 1], scale[1], seq_len), bias[1], a=a)


def kernel(x, ln0_g, ln0_b, w_in, hy_conv_w, hy_conv_b, hy_ffn_w1, hy_ffn_b1, hy_ffn_w2, hy_ffn_b2,
           hy_ffn_w3, hy_freq, hy_bias, rw_mu, rw_w0, rw_w2, rw_a0, rw_a2, rw_v0, rw_v1, rw_v2, rw_g2,
           rw_k_k, rw_k_a, rw_r_k, rw_lnx_g, rw_lnx_b, w_o_hy, w_o_rw, w_out, ln1_g, ln1_b,
           w_router, b_router, w_gate, w_up, w_down, ln2_g, ln2_b):
    bsz, seq_len, d_model = x.shape
    depth = w_in.shape[0]
    d_hy = hy_bias.shape[-1]
    d_rw = rw_w0.shape[-1]
    p_hy = 3 * d_hy
    p_rw = rw_mu.shape[-1]
    lora = p_rw - 3 * d_rw
    lora_pad = _round_up(lora, LANES)
    widths = (p_hy, 3 * d_rw, lora_pad, 2 * d_model)
    alpha = (2 * depth) ** 0.25
    cap = EC_CAPACITY * seq_len // N_EXPERTS
    tokens = bsz * seq_len
    h = _ln_rows(x.reshape(tokens, d_model), ln0_g, ln0_b).reshape(bsz, seq_len, d_model)
    v_first = None
    wg_all, wu_all, wd_all = w_gate.astype(BF16), w_up.astype(BF16), w_down.astype(BF16)
    for l in range(depth):
        c0, c1, c2 = p_hy, p_hy + 3 * d_rw, p_hy + p_rw
        w = jnp.concatenate([w_in[l][:, :c1], _pad_to(w_in[l][:, c1:c2], lora_pad, 1), w_in[l][:, c2:]], axis=1)
        pa, pb, pl_, pg = [t.reshape(bsz, seq_len, -1) for t in _proj(h.reshape(tokens, d_model), w, widths)]
        pos, neg, scale = _hyena_filters(seq_len, hy_ffn_w1[l], hy_ffn_b1[l], hy_ffn_w2[l], hy_ffn_b2[l],
                                         hy_ffn_w3[l], hy_freq[l], d_hy)
        y_hy = _hyena_branch(pa, hy_conv_w[l], hy_conv_b[l], pos, neg, scale, hy_bias[l])
        v_res = None if l == 0 else (rw_v0[l - 1], rw_v1[l - 1], rw_v2[l - 1])
        mu = rw_mu[l]
        r, k, v, kk, lw, a, g = _rwkv_prep(pb, pl_, v_first, v_res, mu[:, :3 * d_rw],
                                           _pad_to(mu[:, 3 * d_rw:], lora_pad, 1), rw_w0[l], rw_w2[l],
                                           rw_a0[l], rw_a2[l], rw_g2[l], rw_k_k[l])
        if l == 0:
            v_first = v
        y_f, y_b = wkv7(r, k, v, kk, lw, a, rw_k_a[l])
        h, aff_t = _mix(y_f, y_b, r, k, v, a, g, y_hy, pg, h, rw_k_a[l], rw_r_k[l], rw_lnx_g[l], rw_lnx_b[l],
                        w_o_hy[l], w_o_rw[l], w_out[l], ln1_g[l], ln1_b[l], w_router[l], b_router[l], alpha)
        idx, gate = _route(aff_t, cap)
        h = _moe_block(h, idx, gate, wg_all, wu_all, wd_all, l, ln2_g[l], ln2_b[l], alpha)
    return h
```
